```python
import jax, jax.numpy as jnp
from jax import lax
import numpy as np

D_MODEL = 1024
BATCH = 8
SEQ = 2048
DEPTH = 4
DEC_BATCH = 128
DEC_SEQ = 1
PAST_LEN = 2048
PAGE_SIZE = 128

A_HEADS = 4
A_DK = 128
A_DV = 128
A_WIDTH = A_HEADS * A_DK
A_CHUNK = 64
B_WIDTH = 512
B_BLOCKS = 8
B_BW = B_WIDTH // B_BLOCKS
CONV_W = 4
LRU_C = 8.0
C_HEADS = 8
C_KV_HEADS = 2
C_GROUP = C_HEADS // C_KV_HEADS
C_DH = 64
C_WIDTH = C_HEADS * C_DH
CMP_LEN = 32
CMP_STRIDE = 16
SLC_BLK = 64
SLC_TOPN = 16
WINDOW = 512
Q_BLK = 128
SLC_QBLK = 32
ROPE_THETA = 10000.0
ATT_SCALE = C_DH ** -0.5
N_BRANCH = 3
BR_WIDTH = 512
C_KV_COLS = 6 * C_KV_HEADS * C_DH
IN_SIZES = (A_WIDTH, A_WIDTH, A_WIDTH, A_WIDTH, B_WIDTH, B_WIDTH, C_WIDTH, C_KV_COLS, 3 * C_HEADS, C_WIDTH, N_BRANCH * D_MODEL)
N_IN = sum(IN_SIZES)
ALPHA = (2 * DEPTH) ** 0.25
BETA = (8 * DEPTH) ** -0.25
LN_EPS = 1e-5
NEG_INF = -1e30
FORCED_SCORE = 1e9

kernel_name = 'hybrid_hgrn2_rglru_nsa_decoder_step'


def layer_norm(x, g, b):
    xf = x.astype(jnp.float32)
    mu = jnp.mean(xf, axis=-1, keepdims=True)
    var = jnp.mean(jnp.square(xf - mu), axis=-1, keepdims=True)
    return ((xf - mu) * lax.rsqrt(var + LN_EPS) * g + b).astype(x.dtype)


def group_rms(o):
    of = o.astype(jnp.float32)
    return of * lax.rsqrt(jnp.mean(of * of, axis=-1, keepdims=True) + LN_EPS)


def rope(x, pos):
    d = x.shape[-1]
    inv = ROPE_THETA ** (-jnp.arange(0, d, 2, dtype=jnp.float32) / d)
    ang = pos.astype(jnp.float32)[:, None] * inv[None, :]
    cos = jnp.cos(ang)[:, None, :]
    sin = jnp.sin(ang)[:, None, :]
    xf = x.astype(jnp.float32)
    x1, x2 = xf[..., : d // 2], xf[..., d // 2:]
    return jnp.concatenate([x1 * cos - x2 * sin, x2 * cos + x1 * sin], axis=-1).astype(x.dtype)


def masked_softmax(s, mask):
    s = jnp.where(mask, s.astype(jnp.float32), NEG_INF)
    return jnp.where(mask, jax.nn.softmax(s, axis=-1), 0.0)


def hgrn2_chunked(q, k, v, log_f, s0):
    B, T, H, DK = q.shape
    n = T // A_CHUNK

    def to_chunks(a):
        return a.reshape(B, n, A_CHUNK, H, a.shape[-1]).transpose(1, 0, 3, 2, 4)

    causal = jnp.tril(jnp.ones((A_CHUNK, A_CHUNK), bool))[:, :, None]

    def step(S, xs):
        qi, ki, vi, gi = xs
        b = jnp.cumsum(gi, axis=2)
        inter = jnp.einsum('bhtk,bhkv->bhtv', qi * jnp.exp(b), S)
        diff = b[:, :, :, None, :] - b[:, :, None, :, :]
        decay = jnp.exp(jnp.where(causal, diff, -jnp.inf))
        att = jnp.sum(qi[:, :, :, None, :] * ki[:, :, None, :, :] * decay, axis=-1)
        intra = jnp.einsum('bhts,bhsv->bhtv', att, vi)
        b_last = b[:, :, -1:, :]
        S_new = jnp.exp(b_last[:, :, 0, :])[..., None] * S + jnp.einsum('bhsk,bhsv->bhkv', ki * jnp.exp(b_last - b), vi)
        return S_new.astype(S.dtype), inter + intra

    S, o = lax.scan(step, s0, (to_chunks(q), to_chunks(k), to_chunks(v), to_chunks(log_f)))
    return o.transpose(1, 0, 3, 2, 4).reshape(B, T, H, -1), S


def hgrn2_recurrent(q, k, v, log_f, s0):
    def step(S, xs):
        qt, kt, vt, gt = xs
        S_new = (jnp.exp(gt)[..., None] * S + kt[..., :, None] * vt[..., None, :]).astype(S.dtype)
        return S_new, jnp.einsum('bhk,bhkv->bhv', qt, S_new)

    S, o = lax.scan(step, s0, (q.swapaxes(0, 1), k.swapaxes(0, 1), v.swapaxes(0, 1), log_f.swapaxes(0, 1)))
    return o.swapaxes(0, 1), S


def _lin_combine(left, right):
    a1, b1 = left
    a2, b2 = right
    return a1 * a2, a2 * b1 + b2


def rglru_branch(xb, z, conv0, h0, conv_w, conv_b, ga_w, ga_b, gx_w, gx_b, lam):
    B, T, W = xb.shape
    xpad = jnp.concatenate([conv0.astype(xb.dtype), xb], axis=1)
    u = conv_b + sum(xpad[:, j:j + T] * conv_w[j] for j in range(CONV_W))
    ub = u.reshape(B, T, B_BLOCKS, B_BW)
    r = jax.nn.sigmoid(jnp.einsum('btnd,nde->btne', ub, ga_w).reshape(B, T, W) + ga_b)
    ig = jax.nn.sigmoid(jnp.einsum('btnd,nde->btne', ub, gx_w).reshape(B, T, W) + gx_b)
    log_a = -LRU_C * r.astype(jnp.float32) * jax.nn.softplus(-lam.astype(jnp.float32))
    a = jnp.exp(log_a)
    bterm = jnp.sqrt(-jnp.expm1(2.0 * log_a)) * (ig * u).astype(jnp.float32)
    bterm = bterm.at[:, 0].add(a[:, 0] * h0.astype(jnp.float32))
    _, hs = lax.associative_scan(_lin_combine, (a, bterm), axis=1)
    y = hs.astype(xb.dtype) * jax.nn.silu(z)
    return y, hs[:, -1].astype(h0.dtype), xpad[:, T:]


def compress(kv_raw, pos_emb, w1, w2):
    B, L = kv_raw.shape[:2]
    n_cmp = (L - CMP_LEN) // CMP_STRIDE + 1
    idx = jnp.arange(n_cmp)[:, None] * CMP_STRIDE + jnp.arange(CMP_LEN)[None, :]
    blk = kv_raw[:, idx] + pos_emb.transpose(1, 0, 2)[None, None, :, :, None, :]
    blk = blk.transpose(0, 1, 3, 4, 2, 5).reshape(B, n_cmp, 2, C_KV_HEADS, CMP_LEN * C_DH)
    hid = jax.nn.gelu(jnp.einsum('bnchi,cio->bncho', blk, w1))
    out = jnp.einsum('bncho,cop->bnchp', hid, w2)
    return out[:, :, 0], out[:, :, 1]


def cmp_attention(q, pos, kc, vc):
    B, Tq = q.shape[:2]
    qg = q.reshape(B, Tq, C_KV_HEADS, C_GROUP, C_DH)
    s = jnp.einsum('btghd,bcgd->bghtc', qg, kc) * ATT_SCALE
    n_cmp = kc.shape[1]
    end = jnp.arange(n_cmp) * CMP_STRIDE + CMP_LEN - 1
    p = masked_softmax(s, end[None, :] <= pos[:, None])
    o = jnp.einsum('bghtc,bcgd->btghd', p.astype(vc.dtype), vc)
    return o.reshape(B, Tq, C_HEADS, C_DH), p


def select_blocks(p_cmp, pos, n_slc):
    n_cmp = p_cmp.shape[-1]
    cs = jnp.arange(n_cmp) * CMP_STRIDE
    ss = jnp.arange(n_slc) * SLC_BLK
    overlap = ((cs[:, None] < ss[None, :] + SLC_BLK) & (cs[:, None] + CMP_LEN > ss[None, :])).astype(jnp.float32)
    imp = jnp.einsum('bghtc,cj->bgtj', p_cmp, overlap)
    cur = pos // SLC_BLK
    j = jnp.arange(n_slc)[None, :]
    valid = j <= cur[:, None]
    forced = (j == 0) | (j == cur[:, None]) | (j == cur[:, None] - 1)
    score = jnp.where(forced, FORCED_SCORE, jnp.where(valid, imp, -1.0))
    _, idx = lax.top_k(score, min(SLC_TOPN, n_slc))
    return idx


def to_blocks(k, n_slc):
    B, L = k.shape[:2]
    kp = jnp.pad(k, ((0, 0), (0, n_slc * SLC_BLK - L), (0, 0), (0, 0)))
    return kp.reshape(B, n_slc, SLC_BLK, C_KV_HEADS, C_DH).transpose(0, 3, 1, 2, 4)


def slc_attention(q, pos, idx, ks_blk, vs_blk):
    B, Tq = q.shape[:2]
    bi = jnp.arange(B)[:, None, None, None]
    gi = jnp.arange(C_KV_HEADS)[None, :, None, None]
    n = idx.shape[-1]
    ksel = ks_blk[bi, gi, idx].reshape(B, C_KV_HEADS, Tq, n * SLC_BLK, C_DH)
    vsel = vs_blk[bi, gi, idx].reshape(B, C_KV_HEADS, Tq, n * SLC_BLK, C_DH)
    qg = q.reshape(B, Tq, C_KV_HEADS, C_GROUP, C_DH)
    s = jnp.einsum('btghd,bgtjd->bghtj', qg, ksel) * ATT_SCALE
    kpos = (idx[..., None] * SLC_BLK + jnp.arange(SLC_BLK)).reshape(B, C_KV_HEADS, Tq, n * SLC_BLK)
    mask = (kpos <= pos[None, None, :, None])[:, :, None]
    p = masked_softmax(s, mask)
    o = jnp.einsum('bghtj,bgtjd->btghd', p.astype(vsel.dtype), vsel)
    return o.reshape(B, Tq, C_HEADS, C_DH)


def slc_attention_blocked(q, pos, idx, ks_blk, vs_blk):
    B, T = q.shape[:2]
    nq = T // SLC_QBLK
    qc = q.reshape(B, nq, SLC_QBLK, C_HEADS, C_DH).swapaxes(0, 1)
    pc = pos.reshape(nq, SLC_QBLK)
    ic = idx.reshape(B, C_KV_HEADS, nq, SLC_QBLK, -1).transpose(2, 0, 1, 3, 4)
    oc = lax.map(lambda a: slc_attention(a[0], a[1], a[2], ks_blk, vs_blk), (qc, pc, ic))
    return oc.swapaxes(0, 1).reshape(B, T, C_HEADS, C_DH)


def win_attention_banded(q, kw, vw):
    B, T = q.shape[:2]
    nb = T // Q_BLK
    span = WINDOW + Q_BLK
    kp = jnp.pad(kw, ((0, 0), (WINDOW, 0), (0, 0), (0, 0)))
    vp = jnp.pad(vw, ((0, 0), (WINDOW, 0), (0, 0), (0, 0)))
    kidx = jnp.arange(nb)[:, None] * Q_BLK + jnp.arange(span)[None, :]
    kb, vb = kp[:, kidx], vp[:, kidx]
    kpos = (kidx - WINDOW)[:, None, :]
    qpos = jnp.arange(T).reshape(nb, Q_BLK)[:, :, None]
    mask = (kpos <= qpos) & (kpos > qpos - WINDOW) & (kpos >= 0)
    qb = q.reshape(B, nb, Q_BLK, C_KV_HEADS, C_GROUP, C_DH)
    s = jnp.einsum('bnqghd,bnkgd->bghnqk', qb, kb) * ATT_SCALE
    p = masked_softmax(s, mask)
    o = jnp.einsum('bghnqk,bnkgd->bnqghd', p.astype(vb.dtype), vb)
    return o.reshape(B, T, C_HEADS, C_DH)


def win_attention_cached(q, pos, kw, vw, kpos):
    B, Tq = q.shape[:2]
    qg = q.reshape(B, Tq, C_KV_HEADS, C_GROUP, C_DH)
    s = jnp.einsum('btghd,bkgd->bghtk', qg, kw) * ATT_SCALE
    mask = (kpos[None, :] <= pos[:, None]) & (kpos[None, :] > pos[:, None] - WINDOW)
    p = masked_softmax(s, mask)
    o = jnp.einsum('bghtk,bkgd->btghd', p.astype(vw.dtype), vw)
    return o.reshape(B, Tq, C_HEADS, C_DH)


def trunk_layer(x, c, pos, w, past_kv, win_buf, s0, h0, conv0, is_prompt):
    (ada_w, ada_b, w_in, lb, a_norm_g, conv_w, conv_b, ga_w, ga_b, gx_w, gx_b, lam,
     cmp_pos, cmp_w1, cmp_w2, w_branch, w_out, ln_g, ln_b) = w
    B, T, D = x.shape
    shift, scale, gate = jnp.split(c @ ada_w + ada_b, 3, axis=-1)
    h = x * (1.0 + scale[:, None]) + shift[:, None]
    splits = np.cumsum(IN_SIZES)[:-1].tolist()
    aq, af, ai, az, bx, bz, cq, ckv, cg, cz, mg = jnp.split(h @ w_in, splits, axis=-1)

    qa = jax.nn.silu(aq).reshape(B, T, A_HEADS, A_DK)
    log_f = jnp.logaddexp(jnp.log(lb), jnp.log1p(-lb) + jax.nn.log_sigmoid(af.astype(jnp.float32)))
    ka = -jnp.expm1(log_f)
    hg = (qa, ka.reshape(B, T, A_HEADS, A_DK), ai.reshape(B, T, A_HEADS, A_DV), log_f.reshape(B, T, A_HEADS, A_DK))
    if is_prompt:
        oa, s_new = hgrn2_chunked(hg[0], hg[1], hg[2], hg[3], s0)
    else:
        oa, s_new = hgrn2_recurrent(hg[0], hg[1], hg[2], hg[3], s0)
    ya = (group_rms(oa).reshape(B, T, A_WIDTH) * a_norm_g).astype(x.dtype) * jax.nn.silu(az)

    yb, h_new, conv_new = rglru_branch(bx, bz, conv0, h0, conv_w, conv_b, ga_w, ga_b, gx_w, gx_b, lam)

    q = cq.reshape(B, T, C_HEADS, C_DH)
    q_rot = rope(q, pos)
    kv = ckv.reshape(B, T, 6, C_KV_HEADS, C_DH)
    new_rows = jnp.stack([kv[:, :, 0], kv[:, :, 1], rope(kv[:, :, 2], pos), kv[:, :, 3]], axis=2)
    win_rows = jnp.stack([rope(kv[:, :, 4], pos), kv[:, :, 5]], axis=2)
    if is_prompt:
        full = new_rows
    else:
        full = jnp.concatenate([past_kv.astype(new_rows.dtype), new_rows], axis=1)
    L = full.shape[1]
    kc, vc = compress(full[:, :, 0:2], cmp_pos, cmp_w1, cmp_w2)
    o_cmp, p_cmp = cmp_attention(q, pos, kc, vc)
    n_slc = -(-L // SLC_BLK)
    idx = select_blocks(p_cmp, pos, n_slc)
    ks_blk = to_blocks(full[:, :, 2], n_slc)
    vs_blk = to_blocks(full[:, :, 3], n_slc)
    if is_prompt:
        o_slc = slc_attention_blocked(q_rot, pos, idx, ks_blk, vs_blk)
        o_win = win_attention_banded(q_rot, win_rows[:, :, 0], win_rows[:, :, 1])
        win_new = win_rows[:, T - min(WINDOW, T):]
    else:
        o_slc = slc_attention(q_rot, pos, idx, ks_blk, vs_blk)
        wb = win_buf.shape[1]
        wkv = jnp.concatenate([win_buf.astype(win_rows.dtype), win_rows], axis=1)
        wpos = pos[0] - wb + jnp.arange(wb + T)
        o_win = win_attention_cached(q_rot, pos, wkv[:, :, 0], wkv[:, :, 1], wpos)
        win_new = wkv[:, T:]
    g3 = jax.nn.sigmoid(cg).reshape(B, T, 3, C_HEADS, 1)
    oc = g3[:, :, 0] * o_cmp + g3[:, :, 1] * o_slc + g3[:, :, 2] * o_win
    yc = oc.reshape(B, T, C_WIDTH).astype(x.dtype) * jax.nn.silu(cz)

    ys = jnp.stack([ya, yb, yc], axis=2)
    branch = jnp.einsum('btnc,ncd->btnd', ys, w_branch)
    merged = jnp.sum(jax.nn.sigmoid(mg).reshape(B, T, N_BRANCH, D) * branch, axis=2)
    out = merged @ w_out
    x_new = layer_norm(ALPHA * x + gate[:, None] * out, ln_g, ln_b)
    return x_new, (new_rows, win_new, s_new, h_new, conv_new)


def setup_inputs(seed: int = 0) -> dict:
    key = jax.random.key(seed)
    ks = jax.random.split(key, 30)
    f32 = jnp.float32

    def nrm(k, shape, s):
        return jax.random.normal(k, shape, f32) * s

    n_pages = PAST_LEN // PAGE_SIZE
    n_pool = (5 * DEC_BATCH * n_pages) // 4
    win_buf = min(WINDOW, PAST_LEN)
    page_table = jax.random.permutation(ks[5], n_pool)[:DEC_BATCH * n_pages].reshape(DEC_BATCH, n_pages).astype(jnp.int32)
    a_init = jax.random.uniform(ks[21], (DEPTH, B_WIDTH), f32, 0.9, 0.999)
    return {
        'x_prompt': nrm(ks[0], (BATCH, SEQ, D_MODEL), 1.0),
        'x_sample': nrm(ks[1], (DEC_BATCH, DEC_SEQ, D_MODEL), 1.0),
        'c_prompt': nrm(ks[2], (BATCH, D_MODEL), 1.0),
        'c_sample': nrm(ks[3], (DEC_BATCH, D_MODEL), 1.0),
        'cache_nsa_kv': nrm(ks[4], (DEPTH, n_pool, PAGE_SIZE, 4, C_KV_HEADS, C_DH), 1.0),
        'page_table': page_table,
        'state_win_kv': nrm(ks[6], (DEPTH, DEC_BATCH, win_buf, 2, C_KV_HEADS, C_DH), 1.0),
        'state_hgrn': nrm(ks[7], (DEPTH, DEC_BATCH, A_HEADS, A_DK, A_DV), 0.3),
        'state_rglru': nrm(ks[8], (DEPTH, DEC_BATCH, B_WIDTH), 0.5),
        'state_conv': nrm(ks[9], (DEPTH, DEC_BATCH, CONV_W - 1, B_WIDTH), 1.0),
        'ada_w': nrm(ks[10], (DEPTH, D_MODEL, 3 * D_MODEL), 0.5 * D_MODEL ** -0.5),
        'ada_b': nrm(ks[11], (DEPTH, 3 * D_MODEL), 0.02),
        'w_in': nrm(ks[12], (DEPTH, D_MODEL, N_IN), D_MODEL ** -0.5),
        'a_lb': nrm(ks[13], (DEPTH, A_WIDTH), 1.0),
        'a_norm_g': 1.0 + nrm(ks[14], (DEPTH, A_WIDTH), 0.02),
        'b_conv_w': nrm(ks[15], (DEPTH, CONV_W, B_WIDTH), CONV_W ** -0.5),
        'b_conv_b': nrm(ks[16], (DEPTH, B_WIDTH), 0.02),
        'b_gate_a_w': nrm(ks[17], (DEPTH, B_BLOCKS, B_BW, B_BW), B_BW ** -0.5),
        'b_gate_a_b': nrm(ks[18], (DEPTH, B_WIDTH), 0.02),
        'b_gate_x_w': nrm(ks[19], (DEPTH, B_BLOCKS, B_BW, B_BW), B_BW ** -0.5),
        'b_gate_x_b': nrm(ks[20], (DEPTH, B_WIDTH), 0.02),
        'b_lambda': jnp.log(a_init) - jnp.log1p(-a_init),
        'c_cmp_pos': nrm(ks[22], (DEPTH, 2, CMP_LEN, C_DH), 0.02),
        'c_cmp_w1': nrm(ks[23], (DEPTH, 2, CMP_LEN * C_DH, C_DH), (CMP_LEN * C_DH) ** -0.5),
        'c_cmp_w2': nrm(ks[24], (DEPTH, 2, C_DH, C_DH), C_DH ** -0.5),
        'w_branch': nrm(ks[25], (DEPTH, N_BRANCH, BR_WIDTH, D_MODEL), BETA * BR_WIDTH ** -0.5),
        'w_out': nrm(ks[26], (DEPTH, D_MODEL, D_MODEL), BETA * D_MODEL ** -0.5),
        'ln_g': 1.0 + nrm(ks[27], (DEPTH, D_MODEL), 0.02),
        'ln_b': nrm(ks[28], (DEPTH, D_MODEL), 0.02),
    }


def reference(x_prompt, x_sample, c_prompt, c_sample, cache_nsa_kv, page_table, state_win_kv, state_hgrn,
              state_rglru, state_conv, ada_w, ada_b, w_in, a_lb, a_norm_g, b_conv_w, b_conv_b, b_gate_a_w,
              b_gate_a_b, b_gate_x_w, b_gate_x_b, b_lambda, c_cmp_pos, c_cmp_w1, c_cmp_w2, w_branch, w_out,
              ln_g, ln_b):
    lb_all = jnp.cumsum(jax.nn.softmax(a_lb.astype(jnp.float32), axis=0), axis=0)
    lb_all = lb_all - lb_all[0:1]
    n_pages = page_table.shape[1]
    past_len = n_pages * cache_nsa_kv.shape[2]
    bp, seq = x_prompt.shape[:2]
    bs, ts = x_sample.shape[:2]
    pos_p = jnp.arange(seq)
    pos_s = past_len + jnp.arange(ts)
    xp, xs = x_prompt, x_sample
    st_prompt, st_sample = [], []
    for l in range(DEPTH):
        w = (ada_w[l], ada_b[l], w_in[l], lb_all[l], a_norm_g[l], b_conv_w[l], b_conv_b[l],
             b_gate_a_w[l], b_gate_a_b[l], b_gate_x_w[l], b_gate_x_b[l], b_lambda[l],
             c_cmp_pos[l], c_cmp_w1[l], c_cmp_w2[l], w_branch[l], w_out[l], ln_g[l], ln_b[l])
        xp, sp = trunk_layer(xp, c_prompt, pos_p, w, None, None,
                             jnp.zeros((bp, A_HEADS, A_DK, A_DV), xp.dtype),
                             jnp.zeros((bp, B_WIDTH), xp.dtype),
                             jnp.zeros((bp, CONV_W - 1, B_WIDTH), xp.dtype), True)
        past = cache_nsa_kv[l][page_table].reshape(bs, past_len, 4, C_KV_HEADS, C_DH)
        xs, ss = trunk_layer(xs, c_sample, pos_s, w, past, state_win_kv[l], state_hgrn[l],
                             state_rglru[l], state_conv[l], False)
        st_prompt.append(sp)
        st_sample.append(ss)
    nsa_p = jnp.stack([s[0] for s in st_prompt], axis=0)
    nsa_s = jnp.stack([s[0] for s in st_sample], axis=0)
    win_p = jnp.stack([s[1] for s in st_prompt], axis=0)
    win_s = jnp.stack([s[1] for s in st_sample], axis=0)
    hg_p = jnp.stack([s[2] for s in st_prompt], axis=0)
    hg_s = jnp.stack([s[2] for s in st_sample], axis=0)
    rg_p = jnp.stack([s[3] for s in st_prompt], axis=0)
    rg_s = jnp.stack([s[3] for s in st_sample], axis=0)
    cv_p = jnp.stack([s[4] for s in st_prompt], axis=0)
    cv_s = jnp.stack([s[4] for s in st_sample], axis=0)
    return (xp, xs, nsa_p, nsa_s, win_p, win_s, hg_p, hg_s, rg_p, rg_s, cv_p, cv_s)
```

```python
import functools

import numpy as np
import jax
import jax.numpy as jnp
from jax import lax
from jax.experimental import pallas as pl
from jax.experimental.pallas import tpu as pltpu

F32 = jnp.float32
BF16 = jnp.bfloat16

D_MODEL = 1024
DEPTH = 4
PAGE_SIZE = 128
A_HEADS = 4
A_DK = 128
A_CHUNK = 64
B_WIDTH = 512
B_BLOCKS = 8
CONV_W = 4
LRU_C = 8.0
C_HEADS = 8
C_KV_HEADS = 2
C_GROUP = 4
C_DH = 64
CMP_LEN = 32
CMP_STRIDE = 16
SLC_BLK = 64
SLC_TOPN = 16
WINDOW = 512
ROPE_THETA = 10000.0
ATT_SCALE = C_DH ** -0.5
ALPHA = (2 * DEPTH) ** 0.25
LN_EPS = 1e-5
NEG_INF = -1e30
FORCED_SCORE = 1e9

COL_MG, COL_AQ, COL_AF, COL_AI, COL_AZ = 0, 3072, 3584, 4096, 4608
COL_BX, COL_BZ, COL_CQ, COL_CZ, COL_CKV, COL_CG = 5120, 5632, 6144, 6656, 7168, 7936
N_PROJ = 8064
PROJ_TN = 1152

VMEM_LIMIT = 56 * 1024 * 1024

NT_DIMS = (((1,), (1,)), ((), ()))
TN_DIMS = (((0,), (0,)), ((), ()))


def _cparams(*sem):
    return pltpu.CompilerParams(dimension_semantics=sem, vmem_limit_bytes=VMEM_LIMIT)


def _sigmoid(x):
    return jax.nn.sigmoid(x)


def _silu(x):
    return x * jax.nn.sigmoid(x)


def _log_f(af, loglb, l1mlb):
    ls = jnp.minimum(af, 0.0) - jnp.log1p(jnp.exp(-jnp.abs(af)))
    y = l1mlb + ls
    return jnp.maximum(loglb, y) + jnp.log1p(jnp.exp(-jnp.abs(loglb - y)))


def _ada_kernel(c_ref, w_ref, b_ref, o_ref):
    o_ref[0] = jnp.dot(c_ref[...], w_ref[0].astype(BF16), preferred_element_type=F32) + b_ref[0]


def _ada_all(c_all, ada_w, ada_b):
    nb = c_all.shape[0]
    return pl.pallas_call(
        _ada_kernel,
        grid=(DEPTH, 3),
        in_specs=[
            pl.BlockSpec((nb, D_MODEL), lambda l, j: (0, 0)),
            pl.BlockSpec((1, D_MODEL, D_MODEL), lambda l, j: (l, 0, j)),
            pl.BlockSpec((1, 1, D_MODEL), lambda l, j: (l, 0, j)),
        ],
        out_specs=pl.BlockSpec((1, nb, D_MODEL), lambda l, j: (l, 0, j)),
        out_shape=jax.ShapeDtypeStruct((DEPTH, nb, 3 * D_MODEL), F32),
        compiler_params=_cparams("arbitrary", "arbitrary"),
        name="ada",
    )(c_all.astype(BF16), ada_w, ada_b.reshape(DEPTH, 1, 3 * D_MODEL))


def _inproj_kernel(x_ref, sc_ref, sh_ref, w_ref, o_ref, h_ref):
    @pl.when(pl.program_id(2) == 0)
    def _():
        h_ref[...] = (x_ref[0] * (1.0 + sc_ref[0]) + sh_ref[0]).astype(BF16)

    o_ref[0] = jnp.dot(h_ref[...], w_ref[...], preferred_element_type=F32)


def _inproj(x, scale, shift, w_bf, tt):
    B, T, _ = x.shape
    ts = scale.shape[1]
    mod_spec = (pl.BlockSpec((1, 1, D_MODEL), lambda b, t, n: (b, 0, 0)) if ts == 1
                else pl.BlockSpec((1, tt, D_MODEL), lambda b, t, n: (b, t, 0)))
    return pl.pallas_call(
        _inproj_kernel,
        grid=(B, T // tt, N_PROJ // PROJ_TN),
        in_specs=[
            pl.BlockSpec((1, tt, D_MODEL), lambda b, t, n: (b, t, 0)),
            mod_spec, mod_spec,
            pl.BlockSpec((D_MODEL, PROJ_TN), lambda b, t, n: (0, n)),
        ],
        out_specs=pl.BlockSpec((1, tt, PROJ_TN), lambda b, t, n: (b, t, n)),
        out_shape=jax.ShapeDtypeStruct((B, T, N_PROJ), F32),
        scratch_shapes=[pltpu.VMEM((tt, D_MODEL), BF16)],
        compiler_params=_cparams("arbitrary", "arbitrary", "arbitrary"),
        name="inproj",
    )(x, scale, shift, w_bf)


HG_TT = 256
HG_LEVELS = (1, 2, 4, 8, 16, 32)


def _hgrn_level_table():
    t = np.arange(HG_TT)[:, None]
    s = np.arange(HG_TT)[None, :]
    lvl = np.full((HG_TT, HG_TT), -1, np.int32)
    same = (t // A_CHUNK) == (s // A_CHUNK)
    for i, h in enumerate(HG_LEVELS):
        m = same & ((t // h) == (s // h) + 1) & (((s // h) % 2) == 0)
        lvl[m] = i
    lvl[t == s] = len(HG_LEVELS)
    return lvl


def _hgrn_prompt_kernel(aq_ref, af_ref, ai_ref, az_ref, llb_ref, l1_ref, g_ref, lvl_ref,
                        ya_ref, so_ref, s_ref):
    t = pl.program_id(2)

    @pl.when(t == 0)
    def _():
        s_ref[...] = jnp.zeros_like(s_ref)

    aq = aq_ref[0]
    v = ai_ref[0]
    az = az_ref[0]
    q = _silu(aq)
    lf = _log_f(af_ref[0], llb_ref[...], l1_ref[...])
    k = 1.0 - jnp.exp(lf)
    vb = v.astype(BF16)
    row = lax.broadcasted_iota(jnp.int32, (HG_TT, A_DK), 0)
    lvl = lvl_ref[...]

    cin = lf
    rex = jnp.zeros_like(lf)
    att = jnp.where(lvl == len(HG_LEVELS), jnp.sum(q * k, axis=-1, keepdims=True), 0.0)
    for i, h in enumerate(HG_LEVELS):
        qh = (q * jnp.exp(cin)).astype(BF16)
        kh = (k * jnp.exp(rex)).astype(BF16)
        a = lax.dot_general(qh, kh, NT_DIMS, preferred_element_type=F32)
        att = jnp.where(lvl == i, a, att)
        tot = cin + rex
        upper = ((row // h) % 2) == 1
        cin = cin + jnp.where(upper, pltpu.roll(tot, h, 0), 0.0)
        rex = rex + jnp.where(upper, 0.0, pltpu.roll(tot, HG_TT - h, 0))

    o = jnp.dot(att.astype(BF16), vb, preferred_element_type=F32)
    qb = (q * jnp.exp(cin)).astype(BF16)
    kd = (k * jnp.exp(rex)).astype(BF16)
    tot = cin + rex
    s = s_ref[...]
    inter = []
    for c in range(HG_TT // A_CHUNK):
        sl = slice(c * A_CHUNK, (c + 1) * A_CHUNK)
        inter.append(jnp.dot(qb[sl], s.astype(BF16), preferred_element_type=F32))
        u = lax.dot_general(kd[sl], vb[sl], TN_DIMS, preferred_element_type=F32)
        e_rows = jnp.broadcast_to(jnp.exp(tot[c * A_CHUNK:c * A_CHUNK + 1]), (A_DK, A_DK))
        s = e_rows.T * s + u
    s_ref[...] = s
    o = o + jnp.concatenate(inter, axis=0)

    ms = jnp.mean(o * o, axis=-1, keepdims=True)
    ya_ref[0] = ((o * lax.rsqrt(ms + LN_EPS)) * g_ref[...] * _silu(az)).astype(BF16)

    @pl.when(t == pl.num_programs(2) - 1)
    def _():
        so_ref[0, 0] = s


def _hgrn_prompt(proj, loglb, l1mlb, gain):
    B, T, _ = proj.shape
    lvl = jnp.asarray(_hgrn_level_table())

    def col(base):
        return pl.BlockSpec((1, HG_TT, A_DK), lambda b, h, t: (b, t, base // A_DK + h))

    vec = pl.BlockSpec((1, A_DK), lambda b, h, t: (0, h))
    return pl.pallas_call(
        _hgrn_prompt_kernel,
        grid=(B, A_HEADS, T // HG_TT),
        in_specs=[col(COL_AQ), col(COL_AF), col(COL_AI), col(COL_AZ), vec, vec, vec,
                  pl.BlockSpec((HG_TT, HG_TT), lambda b, h, t: (0, 0))],
        out_specs=[pl.BlockSpec((1, HG_TT, A_DK), lambda b, h, t: (b, t, h)),
                   pl.BlockSpec((1, 1, A_DK, A_DK), lambda b, h, t: (b, h, 0, 0))],
        out_shape=[jax.ShapeDtypeStruct((B, T, A_HEADS * A_DK), BF16),
                   jax.ShapeDtypeStruct((B, A_HEADS, A_DK, A_DK), F32)],
        scratch_shapes=[pltpu.VMEM((A_DK, A_DK), F32)],
        compiler_params=_cparams("arbitrary", "arbitrary", "arbitrary"),
        name="hgrn_prompt",
    )(proj, proj, proj, proj, loglb, l1mlb, gain, lvl)


HG_SB = 16


def _hgrn_decode_kernel(s_ref, aqc_ref, afc_ref, ai_ref, az_ref, llb_ref, l1_ref, g_ref, so_ref, ya_ref):
    aqc = aqc_ref[0, 0]
    qc = _silu(aqc)
    lf = _log_f(afc_ref[0, 0], llb_ref[0], l1_ref[0])
    ec = jnp.exp(lf)
    kc = 1.0 - ec
    v = ai_ref[...]
    rows = []
    for r in range(HG_SB):
        sn = ec[:, r:r + 1] * s_ref[r, 0] + kc[:, r:r + 1] * v[r:r + 1, :]
        so_ref[r, 0] = sn
        rows.append(jnp.sum(qc[:, r:r + 1] * sn, axis=0, keepdims=True))
    o = jnp.concatenate(rows, axis=0)
    ms = jnp.mean(o * o, axis=-1, keepdims=True)
    ya_ref[...] = ((o * lax.rsqrt(ms + LN_EPS)) * g_ref[...] * _silu(az_ref[...])).astype(BF16)


def _hgrn_decode(state_all, layer, proj_s, loglb_c, l1mlb_c, gain):
    nb = proj_s.shape[0]
    nblk = nb // HG_SB

    def cols(base):
        a = proj_s[:, base:base + A_HEADS * A_DK].reshape(nblk, HG_SB, A_HEADS, A_DK)
        return a.transpose(2, 0, 3, 1)

    st_spec = pl.BlockSpec((None, HG_SB, 1, A_DK, A_DK), lambda h, i: (layer, i, h, 0, 0))
    col_spec = pl.BlockSpec((1, 1, A_DK, HG_SB), lambda h, i: (h, i, 0, 0))
    cvec = pl.BlockSpec((1, A_DK, 1), lambda h, i: (h, 0, 0))
    return pl.pallas_call(
        _hgrn_decode_kernel,
        grid=(A_HEADS, nblk),
        in_specs=[st_spec, col_spec, col_spec,
                  pl.BlockSpec((HG_SB, A_DK), lambda h, i: (i, COL_AI // A_DK + h)),
                  pl.BlockSpec((HG_SB, A_DK), lambda h, i: (i, COL_AZ // A_DK + h)),
                  cvec, cvec,
                  pl.BlockSpec((1, A_DK), lambda h, i: (0, h))],
        out_specs=[pl.BlockSpec((HG_SB, 1, A_DK, A_DK), lambda h, i: (i, h, 0, 0)),
                   pl.BlockSpec((HG_SB, A_DK), lambda h, i: (i, h))],
        out_shape=[jax.ShapeDtypeStruct((nb, A_HEADS, A_DK, A_DK), F32),
                   jax.ShapeDtypeStruct((nb, A_HEADS * A_DK), BF16)],
        compiler_params=_cparams("arbitrary", "arbitrary"),
        name="hgrn_decode",
    )(state_all, cols(COL_AQ), cols(COL_AF), proj_s, proj_s, loglb_c, l1mlb_c, gain)


RG_TT = 256


def _rglru_gates(u, ga_ref, gab_ref, gx_ref, gxb_ref, lam_ref):
    ub = u.astype(BF16)
    r = _sigmoid(jnp.dot(ub, ga_ref[...], preferred_element_type=F32) + gab_ref[...])
    ig = _sigmoid(jnp.dot(ub, gx_ref[...], preferred_element_type=F32) + gxb_ref[...])
    nl = -lam_ref[...]
    softplus = jnp.maximum(nl, 0.0) + jnp.log1p(jnp.exp(-jnp.abs(nl)))
    a = jnp.exp(-LRU_C * r * softplus)
    return a, jnp.sqrt(1.0 - a * a) * (ig * u)


def _rglru_prompt_kernel(bx_ref, bz_ref, cw_ref, cb_ref, ga_ref, gab_ref, gx_ref, gxb_ref, lam_ref,
                         yb_ref, h_ref, cv_ref, ext_ref, hc_ref):
    t = pl.program_id(1)

    @pl.when(t == 0)
    def _():
        ext_ref[0:8, :] = jnp.zeros((8, B_WIDTH), F32)
        hc_ref[...] = jnp.zeros_like(hc_ref)

    x = bx_ref[0]
    ext_ref[8:8 + RG_TT, :] = x
    u = cb_ref[...] + (ext_ref[5:5 + RG_TT, :] * cw_ref[0:1, :] + ext_ref[6:6 + RG_TT, :] * cw_ref[1:2, :]
                       + ext_ref[7:7 + RG_TT, :] * cw_ref[2:3, :] + x * cw_ref[3:4, :])
    ext_ref[0:8, :] = x[RG_TT - 8:RG_TT, :]
    a, bt = _rglru_gates(u, ga_ref, gab_ref, gx_ref, gxb_ref, lam_ref)
    row = lax.broadcasted_iota(jnp.int32, (RG_TT, 128), 0)
    z = bz_ref[0]
    for c in range(B_WIDTH // 128):
        cs = slice(c * 128, (c + 1) * 128)
        ac = a[:, cs]
        bc = bt[:, cs] + jnp.where(row == 0, ac * hc_ref[0:1, cs], 0.0)
        s = 1
        while s < RG_TT:
            keep = row >= s
            bc = jnp.where(keep, ac * pltpu.roll(bc, s, 0) + bc, bc)
            ac = jnp.where(keep, ac * pltpu.roll(ac, s, 0), ac)
            s *= 2
        hc_ref[0:1, cs] = bc[RG_TT - 1:RG_TT, :]
        yb_ref[0, :, cs] = (bc * _silu(z[:, cs])).astype(BF16)

    @pl.when(t == pl.num_programs(1) - 1)
    def _():
        h_ref[0] = hc_ref[0:1, :]
        cv_ref[0] = x[RG_TT - (CONV_W - 1):RG_TT, :]


def _rglru_weight_specs(idx):
    return [pl.BlockSpec((CONV_W, B_WIDTH), idx), pl.BlockSpec((1, B_WIDTH), idx),
            pl.BlockSpec((B_WIDTH, B_WIDTH), idx), pl.BlockSpec((1, B_WIDTH), idx),
            pl.BlockSpec((B_WIDTH, B_WIDTH), idx), pl.BlockSpec((1, B_WIDTH), idx),
            pl.BlockSpec((1, B_WIDTH), idx)]


def _rglru_prompt(proj, w):
    B, T, _ = proj.shape
    return pl.pallas_call(
        _rglru_prompt_kernel,
        grid=(B, T // RG_TT),
        in_specs=[pl.BlockSpec((1, RG_TT, B_WIDTH), lambda b, t: (b, t, COL_BX // B_WIDTH)),
                  pl.BlockSpec((1, RG_TT, B_WIDTH), lambda b, t: (b, t, COL_BZ // B_WIDTH))]
        + _rglru_weight_specs(lambda b, t: (0, 0)),
        out_specs=[pl.BlockSpec((1, RG_TT, B_WIDTH), lambda b, t: (b, t, 0)),
                   pl.BlockSpec((1, 1, B_WIDTH), lambda b, t: (b, 0, 0)),
                   pl.BlockSpec((1, CONV_W - 1, B_WIDTH), lambda b, t: (b, 0, 0))],
        out_shape=[jax.ShapeDtypeStruct((B, T, B_WIDTH), BF16),
                   jax.ShapeDtypeStruct((B, 1, B_WIDTH), F32),
                   jax.ShapeDtypeStruct((B, CONV_W - 1, B_WIDTH), F32)],
        scratch_shapes=[pltpu.VMEM((8 + RG_TT, B_WIDTH), F32), pltpu.VMEM((8, B_WIDTH), F32)],
        compiler_params=_cparams("arbitrary", "arbitrary"),
        name="rglru_prompt",
    )(proj, proj, *w)


def _rglru_decode_kernel(bx_ref, bz_ref, cv0_ref, h0_ref, cw_ref, cb_ref, ga_ref, gab_ref, gx_ref, gxb_ref,
                         lam_ref, yb_ref, h_ref, cv_ref):
    x = bx_ref[...]
    c0 = cv0_ref[:, 0:B_WIDTH]
    c1 = cv0_ref[:, B_WIDTH:2 * B_WIDTH]
    c2 = cv0_ref[:, 2 * B_WIDTH:3 * B_WIDTH]
    u = cb_ref[...] + (c0 * cw_ref[0:1, :] + c1 * cw_ref[1:2, :] + c2 * cw_ref[2:3, :] + x * cw_ref[3:4, :])
    a, bt = _rglru_gates(u, ga_ref, gab_ref, gx_ref, gxb_ref, lam_ref)
    h = a * h0_ref[...] + bt
    h_ref[...] = h
    yb_ref[...] = (h * _silu(bz_ref[...])).astype(BF16)
    cv_ref[:, 0:B_WIDTH] = c1
    cv_ref[:, B_WIDTH:2 * B_WIDTH] = c2
    cv_ref[:, 2 * B_WIDTH:3 * B_WIDTH] = x


def _rglru_decode(proj_s, conv0, h0, w):
    nb = proj_s.shape[0]
    z2 = lambda i: (0, 0)
    return pl.pallas_call(
        _rglru_decode_kernel,
        grid=(1,),
        in_specs=[pl.BlockSpec((nb, B_WIDTH), lambda i: (0, COL_BX // B_WIDTH)),
                  pl.BlockSpec((nb, B_WIDTH), lambda i: (0, COL_BZ // B_WIDTH)),
                  pl.BlockSpec((nb, 3 * B_WIDTH), z2), pl.BlockSpec((nb, B_WIDTH), z2)]
        + _rglru_weight_specs(z2),
        out_specs=[pl.BlockSpec((nb, B_WIDTH), z2), pl.BlockSpec((nb, B_WIDTH), z2),
                   pl.BlockSpec((nb, 3 * B_WIDTH), z2)],
        out_shape=[jax.ShapeDtypeStruct((nb, B_WIDTH), BF16), jax.ShapeDtypeStruct((nb, B_WIDTH), F32),
                   jax.ShapeDtypeStruct((nb, 3 * B_WIDTH), F32)],
        compiler_params=_cparams("arbitrary"),
        name="rglru_decode",
    )(proj_s, proj_s, conv0, h0, *w)


def _gelu_tanh(x):
    return 0.5 * x * (1.0 + jnp.tanh(0.7978845608028654 * (x + 0.044715 * (x * x * x))))


def _compress(load_l, pos_ref, w1_ref, w2_ref):
    acc_a = jnp.zeros((128, 256), F32)
    acc_b = jnp.zeros((128, 256), F32)
    for l in range(CMP_STRIDE):
        xl = load_l(l)
        acc_a = acc_a + jnp.dot((xl + pos_ref[l:l + 1, :]).astype(BF16), w1_ref[l], preferred_element_type=F32)
        acc_b = acc_b + jnp.dot((xl + pos_ref[CMP_STRIDE + l:CMP_STRIDE + l + 1, :]).astype(BF16),
                                w1_ref[CMP_STRIDE + l], preferred_element_type=F32)
    hid = _gelu_tanh(acc_a + pltpu.roll(acc_b, 127, 0))
    return jnp.dot(hid.astype(BF16), w2_ref[...], preferred_element_type=F32)


def _rank_select(score, idx, n_rows, axis):
    rank = jnp.zeros_like(score)
    for jp in range(n_rows):
        other = score[jp:jp + 1, :] if axis == 0 else score[:, jp:jp + 1]
        rank = rank + jnp.where(other > score, 1.0, jnp.where(other == score, jnp.where(idx > jp, 1.0, 0.0), 0.0))
    return jnp.where(rank < float(SLC_TOPN), 1.0, 0.0)


NP_TT = 256


def _nsa_prep_kernel(cq_ref, kv01_ref, kv23_ref, kv45_ref, cos_ref, sin_ref,
                     rows_ref, win_ref, q_ref, qr_ref, ks_ref, vs_ref, kw_ref, vw_ref):
    cos = cos_ref[...]
    sin = sin_ref[...]
    lane = lax.broadcasted_iota(jnp.int32, cos.shape, 1)
    first = (lane % C_DH) < (C_DH // 2)

    def rope(x):
        return x * cos + jnp.where(first, pltpu.roll(x, 128 - C_DH // 2, 1), pltpu.roll(x, C_DH // 2, 1)) * sin

    def heads(ref, x, h0):
        ref[0, h0] = x[:, 0:C_DH].astype(BF16)
        ref[0, h0 + 1] = x[:, C_DH:2 * C_DH].astype(BF16)

    cq = cq_ref[0] * ATT_SCALE
    for c in range(C_HEADS // 2):
        x = cq[:, c * 128:(c + 1) * 128]
        heads(q_ref, x, 2 * c)
        heads(qr_ref, rope(x), 2 * c)
    kv23 = kv23_ref[0]
    ks = rope(kv23[:, 0:128])
    rows_ref[0, :, 0:256] = kv01_ref[0]
    rows_ref[0, :, 256:384] = ks
    rows_ref[0, :, 384:512] = kv23[:, 128:256]
    heads(ks_ref, ks, 0)
    heads(vs_ref, kv23[:, 128:256], 0)
    kv45 = kv45_ref[0]
    kw = rope(kv45[:, 0:128])
    win_ref[0, :, 0:128] = kw
    win_ref[0, :, 128:256] = kv45[:, 128:256]
    heads(kw_ref, kw, 0)
    heads(vw_ref, kv45[:, 128:256], 0)


def _nsa_prep(proj, cos128, sin128):
    B, T, _ = proj.shape
    hm = lambda n: pl.BlockSpec((1, n, NP_TT, C_DH), lambda b, t: (b, 0, t, 0))
    hs = lambda n: jax.ShapeDtypeStruct((B, n, T, C_DH), BF16)
    kvb = COL_CKV // 256
    return pl.pallas_call(
        _nsa_prep_kernel,
        grid=(B, T // NP_TT),
        in_specs=[pl.BlockSpec((1, NP_TT, 512), lambda b, t: (b, t, COL_CQ // 512)),
                  pl.BlockSpec((1, NP_TT, 256), lambda b, t: (b, t, kvb)),
                  pl.BlockSpec((1, NP_TT, 256), lambda b, t: (b, t, kvb + 1)),
                  pl.BlockSpec((1, NP_TT, 256), lambda b, t: (b, t, kvb + 2)),
                  pl.BlockSpec((NP_TT, 128), lambda b, t: (t, 0)),
                  pl.BlockSpec((NP_TT, 128), lambda b, t: (t, 0))],
        out_specs=[pl.BlockSpec((1, NP_TT, 512), lambda b, t: (b, t, 0)),
                   pl.BlockSpec((1, NP_TT, 256), lambda b, t: (b, t, 0)),
                   hm(8), hm(8), hm(2), hm(2), hm(2), hm(2)],
        out_shape=[jax.ShapeDtypeStruct((B, T, 512), F32), jax.ShapeDtypeStruct((B, T, 256), F32),
                   hs(8), hs(8), hs(2), hs(2), hs(2), hs(2)],
        compiler_params=_cparams("arbitrary", "arbitrary"),
        name="nsa_prep",
    )(proj, proj, proj, proj, cos128, sin128)


def _compress_prompt_kernel(xk_ref, xv_ref, pos_ref, w1_ref, w2_ref, o_ref):
    def load_l(l):
        return jnp.concatenate([xk_ref[pl.ds(l, 128, stride=CMP_STRIDE), :],
                                xv_ref[pl.ds(l, 128, stride=CMP_STRIDE), :]], axis=1)

    out = _compress(load_l, pos_ref, w1_ref, w2_ref)
    for i in range(4):
        o_ref[0, i] = out[:, i * C_DH:(i + 1) * C_DH].astype(BF16)


def _compress_prompt(rows, pos256, w1bd, w2bd):
    B, T, _ = rows.shape
    return pl.pallas_call(
        _compress_prompt_kernel,
        grid=(B,),
        in_specs=[pl.BlockSpec((None, T, 128), lambda b: (b, 0, 0)),
                  pl.BlockSpec((None, T, 128), lambda b: (b, 0, 1)),
                  pl.BlockSpec((CMP_LEN, 256), lambda b: (0, 0)),
                  pl.BlockSpec((CMP_LEN, 256, 256), lambda b: (0, 0, 0)),
                  pl.BlockSpec((256, 256), lambda b: (0, 0))],
        out_specs=pl.BlockSpec((1, 4, 128, C_DH), lambda b: (b, 0, 0, 0)),
        out_shape=jax.ShapeDtypeStruct((B, 4, 128, C_DH), BF16),
        compiler_params=_cparams("arbitrary"),
        name="compress_prompt",
    )(rows, rows, pos256, w1bd, w2bd)


NA_TQ = 128
NA_KT = 128
NA_M = C_GROUP * NA_TQ


def _flash(q, k_ref, v_ref, g, j_lo, j_hi, mask_fn, m_ref, l_ref, acc_ref):
    m_ref[...] = jnp.full(m_ref.shape, NEG_INF, F32)
    l_ref[...] = jnp.zeros_like(l_ref)
    acc_ref[...] = jnp.zeros_like(acc_ref)

    def body(j, carry):
        off = pl.multiple_of(j * NA_KT, NA_KT)
        k = k_ref[0, g, pl.ds(off, NA_KT), :]
        v = v_ref[0, g, pl.ds(off, NA_KT), :]
        msk = mask_fn(j)[None]
        s = lax.dot_general(q, k, NT_DIMS, preferred_element_type=F32).reshape(C_GROUP, NA_TQ, NA_KT)
        s = jnp.where(msk, s, NEG_INF)
        m_prev = m_ref[...].reshape(C_GROUP, NA_TQ, 1)
        m_new = jnp.maximum(m_prev, jnp.max(s, axis=-1, keepdims=True))
        p = jnp.where(msk, jnp.exp(s - m_new), 0.0)
        alpha = jnp.exp(m_prev - m_new)
        l_ref[...] = (alpha * l_ref[...].reshape(C_GROUP, NA_TQ, 1)
                      + jnp.sum(p, axis=-1, keepdims=True)).reshape(NA_M, 1)
        pv = jnp.dot(p.reshape(NA_M, NA_KT).astype(BF16), v, preferred_element_type=F32)
        acc_ref[...] = alpha.reshape(NA_M, 1) * acc_ref[...] + pv
        m_ref[...] = m_new.reshape(NA_M, 1)
        return carry

    lax.fori_loop(j_lo, j_hi, body, 0)
    return acc_ref[...] / l_ref[...]


def _nsa_prompt_kernel(q_ref, qr_ref, kcvc_ref, ks_ref, vs_ref, kw_ref, vw_ref, cg_ref, cz_ref, ovt_ref,
                       yc_ref, m_ref, l_ref, acc_ref):
    qi = pl.program_id(1)
    t0 = qi * NA_TQ
    n_slc = ovt_ref.shape[0]
    sig = _sigmoid(cg_ref[0])
    row_t = t0 + lax.broadcasted_iota(jnp.int32, (NA_TQ, 128), 0)
    col = lax.broadcasted_iota(jnp.int32, (NA_TQ, 128), 1)
    cmp_ok = (CMP_STRIDE * col + (CMP_LEN - 1)) <= row_t
    jidx = lax.broadcasted_iota(jnp.int32, (n_slc, NA_TQ), 0)
    cur = (t0 + lax.broadcasted_iota(jnp.int32, (n_slc, NA_TQ), 1)) // SLC_BLK
    blk_row = lax.broadcasted_iota(jnp.int32, (128, NA_KT), 0)
    key_col = lax.broadcasted_iota(jnp.int32, (128, NA_KT), 1)
    outs = []
    for g in range(C_KV_HEADS):
        qg = q_ref[0, C_GROUP * g:C_GROUP * (g + 1)].reshape(NA_M, C_DH)
        qrg = qr_ref[0, C_GROUP * g:C_GROUP * (g + 1)].reshape(NA_M, C_DH)
        s = lax.dot_general(qg, kcvc_ref[0, g], NT_DIMS, preferred_element_type=F32).reshape(C_GROUP, NA_TQ, 128)
        s = jnp.where(cmp_ok[None], s, NEG_INF)
        e = jnp.where(cmp_ok[None], jnp.exp(s - jnp.max(s, axis=-1, keepdims=True)), 0.0)
        p = e / jnp.maximum(jnp.sum(e, axis=-1, keepdims=True), 1e-30)
        o_cmp = jnp.dot(p.reshape(NA_M, 128).astype(BF16), kcvc_ref[0, C_KV_HEADS + g], preferred_element_type=F32)
        psum = p[0] + p[1] + p[2] + p[3]
        imp = lax.dot_general(ovt_ref[...], psum, NT_DIMS, precision=lax.Precision.HIGHEST,
                              preferred_element_type=F32)
        score = jnp.where(jidx == 0, FORCED_SCORE,
                          jnp.where(jidx == cur, FORCED_SCORE,
                                    jnp.where(jidx == cur - 1, FORCED_SCORE,
                                              jnp.where(jidx <= cur, imp, -1.0))))
        sel_t = _rank_select(score, jidx, n_slc, 0)
        sel = jnp.concatenate([sel_t, jnp.zeros((128 - n_slc, NA_TQ), F32)], axis=0).T.astype(BF16)

        def slc_mask(j):
            expand = jnp.where(blk_row == (j * NA_KT + key_col) // SLC_BLK, 1.0, 0.0).astype(BF16)
            chosen = jnp.dot(sel, expand, preferred_element_type=F32)
            return jnp.where(j * NA_KT + col <= row_t, chosen, 0.0) > 0.5

        def win_mask(j):
            kpos = j * NA_KT + col
            return jnp.where(kpos <= row_t, jnp.where(kpos > row_t - WINDOW, 1.0, 0.0), 0.0) > 0.5

        o_slc = _flash(qrg, ks_ref, vs_ref, g, 0, qi + 1, slc_mask, m_ref, l_ref, acc_ref)
        o_win = _flash(qrg, kw_ref, vw_ref, g, jnp.maximum(qi - WINDOW // NA_KT, 0), qi + 1, win_mask,
                       m_ref, l_ref, acc_ref)
        for hh in range(C_GROUP):
            hd = C_GROUP * g + hh
            rs = slice(hh * NA_TQ, (hh + 1) * NA_TQ)
            outs.append(sig[:, hd:hd + 1] * o_cmp[rs] + sig[:, C_HEADS + hd:C_HEADS + hd + 1] * o_slc[rs]
                        + sig[:, 2 * C_HEADS + hd:2 * C_HEADS + hd + 1] * o_win[rs])
    yc_ref[0] = (jnp.concatenate(outs, axis=1) * _silu(cz_ref[0])).astype(BF16)


def _overlap_t(n_slc, n_rows):
    cs = np.arange(128) * CMP_STRIDE
    ss = np.arange(n_rows) * SLC_BLK
    ov = (cs[None, :] < ss[:, None] + SLC_BLK) & (cs[None, :] + CMP_LEN > ss[:, None])
    ov &= (np.arange(128)[None, :] < 127) & (np.arange(n_rows)[:, None] < n_slc)
    return ov.astype(np.float32)


def _nsa_prompt(proj, q_hm, qr_hm, kcvc, ks_hm, vs_hm, kw_hm, vw_hm):
    B, T, _ = proj.shape
    n_slc = T // SLC_BLK
    ovt = jnp.asarray(_overlap_t(n_slc, n_slc))
    qspec = pl.BlockSpec((1, C_HEADS, NA_TQ, C_DH), lambda b, t: (b, 0, t, 0))
    kvspec = pl.BlockSpec((1, C_KV_HEADS, T, C_DH), lambda b, t: (b, 0, 0, 0))
    return pl.pallas_call(
        _nsa_prompt_kernel,
        grid=(B, T // NA_TQ),
        in_specs=[qspec, qspec,
                  pl.BlockSpec((1, 4, 128, C_DH), lambda b, t: (b, 0, 0, 0)),
                  kvspec, kvspec, kvspec, kvspec,
                  pl.BlockSpec((1, NA_TQ, 128), lambda b, t: (b, t, COL_CG // 128)),
                  pl.BlockSpec((1, NA_TQ, 512), lambda b, t: (b, t, COL_CZ // 512)),
                  pl.BlockSpec((n_slc, 128), lambda b, t: (0, 0))],
        out_specs=pl.BlockSpec((1, NA_TQ, 512), lambda b, t: (b, t, 0)),
        out_shape=jax.ShapeDtypeStruct((B, T, 512), BF16),
        scratch_shapes=[pltpu.VMEM((NA_M, 1), F32), pltpu.VMEM((NA_M, 1), F32), pltpu.VMEM((NA_M, C_DH), F32)],
        compiler_params=_cparams("arbitrary", "arbitrary"),
        name="nsa_prompt",
    )(q_hm, qr_hm, kcvc, ks_hm, vs_hm, kw_hm, vw_hm, proj, proj, ovt)


N_PAGES = 16


def _nsa_decode_kernel(pt_ref, *refs):
    pages_k = refs[:N_PAGES]
    pages_v = refs[N_PAGES:2 * N_PAGES]
    pages = refs[2 * N_PAGES:3 * N_PAGES]
    (q_ref, kv_ref, cg_ref, cz_ref, wb_ref, cos_ref, sin_ref, pos_ref, w1_ref, w2_ref, ov_ref, ex_ref,
     yc_ref, rows_ref, wn_ref) = refs[3 * N_PAGES:]
    past = N_PAGES * PAGE_SIZE
    n_slc = -(-(past + 1) // SLC_BLK)
    cos = cos_ref[...]
    sin = sin_ref[...]

    def rope(x):
        return x * cos + jnp.concatenate([x[:, C_DH // 2:], x[:, :C_DH // 2]], axis=1) * sin

    def row_dot(a_bf, b_row):
        return jnp.sum(a_bf.astype(F32) * b_row.astype(BF16).astype(F32), axis=-1, keepdims=True)

    row8 = lax.broadcasted_iota(jnp.int32, (C_HEADS, 128), 0)
    lane = lax.broadcasted_iota(jnp.int32, (C_HEADS, 128), 1)
    g0 = row8 < C_GROUP
    g0h = g0[:, 0:C_DH]

    def spread(x):
        return jnp.concatenate([jnp.where(g0h, x, 0.0), jnp.where(g0h, 0.0, x)], axis=1).astype(BF16)

    def gather(x):
        return jnp.where(g0h, x[:, 0:C_DH], x[:, C_DH:2 * C_DH])

    q8 = q_ref[0] * ATT_SCALE
    kv = kv_ref[0]
    ks_new = rope(kv[4:6])
    kw_new = rope(kv[8:10])
    rows_ref[0, 0:4, :] = kv[0:4]
    rows_ref[0, 4:6, :] = ks_new
    rows_ref[0, 6:8, :] = kv[6:8]
    qs = spread(q8)
    qrs = spread(rope(q8))

    def load_l(l):
        sl = pl.ds(l, PAGE_SIZE // CMP_STRIDE, stride=CMP_STRIDE)
        return jnp.concatenate([jnp.concatenate([pg[sl, :] for pg in pages_k], axis=0),
                                jnp.concatenate([pg[sl, :] for pg in pages_v], axis=0)], axis=1)

    kcvc = _compress(load_l, pos_ref, w1_ref, w2_ref).astype(BF16)
    s = lax.dot_general(qs, kcvc[:, 0:128], NT_DIMS, preferred_element_type=F32)
    ok = (CMP_STRIDE * lane + (CMP_LEN - 1)) <= past
    s = jnp.where(ok, s, NEG_INF)
    e = jnp.where(ok, jnp.exp(s - jnp.max(s, axis=-1, keepdims=True)), 0.0)
    p = e / jnp.maximum(jnp.sum(e, axis=-1, keepdims=True), 1e-30)
    o_cmp = gather(jnp.dot(p.astype(BF16), kcvc[:, 128:256], preferred_element_type=F32))

    ps0 = jnp.sum(jnp.where(g0, p, 0.0), axis=0, keepdims=True)
    ps1 = jnp.sum(jnp.where(g0, 0.0, p), axis=0, keepdims=True)
    imp = jnp.dot(jnp.where(g0, ps0, ps1), ov_ref[...], precision=lax.Precision.HIGHEST, preferred_element_type=F32)
    cur = past // SLC_BLK
    score = jnp.where(lane == 0, FORCED_SCORE,
                      jnp.where(lane == cur, FORCED_SCORE,
                                jnp.where(lane == cur - 1, FORCED_SCORE,
                                          jnp.where(lane <= cur, imp, -1.0))))
    score = jnp.where(lane < n_slc, score, -2.0)
    sel = _rank_select(score, lane, n_slc, 1)
    chosen = jnp.dot(sel.astype(BF16), ex_ref[...], preferred_element_type=F32)

    s_pg, ok_pg = [], []
    for i, pg in enumerate(pages):
        kp = pg[:, 0:128].astype(BF16)
        ok_pg.append(chosen[:, i * PAGE_SIZE:(i + 1) * PAGE_SIZE] > 0.5)
        s_pg.append(jnp.where(ok_pg[i], lax.dot_general(qrs, kp, NT_DIMS, preferred_element_type=F32), NEG_INF))
    new_ok = sel[:, cur:cur + 1] > 0.5
    s_new = jnp.where(new_ok, row_dot(qrs, jnp.concatenate([ks_new[0:1], ks_new[1:2]], axis=1)), NEG_INF)
    m = s_new
    for sp in s_pg:
        m = jnp.maximum(m, jnp.max(sp, axis=-1, keepdims=True))
    p_new = jnp.where(new_ok, jnp.exp(s_new - m), 0.0)
    den = p_new
    vs_new = jnp.concatenate([kv[6:7], kv[7:8]], axis=1)
    acc = p_new * vs_new
    for i, pg in enumerate(pages):
        pp = jnp.where(ok_pg[i], jnp.exp(s_pg[i] - m), 0.0)
        den = den + jnp.sum(pp, axis=-1, keepdims=True)
        acc = acc + jnp.dot(pp.astype(BF16), pg[:, 128:256].astype(BF16), preferred_element_type=F32)
    o_slc = gather(acc / den)

    wb = wb_ref[0]
    nw = wb.shape[0]
    sw = lax.dot_general(qrs, wb[:, 0:128].astype(BF16), NT_DIMS, preferred_element_type=F32)
    wpos = (past - nw) + lax.broadcasted_iota(jnp.int32, (C_HEADS, nw), 1)
    w_ok = jnp.where(wpos <= past, jnp.where(wpos > past - WINDOW, 1.0, 0.0), 0.0) > 0.5
    sw = jnp.where(w_ok, sw, NEG_INF)
    sw_new = row_dot(qrs, jnp.concatenate([kw_new[0:1], kw_new[1:2]], axis=1))
    mw = jnp.maximum(jnp.max(sw, axis=-1, keepdims=True), sw_new)
    pw = jnp.where(w_ok, jnp.exp(sw - mw), 0.0)
    pw_new = jnp.exp(sw_new - mw)
    vw_new = jnp.concatenate([kv[10:11], kv[11:12]], axis=1)
    accw = jnp.dot(pw.astype(BF16), wb[:, 128:256].astype(BF16), preferred_element_type=F32) + pw_new * vw_new
    o_win = gather(accw / (jnp.sum(pw, axis=-1, keepdims=True) + pw_new))

    sig = _sigmoid(cg_ref[0])
    oc = sig[:, 0:1] * o_cmp + sig[:, 1:2] * o_slc + sig[:, 2:3] * o_win
    yc_ref[0] = (oc * _silu(cz_ref[0])).astype(BF16)

    win_row = jnp.concatenate([kw_new[0:1], kw_new[1:2], kv[10:11], kv[11:12]], axis=1)
    wrow = lax.broadcasted_iota(jnp.int32, wb.shape, 0)
    wn_ref[0] = jnp.where(wrow == nw - 1, win_row, pltpu.roll(wb, nw - 1, 0))


def _nsa_decode(cache4, layer, page_table, win_all, q8, kv12, cg83, cz8, cos64, sin64, pos256, w1bd, w2bd):
    nb = q8.shape[0]
    nw = win_all.shape[2]
    past = N_PAGES * PAGE_SIZE
    n_slc = -(-(past + 1) // SLC_BLK)
    ov = jnp.asarray(_overlap_t(n_slc, 128).T)
    keys = np.arange(past)
    ex = jnp.asarray((np.arange(128)[:, None] == keys[None, :] // SLC_BLK).astype(np.float32)).astype(BF16)

    def page_spec(i, width, col):
        return pl.BlockSpec((None, None, PAGE_SIZE, width), lambda b, pt: (layer, pt[b, i], 0, col))

    page_specs = ([page_spec(i, 128, 0) for i in range(N_PAGES)] + [page_spec(i, 128, 1) for i in range(N_PAGES)]
                  + [page_spec(i, 256, 1) for i in range(N_PAGES)])

    c3 = lambda n, d: pl.BlockSpec((1, n, d), lambda b, pt: (b, 0, 0))
    z2 = lambda b, pt: (0, 0)
    grid_spec = pltpu.PrefetchScalarGridSpec(
        num_scalar_prefetch=1,
        grid=(nb,),
        in_specs=page_specs + [
            c3(C_HEADS, C_DH), c3(12, C_DH), c3(C_HEADS, 3), c3(C_HEADS, C_DH),
            pl.BlockSpec((None, 1, nw, 256), lambda b, pt: (layer, b, 0, 0)),
            pl.BlockSpec((1, C_DH), z2), pl.BlockSpec((1, C_DH), z2),
            pl.BlockSpec((CMP_LEN, 256), z2),
            pl.BlockSpec((CMP_LEN, 256, 256), lambda b, pt: (0, 0, 0)),
            pl.BlockSpec((256, 256), z2),
            pl.BlockSpec((128, 128), z2),
            pl.BlockSpec((128, past), z2)],
        out_specs=[c3(C_HEADS, C_DH), c3(C_HEADS, C_DH), pl.BlockSpec((1, nw, 256), lambda b, pt: (b, 0, 0))],
    )
    return pl.pallas_call(
        _nsa_decode_kernel,
        grid_spec=grid_spec,
        out_shape=[jax.ShapeDtypeStruct((nb, C_HEADS, C_DH), BF16),
                   jax.ShapeDtypeStruct((nb, C_HEADS, C_DH), F32),
                   jax.ShapeDtypeStruct((nb, nw, 256), F32)],
        compiler_params=_cparams("arbitrary"),
        name="nsa_decode",
    )(page_table, *([cache4] * (3 * N_PAGES)), q8, kv12, cg83, cz8, win_all, cos64, sin64, pos256, w1bd, w2bd, ov, ex)


def _merge_kernel(ya_ref, yb_ref, yc_ref, mg_ref, x_ref, gate_ref, wb_ref, wo_ref, lg_ref, lb_ref, o_ref):
    merged = None
    for n, y_ref in enumerate((ya_ref, yb_ref, yc_ref)):
        br = jnp.dot(y_ref[0], wb_ref[n], preferred_element_type=F32)
        term = _sigmoid(mg_ref[0, :, n * D_MODEL:(n + 1) * D_MODEL]) * br
        merged = term if merged is None else merged + term
    out = jnp.dot(merged.astype(BF16), wo_ref[...], preferred_element_type=F32)
    z = ALPHA * x_ref[0] + gate_ref[0] * out
    mu = jnp.mean(z, axis=-1, keepdims=True)
    zc = z - mu
    var = jnp.mean(zc * zc, axis=-1, keepdims=True)
    o_ref[0] = zc * lax.rsqrt(var + LN_EPS) * lg_ref[...] + lb_ref[...]


def _merge(ya, yb, yc, proj, x, gate, wb_bf, wo_bf, ln_g, ln_b, tm):
    B, T, _ = x.shape
    ys = pl.BlockSpec((1, tm, 512), lambda b, t: (b, t, 0))
    gs = (pl.BlockSpec((1, 1, D_MODEL), lambda b, t: (b, 0, 0)) if gate.shape[1] == 1
          else pl.BlockSpec((1, tm, D_MODEL), lambda b, t: (b, t, 0)))
    return pl.pallas_call(
        _merge_kernel,
        grid=(B, T // tm),
        in_specs=[ys, ys, ys,
                  pl.BlockSpec((1, tm, 3 * D_MODEL), lambda b, t: (b, t, COL_MG // (3 * D_MODEL))),
                  pl.BlockSpec((1, tm, D_MODEL), lambda b, t: (b, t, 0)),
                  gs,
                  pl.BlockSpec((3, 512, D_MODEL), lambda b, t: (0, 0, 0)),
                  pl.BlockSpec((D_MODEL, D_MODEL), lambda b, t: (0, 0)),
                  pl.BlockSpec((1, D_MODEL), lambda b, t: (0, 0)),
                  pl.BlockSpec((1, D_MODEL), lambda b, t: (0, 0))],
        out_specs=pl.BlockSpec((1, tm, D_MODEL), lambda b, t: (b, t, 0)),
        out_shape=jax.ShapeDtypeStruct((B, T, D_MODEL), F32),
        compiler_params=_cparams("arbitrary", "arbitrary"),
        name="merge",
    )(ya, yb, yc, proj, x, gate, wb_bf, wo_bf, ln_g, ln_b)


def _block_diag(blocks):
    n = len(blocks)
    d = blocks[0].shape[0]
    out = jnp.zeros((n * d, n * d), blocks[0].dtype)
    for i, b in enumerate(blocks):
        out = out.at[i * d:(i + 1) * d, i * d:(i + 1) * d].set(b)
    return out


def _rope_tables(pos, width):
    inv = ROPE_THETA ** (-jnp.arange(0, C_DH, 2, dtype=F32) / C_DH)
    ang = pos.astype(F32)[:, None] * inv[None, :]
    cos = jnp.cos(ang)
    sin = jnp.sin(ang)
    reps = width // C_DH
    return (jnp.tile(jnp.concatenate([cos, cos], axis=1), (1, reps)),
            jnp.tile(jnp.concatenate([-sin, sin], axis=1), (1, reps)))


def kernel(x_prompt, x_sample, c_prompt, c_sample, cache_nsa_kv, page_table, state_win_kv, state_hgrn,
           state_rglru, state_conv, ada_w, ada_b, w_in, a_lb, a_norm_g, b_conv_w, b_conv_b, b_gate_a_w,
           b_gate_a_b, b_gate_x_w, b_gate_x_b, b_lambda, c_cmp_pos, c_cmp_w1, c_cmp_w2, w_branch, w_out,
           ln_g, ln_b):
    bp, seq, _ = x_prompt.shape
    bs = x_sample.shape[0]
    n_pages = page_table.shape[1]
    past_len = n_pages * cache_nsa_kv.shape[2]
    assert n_pages == N_PAGES and cache_nsa_kv.shape[2] == PAGE_SIZE and x_sample.shape[1] == 1

    lb_all = jnp.cumsum(jax.nn.softmax(a_lb.astype(F32), axis=0), axis=0)
    lb_all = lb_all - lb_all[0:1]
    loglb = jnp.log(lb_all)
    l1mlb = jnp.log1p(-lb_all)
    w_in_p = jnp.concatenate(
        [w_in[:, :, 4888:7960], w_in[:, :, 0:3584], w_in[:, :, 4376:4888], w_in[:, :, 3584:4352],
         w_in[:, :, 4352:4376], jnp.zeros((DEPTH, D_MODEL, N_PROJ - 7960), w_in.dtype)], axis=2).astype(BF16)
    wb_bf = w_branch.astype(BF16)
    wo_bf = w_out.astype(BF16)
    cache4 = cache_nsa_kv.reshape(DEPTH, cache_nsa_kv.shape[1], PAGE_SIZE, 512)
    win_all = state_win_kv.reshape(DEPTH, bs, state_win_kv.shape[2], 256)
    cos_p, sin_p = _rope_tables(jnp.arange(seq), 128)
    cos_s, sin_s = _rope_tables(past_len + jnp.arange(1), C_DH)

    ada = _ada_all(jnp.concatenate([c_prompt, c_sample], axis=0), ada_w, ada_b)

    xp, xs = x_prompt, x_sample.reshape(1, bs, D_MODEL)
    outs = {k: [] for k in ("nsa_p", "nsa_s", "win_p", "win_s", "hg_p", "hg_s", "rg_p", "rg_s", "cv_p", "cv_s")}
    for l in range(DEPTH):
        shift, scale, gate = ada[l, :, 0:D_MODEL], ada[l, :, D_MODEL:2 * D_MODEL], ada[l, :, 2 * D_MODEL:]
        rg_w = (b_conv_w[l], b_conv_b[l].reshape(1, -1),
                _block_diag([b_gate_a_w[l, i] for i in range(B_BLOCKS)]).astype(BF16), b_gate_a_b[l].reshape(1, -1),
                _block_diag([b_gate_x_w[l, i] for i in range(B_BLOCKS)]).astype(BF16), b_gate_x_b[l].reshape(1, -1),
                b_lambda[l].reshape(1, -1))
        pos256 = jnp.concatenate([c_cmp_pos[l, 0], c_cmp_pos[l, 0], c_cmp_pos[l, 1], c_cmp_pos[l, 1]], axis=1)
        w1r = c_cmp_w1[l].reshape(2, CMP_LEN, C_DH, C_DH)
        w1bd = jax.vmap(lambda k, v: _block_diag([k, k, v, v]))(w1r[0], w1r[1]).astype(BF16)
        w2bd = _block_diag([c_cmp_w2[l, 0], c_cmp_w2[l, 0], c_cmp_w2[l, 1], c_cmp_w2[l, 1]]).astype(BF16)
        gain = a_norm_g[l].reshape(1, -1)
        lg, lbias = ln_g[l].reshape(1, -1), ln_b[l].reshape(1, -1)

        proj = _inproj(xp, scale[:bp, None], shift[:bp, None], w_in_p[l], 1024)
        ya, s_new = _hgrn_prompt(proj, loglb[l].reshape(1, -1), l1mlb[l].reshape(1, -1), gain)
        yb, h_new, cv_new = _rglru_prompt(proj, rg_w)
        rows, win, q_hm, qr_hm, ks_hm, vs_hm, kw_hm, vw_hm = _nsa_prep(proj, cos_p, sin_p)
        kcvc = _compress_prompt(rows, pos256, w1bd, w2bd)
        yc = _nsa_prompt(proj, q_hm, qr_hm, kcvc, ks_hm, vs_hm, kw_hm, vw_hm)
        xp = _merge(ya, yb, yc, proj, xp, gate[:bp, None], wb_bf[l], wo_bf[l], lg, lbias, 256)
        outs["nsa_p"].append(rows.reshape(bp, seq, 4, C_KV_HEADS, C_DH))
        outs["win_p"].append(win[:, seq - min(WINDOW, seq):].reshape(bp, -1, 2, C_KV_HEADS, C_DH))
        outs["hg_p"].append(s_new)
        outs["rg_p"].append(h_new.reshape(bp, B_WIDTH))
        outs["cv_p"].append(cv_new)

        proj_s = _inproj(xs, scale[None, bp:], shift[None, bp:], w_in_p[l], bs).reshape(bs, N_PROJ)
        s_s, ya_s = _hgrn_decode(state_hgrn, l, proj_s, loglb[l].reshape(A_HEADS, A_DK, 1),
                                 l1mlb[l].reshape(A_HEADS, A_DK, 1), gain)
        yb_s, h_s, cv_s = _rglru_decode(proj_s, state_conv[l].reshape(bs, -1), state_rglru[l], rg_w)
        q8 = proj_s[:, COL_CQ:COL_CQ + 512].reshape(bs, C_HEADS, C_DH)
        cz8 = proj_s[:, COL_CZ:COL_CZ + 512].reshape(bs, C_HEADS, C_DH)
        kv12 = proj_s[:, COL_CKV:COL_CKV + 768].reshape(bs, 12, C_DH)
        cg83 = proj_s[:, COL_CG:COL_CG + 3 * C_HEADS].reshape(bs, 3, C_HEADS).transpose(0, 2, 1)
        yc_s, rows_s, win_s = _nsa_decode(cache4, l, page_table, win_all, q8, kv12, cg83, cz8, cos_s, sin_s,
                                          pos256, w1bd, w2bd)
        xs = _merge(ya_s.reshape(1, bs, 512), yb_s.reshape(1, bs, 512), yc_s.reshape(1, bs, 512),
                    proj_s.reshape(1, bs, N_PROJ), xs, gate[None, bp:], wb_bf[l], wo_bf[l], lg, lbias, bs)
        outs["nsa_s"].append(rows_s.reshape(bs, 1, 4, C_KV_HEADS, C_DH))
        outs["win_s"].append(win_s.reshape(bs, -1, 2, C_KV_HEADS, C_DH))
        outs["hg_s"].append(s_s)
        outs["rg_s"].append(h_s)
        outs["cv_s"].append(cv_s.reshape(bs, CONV_W - 1, B_WIDTH))

    st = {k: jnp.stack(v, axis=0) for k, v in outs.items()}
    return (xp, xs.reshape(bs, 1, D_MODEL), st["nsa_p"], st["nsa_s"], st["win_p"], st["win_s"],
            st["hg_p"], st["hg_s"], st["rg_p"], st["rg_s"], st["cv_p"], st["cv_s"])
```

```python
import numpy as np
import jax
import jax.numpy as jnp
from jax import lax
from jax.experimental import pallas as pl
from jax.experimental.pallas import tpu as pltpu

F32 = jnp.float32
BF16 = jnp.bfloat16

D_MODEL = 1024
DEPTH = 4
PAGE_SIZE = 128
A_HEADS = 4
A_DK = 128
A_CHUNK = 64
B_WIDTH = 512
B_BLOCKS = 8
CONV_W = 4
LRU_C = 8.0
C_HEADS = 8
C_KV_HEADS = 2
C_GROUP = 4
C_DH = 64
CMP_LEN = 32
CMP_STRIDE = 16
SLC_BLK = 64
SLC_TOPN = 16
WINDOW = 512
ROPE_THETA = 10000.0
ATT_SCALE = C_DH ** -0.5
ALPHA = (2 * DEPTH) ** 0.25
LN_EPS = 1e-5
NEG_INF = -1e30
FORCED_SCORE = 1e9

COL_MG, COL_AQ, COL_AF, COL_AI, COL_AZ = 0, 3072, 3584, 4096, 4608
COL_BX, COL_BZ, COL_CQ, COL_CZ, COL_CKV, COL_CG = 5120, 5632, 6144, 6656, 7168, 7936
N_PROJ = 8064
PROJ_TN = 1152

VMEM_LIMIT = 56 * 1024 * 1024

NT_DIMS = (((1,), (1,)), ((), ()))
TN_DIMS = (((0,), (0,)), ((), ()))


def _cparams(*sem):
    return pltpu.CompilerParams(dimension_semantics=sem, vmem_limit_bytes=VMEM_LIMIT)


def _sigmoid(x):
    return jax.nn.sigmoid(x)


def _silu(x):
    return x * jax.nn.sigmoid(x)


def _log_f(af, loglb, l1mlb):
    ls = jnp.minimum(af, 0.0) - jnp.log1p(jnp.exp(-jnp.abs(af)))
    y = l1mlb + ls
    return jnp.maximum(loglb, y) + jnp.log1p(jnp.exp(-jnp.abs(loglb - y)))


def _ada_kernel(c_ref, w_ref, b_ref, o_ref):
    o_ref[0] = jnp.dot(c_ref[...], w_ref[0].astype(BF16), preferred_element_type=F32) + b_ref[0]


def _ada_all(c_all, ada_w, ada_b):
    nb = c_all.shape[0]
    return pl.pallas_call(
        _ada_kernel,
        grid=(DEPTH, 3),
        in_specs=[
            pl.BlockSpec((nb, D_MODEL), lambda l, j: (0, 0)),
            pl.BlockSpec((1, D_MODEL, D_MODEL), lambda l, j: (l, 0, j)),
            pl.BlockSpec((1, 1, D_MODEL), lambda l, j: (l, 0, j)),
        ],
        out_specs=pl.BlockSpec((1, nb, D_MODEL), lambda l, j: (l, 0, j)),
        out_shape=jax.ShapeDtypeStruct((DEPTH, nb, 3 * D_MODEL), F32),
        compiler_params=_cparams("arbitrary", "arbitrary"),
        name="ada",
    )(c_all.astype(BF16), ada_w, ada_b.reshape(DEPTH, 1, 3 * D_MODEL))


def _inproj_kernel(x_ref, sc_ref, sh_ref, w_ref, o_ref, h_ref):
    @pl.when(pl.program_id(2) == 0)
    def _():
        h_ref[...] = (x_ref[0] * (1.0 + sc_ref[0]) + sh_ref[0]).astype(BF16)

    o_ref[0] = jnp.dot(h_ref[...], w_ref[...], preferred_element_type=F32)


def _inproj(x, scale, shift, w_bf, tt):
    B, T, _ = x.shape
    ts = scale.shape[1]
    mod_spec = (pl.BlockSpec((1, 1, D_MODEL), lambda b, t, n: (b, 0, 0)) if ts == 1
                else pl.BlockSpec((1, tt, D_MODEL), lambda b, t, n: (b, t, 0)))
    return pl.pallas_call(
        _inproj_kernel,
        grid=(B, T // tt, N_PROJ // PROJ_TN),
        in_specs=[
            pl.BlockSpec((1, tt, D_MODEL), lambda b, t, n: (b, t, 0)),
            mod_spec, mod_spec,
            pl.BlockSpec((D_MODEL, PROJ_TN), lambda b, t, n: (0, n)),
        ],
        out_specs=pl.BlockSpec((1, tt, PROJ_TN), lambda b, t, n: (b, t, n)),
        out_shape=jax.ShapeDtypeStruct((B, T, N_PROJ), F32),
        scratch_shapes=[pltpu.VMEM((tt, D_MODEL), BF16)],
        compiler_params=_cparams("arbitrary", "arbitrary", "arbitrary"),
        name="inproj",
    )(x, scale, shift, w_bf)


HG_TT = 256
HG_LEVELS = (1, 2, 4, 8, 16, 32)


def _hgrn_level_table():
    t = np.arange(HG_TT)[:, None]
    s = np.arange(HG_TT)[None, :]
    lvl = np.full((HG_TT, HG_TT), -1, np.int32)
    same = (t // A_CHUNK) == (s // A_CHUNK)
    for i, h in enumerate(HG_LEVELS):
        m = same & ((t // h) == (s // h) + 1) & (((s // h) % 2) == 0)
        lvl[m] = i
    lvl[t == s] = len(HG_LEVELS)
    return lvl


def _hgrn_prompt_kernel(aq_ref, af_ref, ai_ref, az_ref, llb_ref, l1_ref, g_ref, lvl_ref,
                        ya_ref, so_ref, s_ref):
    t = pl.program_id(2)

    @pl.when(t == 0)
    def _():
        s_ref[...] = jnp.zeros_like(s_ref)

    aq = aq_ref[0]
    v = ai_ref[0]
    az = az_ref[0]
    q = _silu(aq)
    lf = _log_f(af_ref[0], llb_ref[...], l1_ref[...])
    k = 1.0 - jnp.exp(lf)
    vb = v.astype(BF16)
    row = lax.broadcasted_iota(jnp.int32, (HG_TT, A_DK), 0)
    lvl = lvl_ref[...]

    cin = lf
    rex = jnp.zeros_like(lf)
    att = jnp.where(lvl == len(HG_LEVELS), jnp.sum(q * k, axis=-1, keepdims=True), 0.0)
    for i, h in enumerate(HG_LEVELS):
        qh = (q * jnp.exp(cin)).astype(BF16)
        kh = (k * jnp.exp(rex)).astype(BF16)
        a = lax.dot_general(qh, kh, NT_DIMS, preferred_element_type=F32)
        att = jnp.where(lvl == i, a, att)
        tot = cin + rex
        upper = ((row // h) % 2) == 1
        cin = cin + jnp.where(upper, pltpu.roll(tot, h, 0), 0.0)
        rex = rex + jnp.where(upper, 0.0, pltpu.roll(tot, HG_TT - h, 0))

    o = jnp.dot(att.astype(BF16), vb, preferred_element_type=F32)
    qb = (q * jnp.exp(cin)).astype(BF16)
    kd = (k * jnp.exp(rex)).astype(BF16)
    tot = cin + rex
    s = s_ref[...]
    inter = []
    for c in range(HG_TT // A_CHUNK):
        sl = slice(c * A_CHUNK, (c + 1) * A_CHUNK)
        inter.append(jnp.dot(qb[sl], s.astype(BF16), preferred_element_type=F32))
        u = lax.dot_general(kd[sl], vb[sl], TN_DIMS, preferred_element_type=F32)
        e_rows = jnp.broadcast_to(jnp.exp(tot[c * A_CHUNK:c * A_CHUNK + 1]), (A_DK, A_DK))
        s = e_rows.T * s + u
    s_ref[...] = s
    o = o + jnp.concatenate(inter, axis=0)

    ms = jnp.mean(o * o, axis=-1, keepdims=True)
    ya_ref[0] = ((o * lax.rsqrt(ms + LN_EPS)) * g_ref[...] * _silu(az)).astype(BF16)

    @pl.when(t == pl.num_programs(2) - 1)
    def _():
        so_ref[0, 0] = s


def _hgrn_prompt(proj, loglb, l1mlb, gain):
    B, T, _ = proj.shape
    lvl = jnp.asarray(_hgrn_level_table())

    def col(base):
        return pl.BlockSpec((1, HG_TT, A_DK), lambda b, h, t: (b, t, base // A_DK + h))

    vec = pl.BlockSpec((1, A_DK), lambda b, h, t: (0, h))
    return pl.pallas_call(
        _hgrn_prompt_kernel,
        grid=(B, A_HEADS, T // HG_TT),
        in_specs=[col(COL_AQ), col(COL_AF), col(COL_AI), col(COL_AZ), vec, vec, vec,
                  pl.BlockSpec((HG_TT, HG_TT), lambda b, h, t: (0, 0))],
        out_specs=[pl.BlockSpec((1, HG_TT, A_DK), lambda b, h, t: (b, t, h)),
                   pl.BlockSpec((1, 1, A_DK, A_DK), lambda b, h, t: (b, h, 0, 0))],
        out_shape=[jax.ShapeDtypeStruct((B, T, A_HEADS * A_DK), BF16),
                   jax.ShapeDtypeStruct((B, A_HEADS, A_DK, A_DK), F32)],
        scratch_shapes=[pltpu.VMEM((A_DK, A_DK), F32)],
        compiler_params=_cparams("arbitrary", "arbitrary", "arbitrary"),
        name="hgrn_prompt",
    )(proj, proj, proj, proj, loglb, l1mlb, gain, lvl)


HG_SB = 16


def _hgrn_decode_kernel(s_ref, aqc_ref, afc_ref, ai_ref, az_ref, llb_ref, l1_ref, g_ref, so_ref, ya_ref):
    aqc = aqc_ref[0, 0]
    qc = _silu(aqc)
    lf = _log_f(afc_ref[0, 0], llb_ref[0], l1_ref[0])
    ec = jnp.exp(lf)
    kc = 1.0 - ec
    v = ai_ref[...]
    rows = []
    for r in range(HG_SB):
        sn = ec[:, r:r + 1] * s_ref[r, 0] + kc[:, r:r + 1] * v[r:r + 1, :]
        so_ref[r, 0] = sn
        rows.append(jnp.sum(qc[:, r:r + 1] * sn, axis=0, keepdims=True))
    o = jnp.concatenate(rows, axis=0)
    ms = jnp.mean(o * o, axis=-1, keepdims=True)
    ya_ref[...] = ((o * lax.rsqrt(ms + LN_EPS)) * g_ref[...] * _silu(az_ref[...])).astype(BF16)


def _hgrn_decode(state_all, layer, proj_s, loglb_c, l1mlb_c, gain):
    nb = proj_s.shape[0]
    nblk = nb // HG_SB

    def cols(base):
        a = proj_s[:, base:base + A_HEADS * A_DK].reshape(nblk, HG_SB, A_HEADS, A_DK)
        return a.transpose(2, 0, 3, 1)

    st_spec = pl.BlockSpec((None, HG_SB, 1, A_DK, A_DK), lambda h, i: (layer, i, h, 0, 0))
    col_spec = pl.BlockSpec((1, 1, A_DK, HG_SB), lambda h, i: (h, i, 0, 0))
    cvec = pl.BlockSpec((1, A_DK, 1), lambda h, i: (h, 0, 0))
    return pl.pallas_call(
        _hgrn_decode_kernel,
        grid=(A_HEADS, nblk),
        in_specs=[st_spec, col_spec, col_spec,
                  pl.BlockSpec((HG_SB, A_DK), lambda h, i: (i, COL_AI // A_DK + h)),
                  pl.BlockSpec((HG_SB, A_DK), lambda h, i: (i, COL_AZ // A_DK + h)),
                  cvec, cvec,
                  pl.BlockSpec((1, A_DK), lambda h, i: (0, h))],
        out_specs=[pl.BlockSpec((HG_SB, 1, A_DK, A_DK), lambda h, i: (i, h, 0, 0)),
                   pl.BlockSpec((HG_SB, A_DK), lambda h, i: (i, h))],
        out_shape=[jax.ShapeDtypeStruct((nb, A_HEADS, A_DK, A_DK), F32),
                   jax.ShapeDtypeStruct((nb, A_HEADS * A_DK), BF16)],
        compiler_params=_cparams("arbitrary", "arbitrary"),
        name="hgrn_decode",
    )(state_all, cols(COL_AQ), cols(COL_AF), proj_s, proj_s, loglb_c, l1mlb_c, gain)


RG_TT = 256


def _rglru_gates(u, ga_ref, gab_ref, gx_ref, gxb_ref, lam_ref):
    ub = u.astype(BF16)
    r = _sigmoid(jnp.dot(ub, ga_ref[...], preferred_element_type=F32) + gab_ref[...])
    ig = _sigmoid(jnp.dot(ub, gx_ref[...], preferred_element_type=F32) + gxb_ref[...])
    nl = -lam_ref[...]
    softplus = jnp.maximum(nl, 0.0) + jnp.log1p(jnp.exp(-jnp.abs(nl)))
    a = jnp.exp(-LRU_C * r * softplus)
    return a, jnp.sqrt(1.0 - a * a) * (ig * u)


def _rglru_prompt_kernel(bx_ref, bz_ref, cw_ref, cb_ref, ga_ref, gab_ref, gx_ref, gxb_ref, lam_ref,
                         yb_ref, h_ref, cv_ref, ext_ref, hc_ref):
    t = pl.program_id(1)

    @pl.when(t == 0)
    def _():
        ext_ref[0:8, :] = jnp.zeros((8, B_WIDTH), F32)
        hc_ref[...] = jnp.zeros_like(hc_ref)

    x = bx_ref[0]
    ext_ref[8:8 + RG_TT, :] = x
    u = cb_ref[...] + (ext_ref[5:5 + RG_TT, :] * cw_ref[0:1, :] + ext_ref[6:6 + RG_TT, :] * cw_ref[1:2, :]
                       + ext_ref[7:7 + RG_TT, :] * cw_ref[2:3, :] + x * cw_ref[3:4, :])
    ext_ref[0:8, :] = x[RG_TT - 8:RG_TT, :]
    a, bt = _rglru_gates(u, ga_ref, gab_ref, gx_ref, gxb_ref, lam_ref)
    row = lax.broadcasted_iota(jnp.int32, (RG_TT, 128), 0)
    z = bz_ref[0]
    for c in range(B_WIDTH // 128):
        cs = slice(c * 128, (c + 1) * 128)
        ac = a[:, cs]
        bc = bt[:, cs] + jnp.where(row == 0, ac * hc_ref[0:1, cs], 0.0)
        s = 1
        while s < RG_TT:
            keep = row >= s
            bc = jnp.where(keep, ac * pltpu.roll(bc, s, 0) + bc, bc)
            ac = jnp.where(keep, ac * pltpu.roll(ac, s, 0), ac)
            s *= 2
        hc_ref[0:1, cs] = bc[RG_TT - 1:RG_TT, :]
        yb_ref[0, :, cs] = (bc * _silu(z[:, cs])).astype(BF16)

    @pl.when(t == pl.num_programs(1) - 1)
    def _():
        h_ref[0] = hc_ref[0:1, :]
        cv_ref[0] = x[RG_TT - (CONV_W - 1):RG_TT, :]


def _rglru_weight_specs(idx):
    return [pl.BlockSpec((CONV_W, B_WIDTH), idx), pl.BlockSpec((1, B_WIDTH), idx),
            pl.BlockSpec((B_WIDTH, B_WIDTH), idx), pl.BlockSpec((1, B_WIDTH), idx),
            pl.BlockSpec((B_WIDTH, B_WIDTH), idx), pl.BlockSpec((1, B_WIDTH), idx),
            pl.BlockSpec((1, B_WIDTH), idx)]


def _rglru_prompt(proj, w):
    B, T, _ = proj.shape
    return pl.pallas_call(
        _rglru_prompt_kernel,
        grid=(B, T // RG_TT),
        in_specs=[pl.BlockSpec((1, RG_TT, B_WIDTH), lambda b, t: (b, t, COL_BX // B_WIDTH)),
                  pl.BlockSpec((1, RG_TT, B_WIDTH), lambda b, t: (b, t, COL_BZ // B_WIDTH))]
        + _rglru_weight_specs(lambda b, t: (0, 0)),
        out_specs=[pl.BlockSpec((1, RG_TT, B_WIDTH), lambda b, t: (b, t, 0)),
                   pl.BlockSpec((1, 1, B_WIDTH), lambda b, t: (b, 0, 0)),
                   pl.BlockSpec((1, CONV_W - 1, B_WIDTH), lambda b, t: (b, 0, 0))],
        out_shape=[jax.ShapeDtypeStruct((B, T, B_WIDTH), BF16),
                   jax.ShapeDtypeStruct((B, 1, B_WIDTH), F32),
                   jax.ShapeDtypeStruct((B, CONV_W - 1, B_WIDTH), F32)],
        scratch_shapes=[pltpu.VMEM((8 + RG_TT, B_WIDTH), F32), pltpu.VMEM((8, B_WIDTH), F32)],
        compiler_params=_cparams("arbitrary", "arbitrary"),
        name="rglru_prompt",
    )(proj, proj, *w)


def _rglru_decode_kernel(bx_ref, bz_ref, cv0_ref, h0_ref, cw_ref, cb_ref, ga_ref, gab_ref, gx_ref, gxb_ref,
                         lam_ref, yb_ref, h_ref, cv_ref):
    x = bx_ref[...]
    c0 = cv0_ref[:, 0:B_WIDTH]
    c1 = cv0_ref[:, B_WIDTH:2 * B_WIDTH]
    c2 = cv0_ref[:, 2 * B_WIDTH:3 * B_WIDTH]
    u = cb_ref[...] + (c0 * cw_ref[0:1, :] + c1 * cw_ref[1:2, :] + c2 * cw_ref[2:3, :] + x * cw_ref[3:4, :])
    a, bt = _rglru_gates(u, ga_ref, gab_ref, gx_ref, gxb_ref, lam_ref)
    h = a * h0_ref[...] + bt
    h_ref[...] = h
    yb_ref[...] = (h * _silu(bz_ref[...])).astype(BF16)
    cv_ref[:, 0:B_WIDTH] = c1
    cv_ref[:, B_WIDTH:2 * B_WIDTH] = c2
    cv_ref[:, 2 * B_WIDTH:3 * B_WIDTH] = x


def _rglru_decode(proj_s, conv0, h0, w):
    nb = proj_s.shape[0]
    z2 = lambda i: (0, 0)
    return pl.pallas_call(
        _rglru_decode_kernel,
        grid=(1,),
        in_specs=[pl.BlockSpec((nb, B_WIDTH), lambda i: (0, COL_BX // B_WIDTH)),
                  pl.BlockSpec((nb, B_WIDTH), lambda i: (0, COL_BZ // B_WIDTH)),
                  pl.BlockSpec((nb, 3 * B_WIDTH), z2), pl.BlockSpec((nb, B_WIDTH), z2)]
        + _rglru_weight_specs(z2),
        out_specs=[pl.BlockSpec((nb, B_WIDTH), z2), pl.BlockSpec((nb, B_WIDTH), z2),
                   pl.BlockSpec((nb, 3 * B_WIDTH), z2)],
        out_shape=[jax.ShapeDtypeStruct((nb, B_WIDTH), BF16), jax.ShapeDtypeStruct((nb, B_WIDTH), F32),
                   jax.ShapeDtypeStruct((nb, 3 * B_WIDTH), F32)],
        compiler_params=_cparams("arbitrary"),
        name="rglru_decode",
    )(proj_s, proj_s, conv0, h0, *w)


def _gelu_tanh(x):
    return 0.5 * x * (1.0 + jnp.tanh(0.7978845608028654 * (x + 0.044715 * (x * x * x))))


def _compress(load_l, pos_ref, w1_ref, w2_ref, m):
    acc_a = jnp.zeros((m, 256), F32)
    acc_b = jnp.zeros((m, 256), F32)
    for l in range(CMP_STRIDE):
        xl = load_l(l)
        acc_a = acc_a + jnp.dot((xl + pos_ref[l:l + 1, :]).astype(BF16), w1_ref[l], preferred_element_type=F32)
        acc_b = acc_b + jnp.dot((xl + pos_ref[CMP_STRIDE + l:CMP_STRIDE + l + 1, :]).astype(BF16),
                                w1_ref[CMP_STRIDE + l], preferred_element_type=F32)
    hid = _gelu_tanh(acc_a + pltpu.roll(acc_b, m - 1, 0))
    return jnp.dot(hid.astype(BF16), w2_ref[...], preferred_element_type=F32)


def _rank_select(score, idx, n_rows, axis):
    rank = jnp.zeros_like(score)
    for jp in range(n_rows):
        other = score[jp:jp + 1, :] if axis == 0 else score[:, jp:jp + 1]
        rank = rank + jnp.where(other > score, 1.0, jnp.where(other == score, jnp.where(idx > jp, 1.0, 0.0), 0.0))
    return jnp.where(rank < float(SLC_TOPN), 1.0, 0.0)


def _forced_score(idx, cur, imp):
    return jnp.where(idx == 0, FORCED_SCORE,
                     jnp.where(idx == cur, FORCED_SCORE,
                               jnp.where(idx == cur - 1, FORCED_SCORE, jnp.where(idx <= cur, imp, -1.0))))


NP_TT = 256


def _nsa_prep_kernel(cq_ref, kv01_ref, kv23_ref, kv45_ref, cos_ref, sin_ref,
                     rows_t_ref, win_t_ref, raw_ref, q_ref, qr_ref, ks_ref, kw_ref):
    cos = cos_ref[...]
    sin = sin_ref[...]
    lane = lax.broadcasted_iota(jnp.int32, cos.shape, 1)
    first = (lane % C_DH) < (C_DH // 2)

    def rope(x):
        return x * cos + jnp.where(first, pltpu.roll(x, 128 - C_DH // 2, 1), pltpu.roll(x, C_DH // 2, 1)) * sin

    def heads(ref, x, h0):
        ref[0, h0] = x[:, 0:C_DH].astype(BF16)
        ref[0, h0 + 1] = x[:, C_DH:2 * C_DH].astype(BF16)

    cq = cq_ref[0] * ATT_SCALE
    for c in range(C_HEADS // 2):
        x = cq[:, c * 128:(c + 1) * 128]
        heads(q_ref, x, 2 * c)
        heads(qr_ref, rope(x), 2 * c)
    kv01 = kv01_ref[0]
    kv23 = kv23_ref[0]
    kv45 = kv45_ref[0]
    ks = rope(kv23[:, 0:128])
    kw = rope(kv45[:, 0:128])
    raw_ref[0] = kv01
    heads(ks_ref, ks, 0)
    heads(kw_ref, kw, 0)
    rows_t_ref[0, 0:128, :] = kv01[:, 0:128].T
    rows_t_ref[0, 128:256, :] = kv01[:, 128:256].T
    rows_t_ref[0, 256:384, :] = ks.T
    rows_t_ref[0, 384:512, :] = kv23[:, 128:256].T
    win_t_ref[0, 0:128, :] = kw.T
    win_t_ref[0, 128:256, :] = kv45[:, 128:256].T


def _nsa_prep(proj, cos128, sin128):
    B, T, _ = proj.shape
    hm = lambda n: pl.BlockSpec((1, n, NP_TT, C_DH), lambda b, t: (b, 0, t, 0))
    hs = lambda n: jax.ShapeDtypeStruct((B, n, T, C_DH), BF16)
    kvb = COL_CKV // 256
    return pl.pallas_call(
        _nsa_prep_kernel,
        grid=(B, T // NP_TT),
        in_specs=[pl.BlockSpec((1, NP_TT, 512), lambda b, t: (b, t, COL_CQ // 512)),
                  pl.BlockSpec((1, NP_TT, 256), lambda b, t: (b, t, kvb)),
                  pl.BlockSpec((1, NP_TT, 256), lambda b, t: (b, t, kvb + 1)),
                  pl.BlockSpec((1, NP_TT, 256), lambda b, t: (b, t, kvb + 2)),
                  pl.BlockSpec((NP_TT, 128), lambda b, t: (t, 0)),
                  pl.BlockSpec((NP_TT, 128), lambda b, t: (t, 0))],
        out_specs=[pl.BlockSpec((1, 512, NP_TT), lambda b, t: (b, 0, t)),
                   pl.BlockSpec((1, 256, NP_TT), lambda b, t: (b, 0, t)),
                   pl.BlockSpec((1, NP_TT, 256), lambda b, t: (b, t, 0)),
                   hm(8), hm(8), hm(2), hm(2)],
        out_shape=[jax.ShapeDtypeStruct((B, 512, T), F32), jax.ShapeDtypeStruct((B, 256, T), F32),
                   jax.ShapeDtypeStruct((B, T, 256), F32), hs(8), hs(8), hs(2), hs(2)],
        compiler_params=_cparams("arbitrary", "arbitrary"),
        name="nsa_prep",
    )(proj, proj, proj, proj, cos128, sin128)


def _compress_prompt_kernel(xk_ref, xv_ref, pos_ref, w1_ref, w2_ref, kc_ref, vct_ref):
    def load_l(l):
        return jnp.concatenate([xk_ref[pl.ds(l, 128, stride=CMP_STRIDE), :],
                                xv_ref[pl.ds(l, 128, stride=CMP_STRIDE), :]], axis=1)

    out = _compress(load_l, pos_ref, w1_ref, w2_ref, 128)
    vt = out[:, 128:256].T
    for g in range(C_KV_HEADS):
        kc_ref[0, g] = out[:, g * C_DH:(g + 1) * C_DH].astype(BF16)
        vct_ref[0, g] = vt[g * C_DH:(g + 1) * C_DH, :].astype(BF16)


def _compress_prompt(raw, pos256, w1bd, w2bd):
    B, T, _ = raw.shape
    return pl.pallas_call(
        _compress_prompt_kernel,
        grid=(B,),
        in_specs=[pl.BlockSpec((None, T, 128), lambda b: (b, 0, 0)),
                  pl.BlockSpec((None, T, 128), lambda b: (b, 0, 1)),
                  pl.BlockSpec((CMP_LEN, 256), lambda b: (0, 0)),
                  pl.BlockSpec((CMP_LEN, 256, 256), lambda b: (0, 0, 0)),
                  pl.BlockSpec((256, 256), lambda b: (0, 0))],
        out_specs=[pl.BlockSpec((1, C_KV_HEADS, 128, C_DH), lambda b: (b, 0, 0, 0)),
                   pl.BlockSpec((1, C_KV_HEADS, C_DH, 128), lambda b: (b, 0, 0, 0))],
        out_shape=[jax.ShapeDtypeStruct((B, C_KV_HEADS, 128, C_DH), BF16),
                   jax.ShapeDtypeStruct((B, C_KV_HEADS, C_DH, 128), BF16)],
        compiler_params=_cparams("arbitrary"),
        name="compress_prompt",
    )(raw, raw, pos256, w1bd, w2bd)


NA_TQ = 128
NA_KT = 256
NA_M = C_GROUP * NA_TQ


def _lanes4(x):
    return jnp.concatenate([x] * C_GROUP, axis=1)


def _flash_t(qs, k_ref, vt_ref, j_lo, j_hi, mask_fn, m_ref, l_ref, acc_ref):
    m_ref[...] = jnp.full(m_ref.shape, NEG_INF, F32)
    l_ref[...] = jnp.zeros_like(l_ref)
    acc_ref[...] = jnp.zeros_like(acc_ref)

    def body(j, carry):
        off = pl.multiple_of(j * NA_KT, NA_KT)
        for g in range(C_KV_HEADS):
            k = k_ref[0, g, pl.ds(off, NA_KT), :]
            vt = vt_ref[0, g * C_DH:(g + 1) * C_DH, pl.ds(off, NA_KT)].astype(BF16)
            msk = _lanes4(mask_fn(j, g)) > 0.5
            s = jnp.where(msk, lax.dot_general(k, qs[g], NT_DIMS, preferred_element_type=F32), NEG_INF)
            m_prev = m_ref[g, 0:1, :]
            m_new = jnp.maximum(m_prev, jnp.max(s, axis=0, keepdims=True))
            p = jnp.where(msk, jnp.exp(s - m_new), 0.0)
            alpha = jnp.exp(m_prev - m_new)
            l_ref[g, 0:1, :] = alpha * l_ref[g, 0:1, :] + jnp.sum(p, axis=0, keepdims=True)
            acc_ref[g] = alpha * acc_ref[g] + jnp.dot(vt, p.astype(BF16), preferred_element_type=F32)
            m_ref[g, 0:1, :] = m_new
        return carry

    lax.fori_loop(j_lo, j_hi, body, 0)
    return [acc_ref[g] / l_ref[g, 0:1, :] for g in range(C_KV_HEADS)]


def _nsa_prompt_kernel(q_ref, qr_ref, kc_ref, vct_ref, ks_ref, vst_ref, kw_ref, vwt_ref, cg_ref, cz_ref, ovt_ref,
                       yc_ref, m_ref, l_ref, acc_ref, sel_ref):
    qi = pl.program_id(1)
    t0 = qi * NA_TQ
    n_slc = ovt_ref.shape[0]
    sig_t = _sigmoid(cg_ref[0]).T
    cmp_row = lax.broadcasted_iota(jnp.int32, (128, NA_TQ), 0)
    cmp_q = t0 + lax.broadcasted_iota(jnp.int32, (128, NA_TQ), 1)
    cmp_ok = _lanes4(jnp.where(CMP_STRIDE * cmp_row + (CMP_LEN - 1) <= cmp_q, 1.0, 0.0)) > 0.5
    jidx = lax.broadcasted_iota(jnp.int32, (n_slc, NA_TQ), 0)
    cur = (t0 + lax.broadcasted_iota(jnp.int32, (n_slc, NA_TQ), 1)) // SLC_BLK
    qrs, o_cmp = [], []
    for g in range(C_KV_HEADS):
        qg = q_ref[0, C_GROUP * g:C_GROUP * (g + 1)].reshape(NA_M, C_DH)
        qrs.append(qr_ref[0, C_GROUP * g:C_GROUP * (g + 1)].reshape(NA_M, C_DH))
        s = jnp.where(cmp_ok, lax.dot_general(kc_ref[0, g], qg, NT_DIMS, preferred_element_type=F32), NEG_INF)
        e = jnp.where(cmp_ok, jnp.exp(s - jnp.max(s, axis=0, keepdims=True)), 0.0)
        p = e / jnp.maximum(jnp.sum(e, axis=0, keepdims=True), 1e-30)
        o_cmp.append(jnp.dot(vct_ref[0, g], p.astype(BF16), preferred_element_type=F32))
        psum = p[:, 0:NA_TQ] + p[:, NA_TQ:2 * NA_TQ] + p[:, 2 * NA_TQ:3 * NA_TQ] + p[:, 3 * NA_TQ:4 * NA_TQ]
        imp = jnp.dot(ovt_ref[...], psum, precision=lax.Precision.HIGHEST, preferred_element_type=F32)
        sel_t = _rank_select(_forced_score(jidx, cur, imp), jidx, n_slc, 0)
        for jj in range(n_slc):
            sel_ref[g, jj] = sel_t[jj:jj + 1, :]

    key_row = lax.broadcasted_iota(jnp.int32, (NA_KT, NA_TQ), 0)
    q_pos = t0 + lax.broadcasted_iota(jnp.int32, (NA_KT, NA_TQ), 1)
    blocks_per_tile = NA_KT // SLC_BLK

    def slc_mask(j, g):
        chosen = sel_ref[g, blocks_per_tile * j + blocks_per_tile - 1]
        for i in range(blocks_per_tile - 2, -1, -1):
            chosen = jnp.where(key_row < (i + 1) * SLC_BLK, sel_ref[g, blocks_per_tile * j + i], chosen)
        return jnp.where(j * NA_KT + key_row <= q_pos, chosen, 0.0)

    def win_mask(j, g):
        kpos = j * NA_KT + key_row
        return jnp.where(kpos <= q_pos, jnp.where(kpos > q_pos - WINDOW, 1.0, 0.0), 0.0)

    j_hi = (t0 + NA_TQ + NA_KT - 1) // NA_KT
    o_slc = _flash_t(qrs, ks_ref, vst_ref, 0, j_hi, slc_mask, m_ref, l_ref, acc_ref)
    o_win = _flash_t(qrs, kw_ref, vwt_ref, jnp.maximum(t0 - (WINDOW - 1), 0) // NA_KT, j_hi, win_mask,
                     m_ref, l_ref, acc_ref)
    heads_t = []
    for g in range(C_KV_HEADS):
        for hh in range(C_GROUP):
            hd = C_GROUP * g + hh
            cs = slice(hh * NA_TQ, (hh + 1) * NA_TQ)
            heads_t.append(sig_t[hd:hd + 1, :] * o_cmp[g][:, cs]
                           + sig_t[C_HEADS + hd:C_HEADS + hd + 1, :] * o_slc[g][:, cs]
                           + sig_t[2 * C_HEADS + hd:2 * C_HEADS + hd + 1, :] * o_win[g][:, cs])
    yc_ref[0] = (jnp.concatenate(heads_t, axis=0).T * _silu(cz_ref[0])).astype(BF16)


def _overlap_t(n_slc, n_rows):
    cs = np.arange(128) * CMP_STRIDE
    ss = np.arange(n_rows) * SLC_BLK
    ov = (cs[None, :] < ss[:, None] + SLC_BLK) & (cs[None, :] + CMP_LEN > ss[:, None])
    ov &= (np.arange(128)[None, :] < 127) & (np.arange(n_rows)[:, None] < n_slc)
    return ov.astype(np.float32)


def _nsa_prompt(proj, q_hm, qr_hm, kc, vct, ks_hm, rows_t, kw_hm, win_t):
    B, T, _ = proj.shape
    n_slc = T // SLC_BLK
    ovt = jnp.asarray(_overlap_t(n_slc, n_slc))
    qspec = pl.BlockSpec((1, C_HEADS, NA_TQ, C_DH), lambda b, t: (b, 0, t, 0))
    kspec = pl.BlockSpec((1, C_KV_HEADS, T, C_DH), lambda b, t: (b, 0, 0, 0))
    return pl.pallas_call(
        _nsa_prompt_kernel,
        grid=(B, T // NA_TQ),
        in_specs=[qspec, qspec,
                  pl.BlockSpec((1, C_KV_HEADS, 128, C_DH), lambda b, t: (b, 0, 0, 0)),
                  pl.BlockSpec((1, C_KV_HEADS, C_DH, 128), lambda b, t: (b, 0, 0, 0)),
                  kspec, pl.BlockSpec((1, 128, T), lambda b, t: (b, 3, 0)),
                  kspec, pl.BlockSpec((1, 128, T), lambda b, t: (b, 1, 0)),
                  pl.BlockSpec((1, NA_TQ, 128), lambda b, t: (b, t, COL_CG // 128)),
                  pl.BlockSpec((1, NA_TQ, 512), lambda b, t: (b, t, COL_CZ // 512)),
                  pl.BlockSpec((n_slc, 128), lambda b, t: (0, 0))],
        out_specs=pl.BlockSpec((1, NA_TQ, 512), lambda b, t: (b, t, 0)),
        out_shape=jax.ShapeDtypeStruct((B, T, 512), BF16),
        scratch_shapes=[pltpu.VMEM((C_KV_HEADS, 8, NA_M), F32), pltpu.VMEM((C_KV_HEADS, 8, NA_M), F32),
                        pltpu.VMEM((C_KV_HEADS, C_DH, NA_M), F32), pltpu.VMEM((C_KV_HEADS, n_slc, 1, NA_TQ), F32)],
        compiler_params=_cparams("arbitrary", "arbitrary"),
        name="nsa_prompt",
    )(q_hm, qr_hm, kc, vct, ks_hm, rows_t, kw_hm, win_t, proj, proj, ovt)


N_PAGES = 16
ND_SB = 2


def _nsa_decode_kernel(pt_ref, *refs):
    npg = ND_SB * N_PAGES
    pages = refs[:npg]
    (q_ref, kv_ref, cg_ref, cz_ref, wb_ref, cos_ref, sin_ref, pos_ref, w1_ref, w2_ref, ov_ref, ex_ref,
     yc_ref, rows_ref, wn_ref, xk_ref, xv_ref) = refs[npg:]
    past = N_PAGES * PAGE_SIZE
    n_slc = -(-(past + 1) // SLC_BLK)
    cur = past // SLC_BLK
    cos = cos_ref[...]
    sin = sin_ref[...]

    def rope(x):
        return x * cos + jnp.concatenate([x[:, C_DH // 2:], x[:, :C_DH // 2]], axis=1) * sin

    def row_dot(a_bf, b_row):
        return jnp.sum(a_bf.astype(F32) * b_row.astype(BF16).astype(F32), axis=-1, keepdims=True)

    row8 = lax.broadcasted_iota(jnp.int32, (C_HEADS, 128), 0)
    lane = lax.broadcasted_iota(jnp.int32, (C_HEADS, 128), 1)
    g0 = row8 < C_GROUP
    g0h = g0[:, 0:C_DH]

    def spread(x):
        return jnp.concatenate([jnp.where(g0h, x, 0.0), jnp.where(g0h, 0.0, x)], axis=1).astype(BF16)

    def gather(x):
        return jnp.where(g0h, x[:, 0:C_DH], x[:, C_DH:2 * C_DH])

    def column_tile(row, reps):
        return jnp.concatenate([jnp.broadcast_to(row, (128, 128)).T] * reps, axis=1)

    for i, pg in enumerate(pages):
        xk_ref[i * PAGE_SIZE:(i + 1) * PAGE_SIZE, :] = pg[0:128, :].T
        xv_ref[i * PAGE_SIZE:(i + 1) * PAGE_SIZE, :] = pg[128:256, :].T
    m_rows = ND_SB * 128

    def load_l(l):
        sl = pl.ds(l, m_rows, stride=CMP_STRIDE)
        return jnp.concatenate([xk_ref[sl, :], xv_ref[sl, :]], axis=1)

    kcvc_all = _compress(load_l, pos_ref, w1_ref, w2_ref, m_rows).astype(BF16)

    for sb in range(ND_SB):
        spages = pages[sb * N_PAGES:(sb + 1) * N_PAGES]
        kcvc = kcvc_all[sb * 128:(sb + 1) * 128]
        q8 = q_ref[sb] * ATT_SCALE
        kv = kv_ref[sb]
        ks_new = rope(kv[4:6])
        kw_new = rope(kv[8:10])
        rows_ref[sb, 0:4, :] = kv[0:4]
        rows_ref[sb, 4:6, :] = ks_new
        rows_ref[sb, 6:8, :] = kv[6:8]
        qs = spread(q8)
        qrs = spread(rope(q8))

        s = lax.dot_general(qs, kcvc[:, 0:128], NT_DIMS, preferred_element_type=F32)
        ok = (CMP_STRIDE * lane + (CMP_LEN - 1)) <= past
        s = jnp.where(ok, s, NEG_INF)
        e = jnp.where(ok, jnp.exp(s - jnp.max(s, axis=-1, keepdims=True)), 0.0)
        p = e / jnp.maximum(jnp.sum(e, axis=-1, keepdims=True), 1e-30)
        o_cmp = gather(jnp.dot(p.astype(BF16), kcvc[:, 128:256], preferred_element_type=F32))

        ps0 = jnp.sum(jnp.where(g0, p, 0.0), axis=0, keepdims=True)
        ps1 = jnp.sum(jnp.where(g0, 0.0, p), axis=0, keepdims=True)
        imp = jnp.dot(jnp.where(g0, ps0, ps1), ov_ref[...], precision=lax.Precision.HIGHEST,
                      preferred_element_type=F32)
        score = jnp.where(lane < n_slc, _forced_score(lane, cur, imp), -2.0)
        sel = _rank_select(score, lane, n_slc, 1)
        chosen = jnp.dot(sel.astype(BF16), ex_ref[...], preferred_element_type=F32)

        s_pg, ok_pg = [], []
        for i, pg in enumerate(spages):
            ok_pg.append(chosen[:, i * PAGE_SIZE:(i + 1) * PAGE_SIZE] > 0.5)
            sp = jnp.dot(qrs, pg[256:384, :].astype(BF16), preferred_element_type=F32)
            s_pg.append(jnp.where(ok_pg[i], sp, NEG_INF))
        new_ok = sel[:, cur:cur + 1] > 0.5
        s_new = jnp.where(new_ok, row_dot(qrs, jnp.concatenate([ks_new[0:1], ks_new[1:2]], axis=1)), NEG_INF)
        mx = s_new
        for sp in s_pg:
            mx = jnp.maximum(mx, jnp.max(sp, axis=-1, keepdims=True))
        p_new = jnp.where(new_ok, jnp.exp(s_new - mx), 0.0)
        den = p_new
        acc = p_new * jnp.concatenate([kv[6:7], kv[7:8]], axis=1)
        for i, pg in enumerate(spages):
            pp = jnp.where(ok_pg[i], jnp.exp(s_pg[i] - mx), 0.0)
            den = den + jnp.sum(pp, axis=-1, keepdims=True)
            acc = acc + lax.dot_general(pp.astype(BF16), pg[384:512, :].astype(BF16), NT_DIMS,
                                        preferred_element_type=F32)
        o_slc = gather(acc / den)

        wb = wb_ref[sb]
        nw = wb.shape[1]
        sw = jnp.dot(qrs, wb[0:128, :].astype(BF16), preferred_element_type=F32)
        wpos = (past - nw) + lax.broadcasted_iota(jnp.int32, (C_HEADS, nw), 1)
        w_ok = jnp.where(wpos <= past, jnp.where(wpos > past - WINDOW, 1.0, 0.0), 0.0) > 0.5
        sw = jnp.where(w_ok, sw, NEG_INF)
        kw_row = jnp.concatenate([kw_new[0:1], kw_new[1:2]], axis=1)
        vw_row = jnp.concatenate([kv[10:11], kv[11:12]], axis=1)
        sw_new = row_dot(qrs, kw_row)
        mw = jnp.maximum(jnp.max(sw, axis=-1, keepdims=True), sw_new)
        pw = jnp.where(w_ok, jnp.exp(sw - mw), 0.0)
        pw_new = jnp.exp(sw_new - mw)
        accw = lax.dot_general(pw.astype(BF16), wb[128:256, :].astype(BF16), NT_DIMS,
                               preferred_element_type=F32) + pw_new * vw_row
        o_win = gather(accw / (jnp.sum(pw, axis=-1, keepdims=True) + pw_new))

        sig = _sigmoid(cg_ref[sb])
        oc = sig[:, 0:1] * o_cmp + sig[:, 1:2] * o_slc + sig[:, 2:3] * o_win
        yc_ref[sb] = (oc * _silu(cz_ref[sb])).astype(BF16)

        new_cols = jnp.concatenate([column_tile(kw_row, nw // 128), column_tile(vw_row, nw // 128)], axis=0)
        wlane = lax.broadcasted_iota(jnp.int32, wb.shape, 1)
        wn_ref[sb] = jnp.where(wlane == nw - 1, new_cols, pltpu.roll(wb, nw - 1, 1))


def _nsa_decode(cache_t, layer, page_table, win_t, q8, kv12, cg83, cz8, cos64, sin64, pos256, w1bd, w2bd):
    nb = q8.shape[0]
    nw = win_t.shape[3]
    past = N_PAGES * PAGE_SIZE
    n_slc = -(-(past + 1) // SLC_BLK)
    ov = jnp.asarray(_overlap_t(n_slc, 128).T)
    keys = np.arange(past)
    ex = jnp.asarray((np.arange(128)[:, None] == keys[None, :] // SLC_BLK).astype(np.float32)).astype(BF16)

    def page_spec(sb, i):
        return pl.BlockSpec((None, None, 512, PAGE_SIZE), lambda b, pt: (layer, pt[b * ND_SB + sb, i], 0, 0))

    c3 = lambda n, d: pl.BlockSpec((ND_SB, n, d), lambda b, pt: (b, 0, 0))
    z2 = lambda b, pt: (0, 0)
    grid_spec = pltpu.PrefetchScalarGridSpec(
        num_scalar_prefetch=1,
        grid=(nb // ND_SB,),
        in_specs=[page_spec(sb, i) for sb in range(ND_SB) for i in range(N_PAGES)] + [
            c3(C_HEADS, C_DH), c3(12, C_DH), c3(C_HEADS, 3), c3(C_HEADS, C_DH),
            pl.BlockSpec((None, ND_SB, 256, nw), lambda b, pt: (layer, b, 0, 0)),
            pl.BlockSpec((1, C_DH), z2), pl.BlockSpec((1, C_DH), z2),
            pl.BlockSpec((CMP_LEN, 256), z2),
            pl.BlockSpec((CMP_LEN, 256, 256), lambda b, pt: (0, 0, 0)),
            pl.BlockSpec((256, 256), z2),
            pl.BlockSpec((128, 128), z2),
            pl.BlockSpec((128, past), z2)],
        out_specs=[c3(C_HEADS, C_DH), c3(C_HEADS, C_DH), pl.BlockSpec((ND_SB, 256, nw), lambda b, pt: (b, 0, 0))],
        scratch_shapes=[pltpu.VMEM((ND_SB * past, 128), F32), pltpu.VMEM((ND_SB * past, 128), F32)],
    )
    return pl.pallas_call(
        _nsa_decode_kernel,
        grid_spec=grid_spec,
        out_shape=[jax.ShapeDtypeStruct((nb, C_HEADS, C_DH), BF16),
                   jax.ShapeDtypeStruct((nb, C_HEADS, C_DH), F32),
                   jax.ShapeDtypeStruct((nb, 256, nw), F32)],
        compiler_params=_cparams("arbitrary"),
        name="nsa_decode",
    )(page_table, *([cache_t] * (ND_SB * N_PAGES)), q8, kv12, cg83, cz8, win_t, cos64, sin64, pos256, w1bd, w2bd,
      ov, ex)


def _merge_kernel(ya_ref, yb_ref, yc_ref, mg_ref, x_ref, gate_ref, wb_ref, wo_ref, lg_ref, lb_ref, o_ref):
    merged = None
    for n, y_ref in enumerate((ya_ref, yb_ref, yc_ref)):
        br = jnp.dot(y_ref[0], wb_ref[n], preferred_element_type=F32)
        term = _sigmoid(mg_ref[0, :, n * D_MODEL:(n + 1) * D_MODEL]) * br
        merged = term if merged is None else merged + term
    out = jnp.dot(merged.astype(BF16), wo_ref[...], preferred_element_type=F32)
    z = ALPHA * x_ref[0] + gate_ref[0] * out
    mu = jnp.mean(z, axis=-1, keepdims=True)
    zc = z - mu
    var = jnp.mean(zc * zc, axis=-1, keepdims=True)
    o_ref[0] = zc * lax.rsqrt(var + LN_EPS) * lg_ref[...] + lb_ref[...]


def _merge(ya, yb, yc, proj, x, gate, wb_bf, wo_bf, ln_g, ln_b, tm):
    B, T, _ = x.shape
    ys = pl.BlockSpec((1, tm, 512), lambda b, t: (b, t, 0))
    gs = (pl.BlockSpec((1, 1, D_MODEL), lambda b, t: (b, 0, 0)) if gate.shape[1] == 1
          else pl.BlockSpec((1, tm, D_MODEL), lambda b, t: (b, t, 0)))
    return pl.pallas_call(
        _merge_kernel,
        grid=(B, T // tm),
        in_specs=[ys, ys, ys,
                  pl.BlockSpec((1, tm, 3 * D_MODEL), lambda b, t: (b, t, COL_MG // (3 * D_MODEL))),
                  pl.BlockSpec((1, tm, D_MODEL), lambda b, t: (b, t, 0)),
                  gs,
                  pl.BlockSpec((3, 512, D_MODEL), lambda b, t: (0, 0, 0)),
                  pl.BlockSpec((D_MODEL, D_MODEL), lambda b, t: (0, 0)),
                  pl.BlockSpec((1, D_MODEL), lambda b, t: (0, 0)),
                  pl.BlockSpec((1, D_MODEL), lambda b, t: (0, 0))],
        out_specs=pl.BlockSpec((1, tm, D_MODEL), lambda b, t: (b, t, 0)),
        out_shape=jax.ShapeDtypeStruct((B, T, D_MODEL), F32),
        compiler_params=_cparams("arbitrary", "arbitrary"),
        name="merge",
    )(ya, yb, yc, proj, x, gate, wb_bf, wo_bf, ln_g, ln_b)


def _block_diag(blocks):
    n = len(blocks)
    z = jnp.zeros_like(blocks[0])
    return jnp.concatenate(
        [jnp.concatenate([b if j == i else z for j in range(n)], axis=-1) for i, b in enumerate(blocks)], axis=-2)


def _rope_tables(pos, width):
    inv = ROPE_THETA ** (-jnp.arange(0, C_DH, 2, dtype=F32) / C_DH)
    ang = pos.astype(F32)[:, None] * inv[None, :]
    cos = jnp.cos(ang)
    sin = jnp.sin(ang)
    reps = width // C_DH
    return (jnp.tile(jnp.concatenate([cos, cos], axis=1), (1, reps)),
            jnp.tile(jnp.concatenate([-sin, sin], axis=1), (1, reps)))


def _token_major(x_t, lead):
    kinds = x_t.shape[-2] // (C_KV_HEADS * C_DH)
    x = x_t.reshape(*x_t.shape[:-2], kinds, C_KV_HEADS, C_DH, x_t.shape[-1])
    return jnp.moveaxis(x, -1, lead)


def kernel(x_prompt, x_sample, c_prompt, c_sample, cache_nsa_kv, page_table, state_win_kv, state_hgrn,
           state_rglru, state_conv, ada_w, ada_b, w_in, a_lb, a_norm_g, b_conv_w, b_conv_b, b_gate_a_w,
           b_gate_a_b, b_gate_x_w, b_gate_x_b, b_lambda, c_cmp_pos, c_cmp_w1, c_cmp_w2, w_branch, w_out,
           ln_g, ln_b):
    bp, seq, _ = x_prompt.shape
    bs = x_sample.shape[0]
    n_pages = page_table.shape[1]
    n_pool = cache_nsa_kv.shape[1]
    nw = state_win_kv.shape[2]
    past_len = n_pages * cache_nsa_kv.shape[2]
    assert n_pages == N_PAGES and cache_nsa_kv.shape[2] == PAGE_SIZE and x_sample.shape[1] == 1

    lb_all = jnp.cumsum(jax.nn.softmax(a_lb.astype(F32), axis=0), axis=0)
    lb_all = lb_all - lb_all[0:1]
    loglb = jnp.log(lb_all)
    l1mlb = jnp.log1p(-lb_all)
    w_in_p = jnp.concatenate(
        [w_in[:, :, 4888:7960], w_in[:, :, 0:3584], w_in[:, :, 4376:4888], w_in[:, :, 3584:4352],
         w_in[:, :, 4352:4376], jnp.zeros((DEPTH, D_MODEL, N_PROJ - 7960), w_in.dtype)], axis=2).astype(BF16)
    wb_bf = w_branch.astype(BF16)
    wo_bf = w_out.astype(BF16)
    cache_t = cache_nsa_kv.transpose(0, 1, 3, 4, 5, 2).reshape(DEPTH, n_pool, 512, PAGE_SIZE)
    win_t = state_win_kv.transpose(0, 1, 3, 4, 5, 2).reshape(DEPTH, bs, 256, nw)
    cos_p, sin_p = _rope_tables(jnp.arange(seq), 128)
    cos_s, sin_s = _rope_tables(past_len + jnp.arange(1), C_DH)

    ada = _ada_all(jnp.concatenate([c_prompt, c_sample], axis=0), ada_w, ada_b)

    xp, xs = x_prompt, x_sample.reshape(1, bs, D_MODEL)
    outs = {k: [] for k in ("nsa_p", "nsa_s", "win_p", "win_s", "hg_p", "hg_s", "rg_p", "rg_s", "cv_p", "cv_s")}
    for l in range(DEPTH):
        shift, scale, gate = ada[l, :, 0:D_MODEL], ada[l, :, D_MODEL:2 * D_MODEL], ada[l, :, 2 * D_MODEL:]
        rg_w = (b_conv_w[l], b_conv_b[l].reshape(1, -1),
                _block_diag([b_gate_a_w[l, i] for i in range(B_BLOCKS)]).astype(BF16), b_gate_a_b[l].reshape(1, -1),
                _block_diag([b_gate_x_w[l, i] for i in range(B_BLOCKS)]).astype(BF16), b_gate_x_b[l].reshape(1, -1),
                b_lambda[l].reshape(1, -1))
        pos256 = jnp.concatenate([c_cmp_pos[l, 0], c_cmp_pos[l, 0], c_cmp_pos[l, 1], c_cmp_pos[l, 1]], axis=1)
        w1r = c_cmp_w1[l].reshape(2, CMP_LEN, C_DH, C_DH)
        w1bd = _block_diag([w1r[0], w1r[0], w1r[1], w1r[1]]).astype(BF16)
        w2bd = _block_diag([c_cmp_w2[l, 0], c_cmp_w2[l, 0], c_cmp_w2[l, 1], c_cmp_w2[l, 1]]).astype(BF16)
        gain = a_norm_g[l].reshape(1, -1)
        lg, lbias = ln_g[l].reshape(1, -1), ln_b[l].reshape(1, -1)

        proj = _inproj(xp, scale[:bp, None], shift[:bp, None], w_in_p[l], 1024)
        ya, s_new = _hgrn_prompt(proj, loglb[l].reshape(1, -1), l1mlb[l].reshape(1, -1), gain)
        yb, h_new, cv_new = _rglru_prompt(proj, rg_w)
        rows_t, wrow_t, raw, q_hm, qr_hm, ks_hm, kw_hm = _nsa_prep(proj, cos_p, sin_p)
        kc, vct = _compress_prompt(raw, pos256, w1bd, w2bd)
        yc = _nsa_prompt(proj, q_hm, qr_hm, kc, vct, ks_hm, rows_t, kw_hm, wrow_t)
        xp = _merge(ya, yb, yc, proj, xp, gate[:bp, None], wb_bf[l], wo_bf[l], lg, lbias, 256)
        outs["nsa_p"].append(rows_t)
        outs["win_p"].append(wrow_t[:, :, seq - min(WINDOW, seq):])
        outs["hg_p"].append(s_new)
        outs["rg_p"].append(h_new.reshape(bp, B_WIDTH))
        outs["cv_p"].append(cv_new)

        proj_s = _inproj(xs, scale[None, bp:], shift[None, bp:], w_in_p[l], bs).reshape(bs, N_PROJ)
        s_s, ya_s = _hgrn_decode(state_hgrn, l, proj_s, loglb[l].reshape(A_HEADS, A_DK, 1),
                                 l1mlb[l].reshape(A_HEADS, A_DK, 1), gain)
        yb_s, h_s, cv_s = _rglru_decode(proj_s, state_conv[l].reshape(bs, -1), state_rglru[l], rg_w)
        q8 = proj_s[:, COL_CQ:COL_CQ + 512].reshape(bs, C_HEADS, C_DH)
        cz8 = proj_s[:, COL_CZ:COL_CZ + 512].reshape(bs, C_HEADS, C_DH)
        kv12 = proj_s[:, COL_CKV:COL_CKV + 768].reshape(bs, 12, C_DH)
        cg83 = proj_s[:, COL_CG:COL_CG + 3 * C_HEADS].reshape(bs, 3, C_HEADS).transpose(0, 2, 1)
        yc_s, rows_s, wnew_t = _nsa_decode(cache_t, l, page_table, win_t, q8, kv12, cg83, cz8, cos_s, sin_s,
                                           pos256, w1bd, w2bd)
        xs = _merge(ya_s.reshape(1, bs, 512), yb_s.reshape(1, bs, 512), yc_s.reshape(1, bs, 512),
                    proj_s.reshape(1, bs, N_PROJ), xs, gate[None, bp:], wb_bf[l], wo_bf[l], lg, lbias, bs)
        outs["nsa_s"].append(rows_s.reshape(bs, 1, 4, C_KV_HEADS, C_DH))
        outs["win_s"].append(wnew_t)
        outs["hg_s"].append(s_s)
        outs["rg_s"].append(h_s)
        outs["cv_s"].append(cv_s.reshape(bs, CONV_W - 1, B_WIDTH))

    st = {k: jnp.stack(v, axis=0) for k, v in outs.items()}
    return (xp, xs.reshape(bs, 1, D_MODEL), _token_major(st["nsa_p"], 2), st["nsa_s"],
            _token_major(st["win_p"], 2), _token_major(st["win_s"], 2),
            st["hg_p"], st["hg_s"], st["rg_p"], st["rg_s"], st["cv_p"], st["cv_s"])
```

```python
import numpy as np
import jax
import jax.numpy as jnp
from jax import lax
from jax.experimental import pallas as pl
from jax.experimental.pallas import tpu as pltpu

F32 = jnp.float32
BF16 = jnp.bfloat16

D_MODEL = 1024
DEPTH = 4
PAGE_SIZE = 128
A_HEADS = 4
A_DK = 128
A_CHUNK = 64
B_WIDTH = 512
B_BLOCKS = 8
CONV_W = 4
LRU_C = 8.0
C_HEADS = 8
C_KV_HEADS = 2
C_GROUP = 4
C_DH = 64
CMP_LEN = 32
CMP_STRIDE = 16
SLC_BLK = 64
SLC_TOPN = 16
WINDOW = 512
ROPE_THETA = 10000.0
ATT_SCALE = C_DH ** -0.5
ALPHA = (2 * DEPTH) ** 0.25
LN_EPS = 1e-5
NEG_INF = -1e30
FORCED_SCORE = 1e9

COL_MG, COL_AQ, COL_AF, COL_AI, COL_AZ = 0, 3072, 3584, 4096, 4608
COL_BX, COL_BZ, COL_CQ, COL_CZ, COL_CKV, COL_CG = 5120, 5632, 6144, 6656, 7168, 7936
N_PROJ = 8064
PROJ_TN = 2688

VMEM_LIMIT = 56 * 1024 * 1024

NT_DIMS = (((1,), (1,)), ((), ()))
TN_DIMS = (((0,), (0,)), ((), ()))


def _cparams(*sem):
    return pltpu.CompilerParams(dimension_semantics=sem, vmem_limit_bytes=VMEM_LIMIT)


def _sigmoid(x):
    return jax.nn.sigmoid(x)


def _silu(x):
    return x * jax.nn.sigmoid(x)


def _log_f(af, loglb, l1mlb):
    ls = jnp.minimum(af, 0.0) - jnp.log1p(jnp.exp(-jnp.abs(af)))
    y = l1mlb + ls
    return jnp.maximum(loglb, y) + jnp.log1p(jnp.exp(-jnp.abs(loglb - y)))


def _ada_kernel(c_ref, w_ref, b_ref, o_ref):
    o_ref[0] = jnp.dot(c_ref[...], w_ref[0].astype(BF16), preferred_element_type=F32) + b_ref[0]


def _ada_all(c_all, ada_w, ada_b):
    nb = c_all.shape[0]
    return pl.pallas_call(
        _ada_kernel,
        grid=(DEPTH, 3),
        in_specs=[
            pl.BlockSpec((nb, D_MODEL), lambda l, j: (0, 0)),
            pl.BlockSpec((1, D_MODEL, D_MODEL), lambda l, j: (l, 0, j)),
            pl.BlockSpec((1, 1, D_MODEL), lambda l, j: (l, 0, j)),
        ],
        out_specs=pl.BlockSpec((1, nb, D_MODEL), lambda l, j: (l, 0, j)),
        out_shape=jax.ShapeDtypeStruct((DEPTH, nb, 3 * D_MODEL), F32),
        compiler_params=_cparams("arbitrary", "arbitrary"),
        name="ada",
    )(c_all.astype(BF16), ada_w, ada_b.reshape(DEPTH, 1, 3 * D_MODEL))


def _inproj_kernel(x_ref, sc_ref, sh_ref, w_ref, o_ref, h_ref):
    @pl.when(pl.program_id(2) == 0)
    def _():
        h_ref[...] = (x_ref[0] * (1.0 + sc_ref[0]) + sh_ref[0]).astype(BF16)

    o_ref[0] = jnp.dot(h_ref[...], w_ref[...], preferred_element_type=F32)


def _inproj(x, scale, shift, w_bf, tt):
    B, T, _ = x.shape
    ts = scale.shape[1]
    mod_spec = (pl.BlockSpec((1, 1, D_MODEL), lambda b, t, n: (b, 0, 0)) if ts == 1
                else pl.BlockSpec((1, tt, D_MODEL), lambda b, t, n: (b, t, 0)))
    return pl.pallas_call(
        _inproj_kernel,
        grid=(B, T // tt, N_PROJ // PROJ_TN),
        in_specs=[
            pl.BlockSpec((1, tt, D_MODEL), lambda b, t, n: (b, t, 0)),
            mod_spec, mod_spec,
            pl.BlockSpec((D_MODEL, PROJ_TN), lambda b, t, n: (0, n)),
        ],
        out_specs=pl.BlockSpec((1, tt, PROJ_TN), lambda b, t, n: (b, t, n)),
        out_shape=jax.ShapeDtypeStruct((B, T, N_PROJ), F32),
        scratch_shapes=[pltpu.VMEM((tt, D_MODEL), BF16)],
        compiler_params=_cparams("arbitrary", "arbitrary", "arbitrary"),
        name="inproj",
    )(x, scale, shift, w_bf)


HG_TT = 256
HG_LEVELS = (1, 2, 4, 8, 16, 32)


def _hgrn_level_table():
    t = np.arange(HG_TT)[:, None]
    s = np.arange(HG_TT)[None, :]
    lvl = np.full((HG_TT, HG_TT), -1, np.int32)
    same = (t // A_CHUNK) == (s // A_CHUNK)
    for i, h in enumerate(HG_LEVELS):
        m = same & ((t // h) == (s // h) + 1) & (((s // h) % 2) == 0)
        lvl[m] = i
    lvl[t == s] = len(HG_LEVELS)
    return lvl


def _hgrn_prompt_kernel(aq_ref, af_ref, ai_ref, az_ref, llb_ref, l1_ref, g_ref, lvl_ref,
                        ya_ref, so_ref, s_ref):
    t = pl.program_id(2)

    @pl.when(t == 0)
    def _():
        s_ref[...] = jnp.zeros_like(s_ref)

    aq = aq_ref[0]
    v = ai_ref[0]
    az = az_ref[0]
    q = _silu(aq)
    lf = _log_f(af_ref[0], llb_ref[...], l1_ref[...])
    k = 1.0 - jnp.exp(lf)
    vb = v.astype(BF16)
    row = lax.broadcasted_iota(jnp.int32, (HG_TT, A_DK), 0)
    lvl = lvl_ref[...]

    cin = lf
    rex = jnp.zeros_like(lf)
    att = jnp.where(lvl == len(HG_LEVELS), jnp.sum(q * k, axis=-1, keepdims=True), 0.0)
    for i, h in enumerate(HG_LEVELS):
        qh = (q * jnp.exp(cin)).astype(BF16)
        kh = (k * jnp.exp(rex)).astype(BF16)
        a = lax.dot_general(qh, kh, NT_DIMS, preferred_element_type=F32)
        att = jnp.where(lvl == i, a, att)
        tot = cin + rex
        upper = ((row // h) % 2) == 1
        cin = cin + jnp.where(upper, pltpu.roll(tot, h, 0), 0.0)
        rex = rex + jnp.where(upper, 0.0, pltpu.roll(tot, HG_TT - h, 0))

    o = jnp.dot(att.astype(BF16), vb, preferred_element_type=F32)
    qb = (q * jnp.exp(cin)).astype(BF16)
    kd = (k * jnp.exp(rex)).astype(BF16)
    tot = cin + rex
    s = s_ref[...]
    inter = []
    for c in range(HG_TT // A_CHUNK):
        sl = slice(c * A_CHUNK, (c + 1) * A_CHUNK)
        inter.append(jnp.dot(qb[sl], s.astype(BF16), preferred_element_type=F32))
        u = lax.dot_general(kd[sl], vb[sl], TN_DIMS, preferred_element_type=F32)
        e_rows = jnp.broadcast_to(jnp.exp(tot[c * A_CHUNK:c * A_CHUNK + 1]), (A_DK, A_DK))
        s = e_rows.T * s + u
    s_ref[...] = s
    o = o + jnp.concatenate(inter, axis=0)

    ms = jnp.mean(o * o, axis=-1, keepdims=True)
    ya_ref[0] = ((o * lax.rsqrt(ms + LN_EPS)) * g_ref[...] * _silu(az)).astype(BF16)

    @pl.when(t == pl.num_programs(2) - 1)
    def _():
        so_ref[0, 0] = s


def _hgrn_prompt(proj, loglb, l1mlb, gain):
    B, T, _ = proj.shape
    lvl = jnp.asarray(_hgrn_level_table())

    def col(base):
        return pl.BlockSpec((1, HG_TT, A_DK), lambda b, h, t: (b, t, base // A_DK + h))

    vec = pl.BlockSpec((1, A_DK), lambda b, h, t: (0, h))
    return pl.pallas_call(
        _hgrn_prompt_kernel,
        grid=(B, A_HEADS, T // HG_TT),
        in_specs=[col(COL_AQ), col(COL_AF), col(COL_AI), col(COL_AZ), vec, vec, vec,
                  pl.BlockSpec((HG_TT, HG_TT), lambda b, h, t: (0, 0))],
        out_specs=[pl.BlockSpec((1, HG_TT, A_DK), lambda b, h, t: (b, t, h)),
                   pl.BlockSpec((1, 1, A_DK, A_DK), lambda b, h, t: (b, h, 0, 0))],
        out_shape=[jax.ShapeDtypeStruct((B, T, A_HEADS * A_DK), BF16),
                   jax.ShapeDtypeStruct((B, A_HEADS, A_DK, A_DK), F32)],
        scratch_shapes=[pltpu.VMEM((A_DK, A_DK), F32)],
        compiler_params=_cparams("arbitrary", "arbitrary", "arbitrary"),
        name="hgrn_prompt",
    )(proj, proj, proj, proj, loglb, l1mlb, gain, lvl)


HG_SB = 16


def _hgrn_decode_kernel(s_ref, aqc_ref, afc_ref, ai_ref, az_ref, llb_ref, l1_ref, g_ref, so_ref, ya_ref):
    aqc = aqc_ref[0, 0]
    qc = _silu(aqc)
    lf = _log_f(afc_ref[0, 0], llb_ref[0], l1_ref[0])
    ec = jnp.exp(lf)
    kc = 1.0 - ec
    v = ai_ref[...]
    rows = []
    for r in range(HG_SB):
        sn = ec[:, r:r + 1] * s_ref[r, 0] + kc[:, r:r + 1] * v[r:r + 1, :]
        so_ref[r, 0] = sn
        rows.append(jnp.sum(qc[:, r:r + 1] * sn, axis=0, keepdims=True))
    o = jnp.concatenate(rows, axis=0)
    ms = jnp.mean(o * o, axis=-1, keepdims=True)
    ya_ref[...] = ((o * lax.rsqrt(ms + LN_EPS)) * g_ref[...] * _silu(az_ref[...])).astype(BF16)


def _hgrn_decode(state_all, layer, proj_s, loglb_c, l1mlb_c, gain):
    nb = proj_s.shape[0]
    nblk = nb // HG_SB

    def cols(base):
        a = proj_s[:, base:base + A_HEADS * A_DK].reshape(nblk, HG_SB, A_HEADS, A_DK)
        return a.transpose(2, 0, 3, 1)

    st_spec = pl.BlockSpec((None, HG_SB, 1, A_DK, A_DK), lambda h, i: (layer, i, h, 0, 0))
    col_spec = pl.BlockSpec((1, 1, A_DK, HG_SB), lambda h, i: (h, i, 0, 0))
    cvec = pl.BlockSpec((1, A_DK, 1), lambda h, i: (h, 0, 0))
    return pl.pallas_call(
        _hgrn_decode_kernel,
        grid=(A_HEADS, nblk),
        in_specs=[st_spec, col_spec, col_spec,
                  pl.BlockSpec((HG_SB, A_DK), lambda h, i: (i, COL_AI // A_DK + h)),
                  pl.BlockSpec((HG_SB, A_DK), lambda h, i: (i, COL_AZ // A_DK + h)),
                  cvec, cvec,
                  pl.BlockSpec((1, A_DK), lambda h, i: (0, h))],
        out_specs=[pl.BlockSpec((HG_SB, 1, A_DK, A_DK), lambda h, i: (i, h, 0, 0)),
                   pl.BlockSpec((HG_SB, A_DK), lambda h, i: (i, h))],
        out_shape=[jax.ShapeDtypeStruct((nb, A_HEADS, A_DK, A_DK), F32),
                   jax.ShapeDtypeStruct((nb, A_HEADS * A_DK), BF16)],
        compiler_params=_cparams("arbitrary", "arbitrary"),
        name="hgrn_decode",
    )(state_all, cols(COL_AQ), cols(COL_AF), proj_s, proj_s, loglb_c, l1mlb_c, gain)


RG_TT = 256


def _rglru_gates(u, ga_ref, gab_ref, gx_ref, gxb_ref, lam_ref):
    ub = u.astype(BF16)
    r = _sigmoid(jnp.dot(ub, ga_ref[...], preferred_element_type=F32) + gab_ref[...])
    ig = _sigmoid(jnp.dot(ub, gx_ref[...], preferred_element_type=F32) + gxb_ref[...])
    nl = -lam_ref[...]
    softplus = jnp.maximum(nl, 0.0) + jnp.log1p(jnp.exp(-jnp.abs(nl)))
    a = jnp.exp(-LRU_C * r * softplus)
    return a, jnp.sqrt(1.0 - a * a) * (ig * u)


def _rglru_prompt_kernel(bx_ref, bz_ref, cw_ref, cb_ref, ga_ref, gab_ref, gx_ref, gxb_ref, lam_ref,
                         yb_ref, h_ref, cv_ref, ext_ref, hc_ref):
    t = pl.program_id(1)

    @pl.when(t == 0)
    def _():
        ext_ref[0:8, :] = jnp.zeros((8, B_WIDTH), F32)
        hc_ref[...] = jnp.zeros_like(hc_ref)

    x = bx_ref[0]
    ext_ref[8:8 + RG_TT, :] = x
    u = cb_ref[...] + (ext_ref[5:5 + RG_TT, :] * cw_ref[0:1, :] + ext_ref[6:6 + RG_TT, :] * cw_ref[1:2, :]
                       + ext_ref[7:7 + RG_TT, :] * cw_ref[2:3, :] + x * cw_ref[3:4, :])
    ext_ref[0:8, :] = x[RG_TT - 8:RG_TT, :]
    a, bt = _rglru_gates(u, ga_ref, gab_ref, gx_ref, gxb_ref, lam_ref)
    row = lax.broadcasted_iota(jnp.int32, (RG_TT, 128), 0)
    z = bz_ref[0]
    for c in range(B_WIDTH // 128):
        cs = slice(c * 128, (c + 1) * 128)
        ac = a[:, cs]
        bc = bt[:, cs] + jnp.where(row == 0, ac * hc_ref[0:1, cs], 0.0)
        s = 1
        while s < RG_TT:
            keep = row >= s
            bc = jnp.where(keep, ac * pltpu.roll(bc, s, 0) + bc, bc)
            ac = jnp.where(keep, ac * pltpu.roll(ac, s, 0), ac)
            s *= 2
        hc_ref[0:1, cs] = bc[RG_TT - 1:RG_TT, :]
        yb_ref[0, :, cs] = (bc * _silu(z[:, cs])).astype(BF16)

    @pl.when(t == pl.num_programs(1) - 1)
    def _():
        h_ref[0] = hc_ref[0:1, :]
        cv_ref[0] = x[RG_TT - (CONV_W - 1):RG_TT, :]


def _rglru_weight_specs(idx):
    return [pl.BlockSpec((CONV_W, B_WIDTH), idx), pl.BlockSpec((1, B_WIDTH), idx),
            pl.BlockSpec((B_WIDTH, B_WIDTH), idx), pl.BlockSpec((1, B_WIDTH), idx),
            pl.BlockSpec((B_WIDTH, B_WIDTH), idx), pl.BlockSpec((1, B_WIDTH), idx),
            pl.BlockSpec((1, B_WIDTH), idx)]


def _rglru_prompt(proj, w):
    B, T, _ = proj.shape
    return pl.pallas_call(
        _rglru_prompt_kernel,
        grid=(B, T // RG_TT),
        in_specs=[pl.BlockSpec((1, RG_TT, B_WIDTH), lambda b, t: (b, t, COL_BX // B_WIDTH)),
                  pl.BlockSpec((1, RG_TT, B_WIDTH), lambda b, t: (b, t, COL_BZ // B_WIDTH))]
        + _rglru_weight_specs(lambda b, t: (0, 0)),
        out_specs=[pl.BlockSpec((1, RG_TT, B_WIDTH), lambda b, t: (b, t, 0)),
                   pl.BlockSpec((1, 1, B_WIDTH), lambda b, t: (b, 0, 0)),
                   pl.BlockSpec((1, CONV_W - 1, B_WIDTH), lambda b, t: (b, 0, 0))],
        out_shape=[jax.ShapeDtypeStruct((B, T, B_WIDTH), BF16),
                   jax.ShapeDtypeStruct((B, 1, B_WIDTH), F32),
                   jax.ShapeDtypeStruct((B, CONV_W - 1, B_WIDTH), F32)],
        scratch_shapes=[pltpu.VMEM((8 + RG_TT, B_WIDTH), F32), pltpu.VMEM((8, B_WIDTH), F32)],
        compiler_params=_cparams("arbitrary", "arbitrary"),
        name="rglru_prompt",
    )(proj, proj, *w)


def _rglru_decode_kernel(bx_ref, bz_ref, cv0_ref, h0_ref, cw_ref, cb_ref, ga_ref, gab_ref, gx_ref, gxb_ref,
                         lam_ref, yb_ref, h_ref, cv_ref):
    x = bx_ref[...]
    c0 = cv0_ref[:, 0:B_WIDTH]
    c1 = cv0_ref[:, B_WIDTH:2 * B_WIDTH]
    c2 = cv0_ref[:, 2 * B_WIDTH:3 * B_WIDTH]
    u = cb_ref[...] + (c0 * cw_ref[0:1, :] + c1 * cw_ref[1:2, :] + c2 * cw_ref[2:3, :] + x * cw_ref[3:4, :])
    a, bt = _rglru_gates(u, ga_ref, gab_ref, gx_ref, gxb_ref, lam_ref)
    h = a * h0_ref[...] + bt
    h_ref[...] = h
    yb_ref[...] = (h * _silu(bz_ref[...])).astype(BF16)
    cv_ref[:, 0:B_WIDTH] = c1
    cv_ref[:, B_WIDTH:2 * B_WIDTH] = c2
    cv_ref[:, 2 * B_WIDTH:3 * B_WIDTH] = x


def _rglru_decode(proj_s, conv0, h0, w):
    nb = proj_s.shape[0]
    z2 = lambda i: (0, 0)
    return pl.pallas_call(
        _rglru_decode_kernel,
        grid=(1,),
        in_specs=[pl.BlockSpec((nb, B_WIDTH), lambda i: (0, COL_BX // B_WIDTH)),
                  pl.BlockSpec((nb, B_WIDTH), lambda i: (0, COL_BZ // B_WIDTH)),
                  pl.BlockSpec((nb, 3 * B_WIDTH), z2), pl.BlockSpec((nb, B_WIDTH), z2)]
        + _rglru_weight_specs(z2),
        out_specs=[pl.BlockSpec((nb, B_WIDTH), z2), pl.BlockSpec((nb, B_WIDTH), z2),
                   pl.BlockSpec((nb, 3 * B_WIDTH), z2)],
        out_shape=[jax.ShapeDtypeStruct((nb, B_WIDTH), BF16), jax.ShapeDtypeStruct((nb, B_WIDTH), F32),
                   jax.ShapeDtypeStruct((nb, 3 * B_WIDTH), F32)],
        compiler_params=_cparams("arbitrary"),
        name="rglru_decode",
    )(proj_s, proj_s, conv0, h0, *w)


def _gelu_tanh(x):
    return 0.5 * x * (1.0 + jnp.tanh(0.7978845608028654 * (x + 0.044715 * (x * x * x))))


def _compress(load_l, pos_ref, w1_ref, w2_ref, m):
    acc_a = jnp.zeros((m, 256), F32)
    acc_b = jnp.zeros((m, 256), F32)
    for l in range(CMP_STRIDE):
        xl = load_l(l)
        acc_a = acc_a + jnp.dot((xl + pos_ref[l:l + 1, :]).astype(BF16), w1_ref[l], preferred_element_type=F32)
        acc_b = acc_b + jnp.dot((xl + pos_ref[CMP_STRIDE + l:CMP_STRIDE + l + 1, :]).astype(BF16),
                                w1_ref[CMP_STRIDE + l], preferred_element_type=F32)
    hid = _gelu_tanh(acc_a + pltpu.roll(acc_b, m - 1, 0))
    return jnp.dot(hid.astype(BF16), w2_ref[...], preferred_element_type=F32)


def _rank_select(score, idx, n_rows, axis):
    rank = jnp.zeros_like(score)
    for jp in range(n_rows):
        other = score[jp:jp + 1, :] if axis == 0 else score[:, jp:jp + 1]
        rank = rank + jnp.where(other > score, 1.0, jnp.where(other == score, jnp.where(idx > jp, 1.0, 0.0), 0.0))
    return jnp.where(rank < float(SLC_TOPN), 1.0, 0.0)


def _forced_score(idx, cur, imp):
    return jnp.where(idx == 0, FORCED_SCORE,
                     jnp.where(idx == cur, FORCED_SCORE,
                               jnp.where(idx == cur - 1, FORCED_SCORE, jnp.where(idx <= cur, imp, -1.0))))


NP_TT = 256


def _nsa_prep_kernel(cq_ref, kv01_ref, kv23_ref, kv45_ref, cos_ref, sin_ref,
                     rows_t_ref, win_t_ref, raw_ref, q_ref, qr_ref, ks_ref, kw_ref):
    cos = cos_ref[...]
    sin = sin_ref[...]
    lane = lax.broadcasted_iota(jnp.int32, cos.shape, 1)
    first = (lane % C_DH) < (C_DH // 2)

    def rope(x):
        return x * cos + jnp.where(first, pltpu.roll(x, 128 - C_DH // 2, 1), pltpu.roll(x, C_DH // 2, 1)) * sin

    def heads(ref, x, h0):
        ref[0, h0] = x[:, 0:C_DH].astype(BF16)
        ref[0, h0 + 1] = x[:, C_DH:2 * C_DH].astype(BF16)

    cq = cq_ref[0] * ATT_SCALE
    for c in range(C_HEADS // 2):
        x = cq[:, c * 128:(c + 1) * 128]
        heads(q_ref, x, 2 * c)
        heads(qr_ref, rope(x), 2 * c)
    kv01 = kv01_ref[0]
    kv23 = kv23_ref[0]
    kv45 = kv45_ref[0]
    ks = rope(kv23[:, 0:128])
    kw = rope(kv45[:, 0:128])
    raw_ref[0] = kv01
    heads(ks_ref, ks, 0)
    heads(kw_ref, kw, 0)
    rows_t_ref[0, 0:128, :] = kv01[:, 0:128].T
    rows_t_ref[0, 128:256, :] = kv01[:, 128:256].T
    rows_t_ref[0, 256:384, :] = ks.T
    rows_t_ref[0, 384:512, :] = kv23[:, 128:256].T
    win_t_ref[0, 0:128, :] = kw.T
    win_t_ref[0, 128:256, :] = kv45[:, 128:256].T


def _nsa_prep(proj, cos128, sin128):
    B, T, _ = proj.shape
    hm = lambda n: pl.BlockSpec((1, n, NP_TT, C_DH), lambda b, t: (b, 0, t, 0))
    hs = lambda n: jax.ShapeDtypeStruct((B, n, T, C_DH), BF16)
    kvb = COL_CKV // 256
    return pl.pallas_call(
        _nsa_prep_kernel,
        grid=(B, T // NP_TT),
        in_specs=[pl.BlockSpec((1, NP_TT, 512), lambda b, t: (b, t, COL_CQ // 512)),
                  pl.BlockSpec((1, NP_TT, 256), lambda b, t: (b, t, kvb)),
                  pl.BlockSpec((1, NP_TT, 256), lambda b, t: (b, t, kvb + 1)),
                  pl.BlockSpec((1, NP_TT, 256), lambda b, t: (b, t, kvb + 2)),
                  pl.BlockSpec((NP_TT, 128), lambda b, t: (t, 0)),
                  pl.BlockSpec((NP_TT, 128), lambda b, t: (t, 0))],
        out_specs=[pl.BlockSpec((1, 512, NP_TT), lambda b, t: (b, 0, t)),
                   pl.BlockSpec((1, 256, NP_TT), lambda b, t: (b, 0, t)),
                   pl.BlockSpec((1, NP_TT, 256), lambda b, t: (b, t, 0)),
                   hm(8), hm(8), hm(2), hm(2)],
        out_shape=[jax.ShapeDtypeStruct((B, 512, T), F32), jax.ShapeDtypeStruct((B, 256, T), F32),
                   jax.ShapeDtypeStruct((B, T, 256), F32), hs(8), hs(8), hs(2), hs(2)],
        compiler_params=_cparams("arbitrary", "arbitrary"),
        name="nsa_prep",
    )(proj, proj, proj, proj, cos128, sin128)


def _compress_prompt_kernel(xk_ref, xv_ref, pos_ref, w1_ref, w2_ref, kc_ref, vct_ref):
    def load_l(l):
        return jnp.concatenate([xk_ref[pl.ds(l, 128, stride=CMP_STRIDE), :],
                                xv_ref[pl.ds(l, 128, stride=CMP_STRIDE), :]], axis=1)

    out = _compress(load_l, pos_ref, w1_ref, w2_ref, 128)
    vt = out[:, 128:256].T
    for g in range(C_KV_HEADS):
        kc_ref[0, g] = out[:, g * C_DH:(g + 1) * C_DH].astype(BF16)
        vct_ref[0, g] = vt[g * C_DH:(g + 1) * C_DH, :].astype(BF16)


def _compress_prompt(raw, pos256, w1bd, w2bd):
    B, T, _ = raw.shape
    return pl.pallas_call(
        _compress_prompt_kernel,
        grid=(B,),
        in_specs=[pl.BlockSpec((None, T, 128), lambda b: (b, 0, 0)),
                  pl.BlockSpec((None, T, 128), lambda b: (b, 0, 1)),
                  pl.BlockSpec((CMP_LEN, 256), lambda b: (0, 0)),
                  pl.BlockSpec((CMP_LEN, 256, 256), lambda b: (0, 0, 0)),
                  pl.BlockSpec((256, 256), lambda b: (0, 0))],
        out_specs=[pl.BlockSpec((1, C_KV_HEADS, 128, C_DH), lambda b: (b, 0, 0, 0)),
                   pl.BlockSpec((1, C_KV_HEADS, C_DH, 128), lambda b: (b, 0, 0, 0))],
        out_shape=[jax.ShapeDtypeStruct((B, C_KV_HEADS, 128, C_DH), BF16),
                   jax.ShapeDtypeStruct((B, C_KV_HEADS, C_DH, 128), BF16)],
        compiler_params=_cparams("arbitrary"),
        name="compress_prompt",
    )(raw, raw, pos256, w1bd, w2bd)


NA_TQ = 128
NA_KT = 256
NA_M = C_GROUP * NA_TQ


def _lanes4(x):
    return jnp.concatenate([x] * C_GROUP, axis=1)


def _flash_t(qs, k_ref, vt_ref, j_lo, j_hi, bias_fn, s_ref, p_ref, acc_ref):
    groups = range(C_KV_HEADS)

    def scores(j, g):
        off = pl.multiple_of(j * NA_KT, NA_KT)
        return lax.dot_general(k_ref[0, g, pl.ds(off, NA_KT), :], qs[g], NT_DIMS, preferred_element_type=F32)

    def values(j, g):
        off = pl.multiple_of(j * NA_KT, NA_KT)
        vt = vt_ref[0, g * C_DH:(g + 1) * C_DH, pl.ds(off, NA_KT)].astype(BF16)
        return jnp.dot(vt, p_ref[g], preferred_element_type=F32)

    acc_ref[...] = jnp.zeros_like(acc_ref)
    p_ref[...] = jnp.zeros_like(p_ref)
    for g in groups:
        s_ref[g] = scores(j_lo, g)
    row = lambda v: tuple(jnp.full((1, NA_M), v, F32) for _ in groups)

    def body(j, carry):
        m, l, alpha = carry
        j_next = jnp.minimum(j + 1, j_hi - 1)
        j_prev = jnp.maximum(j - 1, j_lo)
        m_out, l_out, a_out = [], [], []
        for g in groups:
            s = s_ref[g] + _lanes4(bias_fn(j, g))
            s_ref[g] = scores(j_next, g)
            acc_ref[g] = alpha[g] * acc_ref[g] + values(j_prev, g)
            m_new = jnp.maximum(m[g], jnp.max(s, axis=0, keepdims=True))
            p = jnp.exp(s - m_new)
            a = jnp.exp(m[g] - m_new)
            p_ref[g] = p.astype(BF16)
            m_out.append(m_new)
            l_out.append(a * l[g] + jnp.sum(p, axis=0, keepdims=True))
            a_out.append(a)
        return tuple(m_out), tuple(l_out), tuple(a_out)

    m, l, alpha = lax.fori_loop(j_lo, j_hi, body, (row(NEG_INF), row(0.0), row(1.0)))
    return [(alpha[g] * acc_ref[g] + values(j_hi - 1, g)) / l[g] for g in groups]


def _nsa_prompt_kernel(q_ref, qr_ref, kc_ref, vct_ref, ks_ref, vst_ref, kw_ref, vwt_ref, cg_ref, cz_ref, ovt_ref,
                       yc_ref, s_ref, p_ref, acc_ref, sel_ref):
    qi = pl.program_id(1)
    t0 = qi * NA_TQ
    n_slc = ovt_ref.shape[0]
    sig_t = _sigmoid(cg_ref[0]).T
    cmp_row = lax.broadcasted_iota(jnp.int32, (128, NA_TQ), 0)
    cmp_q = t0 + lax.broadcasted_iota(jnp.int32, (128, NA_TQ), 1)
    cmp_ok = _lanes4(jnp.where(CMP_STRIDE * cmp_row + (CMP_LEN - 1) <= cmp_q, 1.0, 0.0)) > 0.5
    jidx = lax.broadcasted_iota(jnp.int32, (n_slc, NA_TQ), 0)
    cur = (t0 + lax.broadcasted_iota(jnp.int32, (n_slc, NA_TQ), 1)) // SLC_BLK
    qrs, o_cmp = [], []
    for g in range(C_KV_HEADS):
        qg = q_ref[0, C_GROUP * g:C_GROUP * (g + 1)].reshape(NA_M, C_DH)
        qrs.append(qr_ref[0, C_GROUP * g:C_GROUP * (g + 1)].reshape(NA_M, C_DH))
        s = jnp.where(cmp_ok, lax.dot_general(kc_ref[0, g], qg, NT_DIMS, preferred_element_type=F32), NEG_INF)
        e = jnp.where(cmp_ok, jnp.exp(s - jnp.max(s, axis=0, keepdims=True)), 0.0)
        p = e / jnp.maximum(jnp.sum(e, axis=0, keepdims=True), 1e-30)
        o_cmp.append(jnp.dot(vct_ref[0, g], p.astype(BF16), preferred_element_type=F32))
        psum = p[:, 0:NA_TQ] + p[:, NA_TQ:2 * NA_TQ] + p[:, 2 * NA_TQ:3 * NA_TQ] + p[:, 3 * NA_TQ:4 * NA_TQ]
        imp = jnp.dot(ovt_ref[...], psum, precision=lax.Precision.HIGHEST, preferred_element_type=F32)
        sel_t = _rank_select(_forced_score(jidx, cur, imp), jidx, n_slc, 0)
        sel_bias = jnp.where(sel_t > 0.5, 0.0, NEG_INF)
        for jj in range(n_slc):
            sel_ref[g, jj] = sel_bias[jj:jj + 1, :]

    key_row = lax.broadcasted_iota(jnp.int32, (NA_KT, NA_TQ), 0)
    q_pos = t0 + lax.broadcasted_iota(jnp.int32, (NA_KT, NA_TQ), 1)
    blocks_per_tile = NA_KT // SLC_BLK

    def slc_mask(j, g):
        chosen = sel_ref[g, blocks_per_tile * j + blocks_per_tile - 1]
        for i in range(blocks_per_tile - 2, -1, -1):
            chosen = jnp.where(key_row < (i + 1) * SLC_BLK, sel_ref[g, blocks_per_tile * j + i], chosen)
        return jnp.where(j * NA_KT + key_row <= q_pos, chosen, NEG_INF)

    def win_mask(j, g):
        kpos = j * NA_KT + key_row
        return jnp.where(kpos <= q_pos, jnp.where(kpos > q_pos - WINDOW, 0.0, NEG_INF), NEG_INF)

    j_hi = (t0 + NA_TQ + NA_KT - 1) // NA_KT
    o_slc = _flash_t(qrs, ks_ref, vst_ref, 0, j_hi, slc_mask, s_ref, p_ref, acc_ref)
    o_win = _flash_t(qrs, kw_ref, vwt_ref, jnp.maximum(t0 - (WINDOW - 1), 0) // NA_KT, j_hi, win_mask,
                     s_ref, p_ref, acc_ref)
    heads_t = []
    for g in range(C_KV_HEADS):
        for hh in range(C_GROUP):
            hd = C_GROUP * g + hh
            cs = slice(hh * NA_TQ, (hh + 1) * NA_TQ)
            heads_t.append(sig_t[hd:hd + 1, :] * o_cmp[g][:, cs]
                           + sig_t[C_HEADS + hd:C_HEADS + hd + 1, :] * o_slc[g][:, cs]
                           + sig_t[2 * C_HEADS + hd:2 * C_HEADS + hd + 1, :] * o_win[g][:, cs])
    yc_ref[0] = (jnp.concatenate(heads_t, axis=0).T * _silu(cz_ref[0])).astype(BF16)


def _overlap_t(n_slc, n_rows):
    cs = np.arange(128) * CMP_STRIDE
    ss = np.arange(n_rows) * SLC_BLK
    ov = (cs[None, :] < ss[:, None] + SLC_BLK) & (cs[None, :] + CMP_LEN > ss[:, None])
    ov &= (np.arange(128)[None, :] < 127) & (np.arange(n_rows)[:, None] < n_slc)
    return ov.astype(np.float32)


def _nsa_prompt(proj, q_hm, qr_hm, kc, vct, ks_hm, rows_t, kw_hm, win_t):
    B, T, _ = proj.shape
    n_slc = T // SLC_BLK
    ovt = jnp.asarray(_overlap_t(n_slc, n_slc))
    qspec = pl.BlockSpec((1, C_HEADS, NA_TQ, C_DH), lambda b, t: (b, 0, t, 0))
    kspec = pl.BlockSpec((1, C_KV_HEADS, T, C_DH), lambda b, t: (b, 0, 0, 0))
    return pl.pallas_call(
        _nsa_prompt_kernel,
        grid=(B, T // NA_TQ),
        in_specs=[qspec, qspec,
                  pl.BlockSpec((1, C_KV_HEADS, 128, C_DH), lambda b, t: (b, 0, 0, 0)),
                  pl.BlockSpec((1, C_KV_HEADS, C_DH, 128), lambda b, t: (b, 0, 0, 0)),
                  kspec, pl.BlockSpec((1, 128, T), lambda b, t: (b, 3, 0)),
                  kspec, pl.BlockSpec((1, 128, T), lambda b, t: (b, 1, 0)),
                  pl.BlockSpec((1, NA_TQ, 128), lambda b, t: (b, t, COL_CG // 128)),
                  pl.BlockSpec((1, NA_TQ, 512), lambda b, t: (b, t, COL_CZ // 512)),
                  pl.BlockSpec((n_slc, 128), lambda b, t: (0, 0))],
        out_specs=pl.BlockSpec((1, NA_TQ, 512), lambda b, t: (b, t, 0)),
        out_shape=jax.ShapeDtypeStruct((B, T, 512), BF16),
        scratch_shapes=[pltpu.VMEM((C_KV_HEADS, NA_KT, NA_M), F32), pltpu.VMEM((C_KV_HEADS, NA_KT, NA_M), BF16),
                        pltpu.VMEM((C_KV_HEADS, C_DH, NA_M), F32), pltpu.VMEM((C_KV_HEADS, n_slc, 1, NA_TQ), F32)],
        compiler_params=_cparams("arbitrary", "arbitrary"),
        name="nsa_prompt",
    )(q_hm, qr_hm, kc, vct, ks_hm, rows_t, kw_hm, win_t, proj, proj, ovt)


N_PAGES = 16
ND_SB = 2


def _nsa_decode_kernel(pt_ref, *refs):
    npg = ND_SB * N_PAGES
    pages = refs[:npg]
    (q_ref, kv_ref, cg_ref, cz_ref, wb_ref, cos_ref, sin_ref, pos_ref, w1_ref, w2_ref, ov_ref, ex_ref,
     yc_ref, rows_ref, wn_ref, xk_ref, xv_ref) = refs[npg:]
    past = N_PAGES * PAGE_SIZE
    n_slc = -(-(past + 1) // SLC_BLK)
    cur = past // SLC_BLK
    cos = cos_ref[...]
    sin = sin_ref[...]

    def rope(x):
        return x * cos + jnp.concatenate([x[:, C_DH // 2:], x[:, :C_DH // 2]], axis=1) * sin

    def row_dot(a_bf, b_row):
        return jnp.sum(a_bf.astype(F32) * b_row.astype(BF16).astype(F32), axis=-1, keepdims=True)

    row8 = lax.broadcasted_iota(jnp.int32, (C_HEADS, 128), 0)
    lane = lax.broadcasted_iota(jnp.int32, (C_HEADS, 128), 1)
    g0 = row8 < C_GROUP
    g0h = g0[:, 0:C_DH]

    def spread(x):
        return jnp.concatenate([jnp.where(g0h, x, 0.0), jnp.where(g0h, 0.0, x)], axis=1).astype(BF16)

    def gather(x):
        return jnp.where(g0h, x[:, 0:C_DH], x[:, C_DH:2 * C_DH])

    def column_tile(row, reps):
        return jnp.concatenate([jnp.broadcast_to(row, (128, 128)).T] * reps, axis=1)

    rows_per_page = PAGE_SIZE // CMP_STRIDE

    def by_offset(x_t):
        return jnp.swapaxes(x_t.T.reshape(rows_per_page, CMP_STRIDE, 128), 0, 1)

    for i, pg in enumerate(pages):
        xk_ref[:, i * rows_per_page:(i + 1) * rows_per_page, :] = by_offset(pg[0:128, :])
        xv_ref[:, i * rows_per_page:(i + 1) * rows_per_page, :] = by_offset(pg[128:256, :])
    m_rows = ND_SB * 128

    def load_l(l):
        return jnp.concatenate([xk_ref[l], xv_ref[l]], axis=1)

    kcvc_all = _compress(load_l, pos_ref, w1_ref, w2_ref, m_rows).astype(BF16)

    for sb in range(ND_SB):
        spages = pages[sb * N_PAGES:(sb + 1) * N_PAGES]
        kcvc = kcvc_all[sb * 128:(sb + 1) * 128]
        q8 = q_ref[sb] * ATT_SCALE
        kv = kv_ref[sb]
        ks_new = rope(kv[4:6])
        kw_new = rope(kv[8:10])
        rows_ref[sb, 0:4, :] = kv[0:4]
        rows_ref[sb, 4:6, :] = ks_new
        rows_ref[sb, 6:8, :] = kv[6:8]
        qs = spread(q8)
        qrs = spread(rope(q8))

        s = lax.dot_general(qs, kcvc[:, 0:128], NT_DIMS, preferred_element_type=F32)
        ok = (CMP_STRIDE * lane + (CMP_LEN - 1)) <= past
        s = jnp.where(ok, s, NEG_INF)
        e = jnp.where(ok, jnp.exp(s - jnp.max(s, axis=-1, keepdims=True)), 0.0)
        p = e / jnp.maximum(jnp.sum(e, axis=-1, keepdims=True), 1e-30)
        o_cmp = gather(jnp.dot(p.astype(BF16), kcvc[:, 128:256], preferred_element_type=F32))

        ps0 = jnp.sum(jnp.where(g0, p, 0.0), axis=0, keepdims=True)
        ps1 = jnp.sum(jnp.where(g0, 0.0, p), axis=0, keepdims=True)
        imp = jnp.dot(jnp.where(g0, ps0, ps1), ov_ref[...], precision=lax.Precision.HIGHEST,
                      preferred_element_type=F32)
        score = jnp.where(lane < n_slc, _forced_score(lane, cur, imp), -2.0)
        sel = _rank_select(score, lane, n_slc, 1)
        chosen = jnp.dot(sel.astype(BF16), ex_ref[...], preferred_element_type=F32)

        s_pg, ok_pg = [], []
        for i, pg in enumerate(spages):
            ok_pg.append(chosen[:, i * PAGE_SIZE:(i + 1) * PAGE_SIZE] > 0.5)
            sp = jnp.dot(qrs, pg[256:384, :].astype(BF16), preferred_element_type=F32)
            s_pg.append(jnp.where(ok_pg[i], sp, NEG_INF))
        new_ok = sel[:, cur:cur + 1] > 0.5
        s_new = jnp.where(new_ok, row_dot(qrs, jnp.concatenate([ks_new[0:1], ks_new[1:2]], axis=1)), NEG_INF)
        mx = s_new
        for sp in s_pg:
            mx = jnp.maximum(mx, jnp.max(sp, axis=-1, keepdims=True))
        p_new = jnp.where(new_ok, jnp.exp(s_new - mx), 0.0)
        den = p_new
        acc = p_new * jnp.concatenate([kv[6:7], kv[7:8]], axis=1)
        for i, pg in enumerate(spages):
            pp = jnp.where(ok_pg[i], jnp.exp(s_pg[i] - mx), 0.0)
            den = den + jnp.sum(pp, axis=-1, keepdims=True)
            acc = acc + lax.dot_general(pp.astype(BF16), pg[384:512, :].astype(BF16), NT_DIMS,
                                        preferred_element_type=F32)
        o_slc = gather(acc / den)

        wb = wb_ref[sb]
        nw = wb.shape[1]
        sw = jnp.dot(qrs, wb[0:128, :].astype(BF16), preferred_element_type=F32)
        wpos = (past - nw) + lax.broadcasted_iota(jnp.int32, (C_HEADS, nw), 1)
        w_ok = jnp.where(wpos <= past, jnp.where(wpos > past - WINDOW, 1.0, 0.0), 0.0) > 0.5
        sw = jnp.where(w_ok, sw, NEG_INF)
        kw_row = jnp.concatenate([kw_new[0:1], kw_new[1:2]], axis=1)
        vw_row = jnp.concatenate([kv[10:11], kv[11:12]], axis=1)
        sw_new = row_dot(qrs, kw_row)
        mw = jnp.maximum(jnp.max(sw, axis=-1, keepdims=True), sw_new)
        pw = jnp.where(w_ok, jnp.exp(sw - mw), 0.0)
        pw_new = jnp.exp(sw_new - mw)
        accw = lax.dot_general(pw.astype(BF16), wb[128:256, :].astype(BF16), NT_DIMS,
                               preferred_element_type=F32) + pw_new * vw_row
        o_win = gather(accw / (jnp.sum(pw, axis=-1, keepdims=True) + pw_new))

        sig = _sigmoid(cg_ref[sb])
        oc = sig[:, 0:1] * o_cmp + sig[:, 1:2] * o_slc + sig[:, 2:3] * o_win
        yc_ref[sb] = (oc * _silu(cz_ref[sb])).astype(BF16)

        new_cols = jnp.concatenate([column_tile(kw_row, nw // 128), column_tile(vw_row, nw // 128)], axis=0)
        wlane = lax.broadcasted_iota(jnp.int32, wb.shape, 1)
        wn_ref[sb] = jnp.where(wlane == nw - 1, new_cols, pltpu.roll(wb, nw - 1, 1))


def _nsa_decode(cache_t, layer, page_table, win_t, q8, kv12, cg83, cz8, cos64, sin64, pos256, w1bd, w2bd):
    nb = q8.shape[0]
    nw = win_t.shape[3]
    past = N_PAGES * PAGE_SIZE
    n_slc = -(-(past + 1) // SLC_BLK)
    ov = jnp.asarray(_overlap_t(n_slc, 128).T)
    keys = np.arange(past)
    ex = jnp.asarray((np.arange(128)[:, None] == keys[None, :] // SLC_BLK).astype(np.float32)).astype(BF16)

    def page_spec(sb, i):
        return pl.BlockSpec((None, None, 512, PAGE_SIZE), lambda b, pt: (layer, pt[b * ND_SB + sb, i], 0, 0))

    c3 = lambda n, d: pl.BlockSpec((ND_SB, n, d), lambda b, pt: (b, 0, 0))
    z2 = lambda b, pt: (0, 0)
    grid_spec = pltpu.PrefetchScalarGridSpec(
        num_scalar_prefetch=1,
        grid=(nb // ND_SB,),
        in_specs=[page_spec(sb, i) for sb in range(ND_SB) for i in range(N_PAGES)] + [
            c3(C_HEADS, C_DH), c3(12, C_DH), c3(C_HEADS, 3), c3(C_HEADS, C_DH),
            pl.BlockSpec((None, ND_SB, 256, nw), lambda b, pt: (layer, b, 0, 0)),
            pl.BlockSpec((1, C_DH), z2), pl.BlockSpec((1, C_DH), z2),
            pl.BlockSpec((CMP_LEN, 256), z2),
            pl.BlockSpec((CMP_LEN, 256, 256), lambda b, pt: (0, 0, 0)),
            pl.BlockSpec((256, 256), z2),
            pl.BlockSpec((128, 128), z2),
            pl.BlockSpec((128, past), z2)],
        out_specs=[c3(C_HEADS, C_DH), c3(C_HEADS, C_DH), pl.BlockSpec((ND_SB, 256, nw), lambda b, pt: (b, 0, 0))],
        scratch_shapes=[pltpu.VMEM((CMP_STRIDE, ND_SB * 128, 128), F32),
                        pltpu.VMEM((CMP_STRIDE, ND_SB * 128, 128), F32)],
    )
    return pl.pallas_call(
        _nsa_decode_kernel,
        grid_spec=grid_spec,
        out_shape=[jax.ShapeDtypeStruct((nb, C_HEADS, C_DH), BF16),
                   jax.ShapeDtypeStruct((nb, C_HEADS, C_DH), F32),
                   jax.ShapeDtypeStruct((nb, 256, nw), F32)],
        compiler_params=_cparams("arbitrary"),
        name="nsa_decode",
    )(page_table, *([cache_t] * (ND_SB * N_PAGES)), q8, kv12, cg83, cz8, win_t, cos64, sin64, pos256, w1bd, w2bd,
      ov, ex)


def _merge_kernel(ya_ref, yb_ref, yc_ref, mg_ref, x_ref, gate_ref, wb_ref, wo_ref, lg_ref, lb_ref, o_ref):
    merged = None
    for n, y_ref in enumerate((ya_ref, yb_ref, yc_ref)):
        br = jnp.dot(y_ref[0], wb_ref[n], preferred_element_type=F32)
        term = _sigmoid(mg_ref[0, :, n * D_MODEL:(n + 1) * D_MODEL]) * br
        merged = term if merged is None else merged + term
    out = jnp.dot(merged.astype(BF16), wo_ref[...], preferred_element_type=F32)
    z = ALPHA * x_ref[0] + gate_ref[0] * out
    mu = jnp.mean(z, axis=-1, keepdims=True)
    zc = z - mu
    var = jnp.mean(zc * zc, axis=-1, keepdims=True)
    o_ref[0] = zc * lax.rsqrt(var + LN_EPS) * lg_ref[...] + lb_ref[...]


def _merge(ya, yb, yc, proj, x, gate, wb_bf, wo_bf, ln_g, ln_b, tm):
    B, T, _ = x.shape
    ys = pl.BlockSpec((1, tm, 512), lambda b, t: (b, t, 0))
    gs = (pl.BlockSpec((1, 1, D_MODEL), lambda b, t: (b, 0, 0)) if gate.shape[1] == 1
          else pl.BlockSpec((1, tm, D_MODEL), lambda b, t: (b, t, 0)))
    return pl.pallas_call(
        _merge_kernel,
        grid=(B, T // tm),
        in_specs=[ys, ys, ys,
                  pl.BlockSpec((1, tm, 3 * D_MODEL), lambda b, t: (b, t, COL_MG // (3 * D_MODEL))),
                  pl.BlockSpec((1, tm, D_MODEL), lambda b, t: (b, t, 0)),
                  gs,
                  pl.BlockSpec((3, 512, D_MODEL), lambda b, t: (0, 0, 0)),
                  pl.BlockSpec((D_MODEL, D_MODEL), lambda b, t: (0, 0)),
                  pl.BlockSpec((1, D_MODEL), lambda b, t: (0, 0)),
                  pl.BlockSpec((1, D_MODEL), lambda b, t: (0, 0))],
        out_specs=pl.BlockSpec((1, tm, D_MODEL), lambda b, t: (b, t, 0)),
        out_shape=jax.ShapeDtypeStruct((B, T, D_MODEL), F32),
        compiler_params=_cparams("arbitrary", "arbitrary"),
        name="merge",
    )(ya, yb, yc, proj, x, gate, wb_bf, wo_bf, ln_g, ln_b)


def _block_diag(blocks):
    n = len(blocks)
    z = jnp.zeros_like(blocks[0])
    return jnp.concatenate(
        [jnp.concatenate([b if j == i else z for j in range(n)], axis=-1) for i, b in enumerate(blocks)], axis=-2)


def _rope_tables(pos, width):
    inv = ROPE_THETA ** (-jnp.arange(0, C_DH, 2, dtype=F32) / C_DH)
    ang = pos.astype(F32)[:, None] * inv[None, :]
    cos = jnp.cos(ang)
    sin = jnp.sin(ang)
    reps = width // C_DH
    return (jnp.tile(jnp.concatenate([cos, cos], axis=1), (1, reps)),
            jnp.tile(jnp.concatenate([-sin, sin], axis=1), (1, reps)))


def _token_major(x_t, lead):
    kinds = x_t.shape[-2] // (C_KV_HEADS * C_DH)
    x = x_t.reshape(*x_t.shape[:-2], kinds, C_KV_HEADS, C_DH, x_t.shape[-1])
    return jnp.moveaxis(x, -1, lead)


def kernel(x_prompt, x_sample, c_prompt, c_sample, cache_nsa_kv, page_table, state_win_kv, state_hgrn,
           state_rglru, state_conv, ada_w, ada_b, w_in, a_lb, a_norm_g, b_conv_w, b_conv_b, b_gate_a_w,
           b_gate_a_b, b_gate_x_w, b_gate_x_b, b_lambda, c_cmp_pos, c_cmp_w1, c_cmp_w2, w_branch, w_out,
           ln_g, ln_b):
    bp, seq, _ = x_prompt.shape
    bs = x_sample.shape[0]
    n_pages = page_table.shape[1]
    n_pool = cache_nsa_kv.shape[1]
    nw = state_win_kv.shape[2]
    past_len = n_pages * cache_nsa_kv.shape[2]
    assert n_pages == N_PAGES and cache_nsa_kv.shape[2] == PAGE_SIZE and x_sample.shape[1] == 1

    lb_all = jnp.cumsum(jax.nn.softmax(a_lb.astype(F32), axis=0), axis=0)
    lb_all = lb_all - lb_all[0:1]
    loglb = jnp.log(lb_all)
    l1mlb = jnp.log1p(-lb_all)
    w_in_p = jnp.concatenate(
        [w_in[:, :, 4888:7960], w_in[:, :, 0:3584], w_in[:, :, 4376:4888], w_in[:, :, 3584:4352],
         w_in[:, :, 4352:4376], jnp.zeros((DEPTH, D_MODEL, N_PROJ - 7960), w_in.dtype)], axis=2).astype(BF16)
    wb_bf = w_branch.astype(BF16)
    wo_bf = w_out.astype(BF16)
    cache_t = cache_nsa_kv.transpose(0, 1, 3, 4, 5, 2).reshape(DEPTH, n_pool, 512, PAGE_SIZE)
    win_t = state_win_kv.transpose(0, 1, 3, 4, 5, 2).reshape(DEPTH, bs, 256, nw)
    cos_p, sin_p = _rope_tables(jnp.arange(seq), 128)
    cos_s, sin_s = _rope_tables(past_len + jnp.arange(1), C_DH)

    ada = _ada_all(jnp.concatenate([c_prompt, c_sample], axis=0), ada_w, ada_b)

    xp, xs = x_prompt, x_sample.reshape(1, bs, D_MODEL)
    outs = {k: [] for k in ("nsa_p", "nsa_s", "win_p", "win_s", "hg_p", "hg_s", "rg_p", "rg_s", "cv_p", "cv_s")}
    for l in range(DEPTH):
        shift, scale, gate = ada[l, :, 0:D_MODEL], ada[l, :, D_MODEL:2 * D_MODEL], ada[l, :, 2 * D_MODEL:]
        rg_w = (b_conv_w[l], b_conv_b[l].reshape(1, -1),
                _block_diag([b_gate_a_w[l, i] for i in range(B_BLOCKS)]).astype(BF16), b_gate_a_b[l].reshape(1, -1),
                _block_diag([b_gate_x_w[l, i] for i in range(B_BLOCKS)]).astype(BF16), b_gate_x_b[l].reshape(1, -1),
                b_lambda[l].reshape(1, -1))
        pos256 = jnp.concatenate([c_cmp_pos[l, 0], c_cmp_pos[l, 0], c_cmp_pos[l, 1], c_cmp_pos[l, 1]], axis=1)
        w1r = c_cmp_w1[l].reshape(2, CMP_LEN, C_DH, C_DH)
        w1bd = _block_diag([w1r[0], w1r[0], w1r[1], w1r[1]]).astype(BF16)
        w2bd = _block_diag([c_cmp_w2[l, 0], c_cmp_w2[l, 0], c_cmp_w2[l, 1], c_cmp_w2[l, 1]]).astype(BF16)
        gain = a_norm_g[l].reshape(1, -1)
        lg, lbias = ln_g[l].reshape(1, -1), ln_b[l].reshape(1, -1)

        proj = _inproj(xp, scale[:bp, None], shift[:bp, None], w_in_p[l], 1024)
        ya, s_new = _hgrn_prompt(proj, loglb[l].reshape(1, -1), l1mlb[l].reshape(1, -1), gain)
        yb, h_new, cv_new = _rglru_prompt(proj, rg_w)
        rows_t, wrow_t, raw, q_hm, qr_hm, ks_hm, kw_hm = _nsa_prep(proj, cos_p, sin_p)
        kc, vct = _compress_prompt(raw, pos256, w1bd, w2bd)
        yc = _nsa_prompt(proj, q_hm, qr_hm, kc, vct, ks_hm, rows_t, kw_hm, wrow_t)
        xp = _merge(ya, yb, yc, proj, xp, gate[:bp, None], wb_bf[l], wo_bf[l], lg, lbias, 256)
        outs["nsa_p"].append(rows_t)
        outs["win_p"].append(wrow_t[:, :, seq - min(WINDOW, seq):])
        outs["hg_p"].append(s_new)
        outs["rg_p"].append(h_new.reshape(bp, B_WIDTH))
        outs["cv_p"].append(cv_new)

        proj_s = _inproj(xs, scale[None, bp:], shift[None, bp:], w_in_p[l], bs).reshape(bs, N_PROJ)
        s_s, ya_s = _hgrn_decode(state_hgrn, l, proj_s, loglb[l].reshape(A_HEADS, A_DK, 1),
                                 l1mlb[l].reshape(A_HEADS, A_DK, 1), gain)
        yb_s, h_s, cv_s = _rglru_decode(proj_s, state_conv[l].reshape(bs, -1), state_rglru[l], rg_w)
        q8 = proj_s[:, COL_CQ:COL_CQ + 512].reshape(bs, C_HEADS, C_DH)
        cz8 = proj_s[:, COL_CZ:COL_CZ + 512].reshape(bs, C_HEADS, C_DH)
        kv12 = proj_s[:, COL_CKV:COL_CKV + 768].reshape(bs, 12, C_DH)
        cg83 = proj_s[:, COL_CG:COL_CG + 3 * C_HEADS].reshape(bs, 3, C_HEADS).transpose(0, 2, 1)
        yc_s, rows_s, wnew_t = _nsa_decode(cache_t, l, page_table, win_t, q8, kv12, cg83, cz8, cos_s, sin_s,
                                           pos256, w1bd, w2bd)
        xs = _merge(ya_s.reshape(1, bs, 512), yb_s.reshape(1, bs, 512), yc_s.reshape(1, bs, 512),
                    proj_s.reshape(1, bs, N_PROJ), xs, gate[None, bp:], wb_bf[l], wo_bf[l], lg, lbias, bs)
        outs["nsa_s"].append(rows_s.reshape(bs, 1, 4, C_KV_HEADS, C_DH))
        outs["win_s"].append(wnew_t)
        outs["hg_s"].append(s_s)
        outs["rg_s"].append(h_s)
        outs["cv_s"].append(cv_s.reshape(bs, CONV_W - 1, B_WIDTH))

    st = {k: jnp.stack(v, axis=0) for k, v in outs.items()}
    return (xp, xs.reshape(bs, 1, D_MODEL), _token_major(st["nsa_p"], 2), st["nsa_s"],
            _token_major(st["win_p"], 2), _token_major(st["win_s"], 2),
            st["hg_p"], st["hg_s"], st["rg_p"], st["rg_s"], st["cv_p"], st["cv_s"])
```

```python
import numpy as np
import jax
import jax.numpy as jnp
from jax import lax
from jax.experimental import pallas as pl
from jax.experimental.pallas import tpu as pltpu

F32 = jnp.float32
BF16 = jnp.bfloat16

D_MODEL = 1024
DEPTH = 4
PAGE_SIZE = 128
A_HEADS = 4
A_DK = 128
A_CHUNK = 64
B_WIDTH = 512
B_BLOCKS = 8
CONV_W = 4
LRU_C = 8.0
C_HEADS = 8
C_KV_HEADS = 2
C_GROUP = 4
C_DH = 64
CMP_LEN = 32
CMP_STRIDE = 16
SLC_BLK = 64
SLC_TOPN = 16
WINDOW = 512
ROPE_THETA = 10000.0
ATT_SCALE = C_DH ** -0.5
LOG2_E = 1.4426950408889634
ALPHA = (2 * DEPTH) ** 0.25
LN_EPS = 1e-5
NEG_INF = -1e30
FORCED_SCORE = 1e9

COL_MG, COL_AQ, COL_AF, COL_AI, COL_AZ = 0, 3072, 3584, 4096, 4608
COL_BX, COL_BZ, COL_CQ, COL_CZ, COL_CKV, COL_CG = 5120, 5632, 6144, 6656, 7168, 7936
N_PROJ = 8064
PROJ_TN = 2688

VMEM_LIMIT = 56 * 1024 * 1024

NT_DIMS = (((1,), (1,)), ((), ()))
TN_DIMS = (((0,), (0,)), ((), ()))


def _cparams(*sem):
    return pltpu.CompilerParams(dimension_semantics=sem, vmem_limit_bytes=VMEM_LIMIT)


def _sigmoid(x):
    return jax.nn.sigmoid(x)


def _silu(x):
    return x * jax.nn.sigmoid(x)


def _log_f(af, loglb, l1mlb):
    ls = jnp.minimum(af, 0.0) - jnp.log1p(jnp.exp(-jnp.abs(af)))
    y = l1mlb + ls
    return jnp.maximum(loglb, y) + jnp.log1p(jnp.exp(-jnp.abs(loglb - y)))


def _ada_kernel(c_ref, w_ref, b_ref, o_ref):
    o_ref[0] = jnp.dot(c_ref[...], w_ref[0].astype(BF16), preferred_element_type=F32) + b_ref[0]


def _ada_all(c_all, ada_w, ada_b):
    nb = c_all.shape[0]
    return pl.pallas_call(
        _ada_kernel,
        grid=(DEPTH, 3),
        in_specs=[
            pl.BlockSpec((nb, D_MODEL), lambda l, j: (0, 0)),
            pl.BlockSpec((1, D_MODEL, D_MODEL), lambda l, j: (l, 0, j)),
            pl.BlockSpec((1, 1, D_MODEL), lambda l, j: (l, 0, j)),
        ],
        out_specs=pl.BlockSpec((1, nb, D_MODEL), lambda l, j: (l, 0, j)),
        out_shape=jax.ShapeDtypeStruct((DEPTH, nb, 3 * D_MODEL), F32),
        compiler_params=_cparams("arbitrary", "arbitrary"),
        name="ada",
    )(c_all.astype(BF16), ada_w, ada_b.reshape(DEPTH, 1, 3 * D_MODEL))


def _inproj_kernel(x_ref, sc_ref, sh_ref, w_ref, o_ref, h_ref):
    @pl.when(pl.program_id(2) == 0)
    def _():
        h_ref[...] = (x_ref[0] * (1.0 + sc_ref[0]) + sh_ref[0]).astype(BF16)

    o_ref[0] = jnp.dot(h_ref[...], w_ref[...], preferred_element_type=F32)


def _inproj(x, scale, shift, w_bf, tt):
    B, T, _ = x.shape
    ts = scale.shape[1]
    mod_spec = (pl.BlockSpec((1, 1, D_MODEL), lambda b, t, n: (b, 0, 0)) if ts == 1
                else pl.BlockSpec((1, tt, D_MODEL), lambda b, t, n: (b, t, 0)))
    return pl.pallas_call(
        _inproj_kernel,
        grid=(B, T // tt, N_PROJ // PROJ_TN),
        in_specs=[
            pl.BlockSpec((1, tt, D_MODEL), lambda b, t, n: (b, t, 0)),
            mod_spec, mod_spec,
            pl.BlockSpec((D_MODEL, PROJ_TN), lambda b, t, n: (0, n)),
        ],
        out_specs=pl.BlockSpec((1, tt, PROJ_TN), lambda b, t, n: (b, t, n)),
        out_shape=jax.ShapeDtypeStruct((B, T, N_PROJ), F32),
        scratch_shapes=[pltpu.VMEM((tt, D_MODEL), BF16)],
        compiler_params=_cparams("arbitrary", "arbitrary", "arbitrary"),
        name="inproj",
    )(x, scale, shift, w_bf)


HG_TT = 256
HG_HP = 4
HG_LEVELS = (1, 2, 4, 8, 16, 32)


def _hgrn_level_table():
    t = np.arange(HG_TT)[:, None]
    s = np.arange(HG_TT)[None, :]
    lvl = np.full((HG_TT, HG_TT), -1, np.int32)
    same = (t // A_CHUNK) == (s // A_CHUNK)
    for i, h in enumerate(HG_LEVELS):
        m = same & ((t // h) == (s // h) + 1) & (((s // h) % 2) == 0)
        lvl[m] = i
    lvl[t == s] = len(HG_LEVELS)
    return lvl


def _hgrn_prompt_kernel(aq_ref, af_ref, ai_ref, az_ref, llb_ref, l1_ref, g_ref, lvl_ref,
                        ya_ref, so_ref, s_ref):
    t = pl.program_id(2)

    @pl.when(t == 0)
    def _():
        s_ref[...] = jnp.zeros_like(s_ref)

    for hh in range(HG_HP):
        cs = slice(hh * A_DK, (hh + 1) * A_DK)
        ya, s_new = _hgrn_head(aq_ref[0, :, cs], af_ref[0, :, cs], ai_ref[0, :, cs], az_ref[0, :, cs],
                               llb_ref[:, cs], l1_ref[:, cs], g_ref[:, cs], lvl_ref[...], s_ref[hh])
        ya_ref[0, :, cs] = ya
        s_ref[hh] = s_new

    @pl.when(t == pl.num_programs(2) - 1)
    def _():
        so_ref[0] = s_ref[...]


def _hgrn_head(aq, af, v, az, loglb, l1mlb, gain, lvl, s):
    q = _silu(aq)
    lf = _log_f(af, loglb, l1mlb)
    k = 1.0 - jnp.exp(lf)
    vb = v.astype(BF16)
    row = lax.broadcasted_iota(jnp.int32, (HG_TT, A_DK), 0)

    cin = lf
    rex = jnp.zeros_like(lf)
    att = jnp.where(lvl == len(HG_LEVELS), jnp.sum(q * k, axis=-1, keepdims=True), 0.0)
    for i, h in enumerate(HG_LEVELS):
        qh = (q * jnp.exp(cin)).astype(BF16)
        kh = (k * jnp.exp(rex)).astype(BF16)
        a = lax.dot_general(qh, kh, NT_DIMS, preferred_element_type=F32)
        att = jnp.where(lvl == i, a, att)
        tot = cin + rex
        upper = ((row // h) % 2) == 1
        cin = cin + jnp.where(upper, pltpu.roll(tot, h, 0), 0.0)
        rex = rex + jnp.where(upper, 0.0, pltpu.roll(tot, HG_TT - h, 0))

    o = jnp.dot(att.astype(BF16), vb, preferred_element_type=F32)
    qb = (q * jnp.exp(cin)).astype(BF16)
    kd = (k * jnp.exp(rex)).astype(BF16)
    tot = cin + rex
    inter = []
    for c in range(HG_TT // A_CHUNK):
        sl = slice(c * A_CHUNK, (c + 1) * A_CHUNK)
        inter.append(jnp.dot(qb[sl], s.astype(BF16), preferred_element_type=F32))
        u = lax.dot_general(kd[sl], vb[sl], TN_DIMS, preferred_element_type=F32)
        e_rows = jnp.broadcast_to(jnp.exp(tot[c * A_CHUNK:c * A_CHUNK + 1]), (A_DK, A_DK))
        s = e_rows.T * s + u
    o = o + jnp.concatenate(inter, axis=0)

    ms = jnp.mean(o * o, axis=-1, keepdims=True)
    return ((o * lax.rsqrt(ms + LN_EPS)) * gain * _silu(az)).astype(BF16), s


def _hgrn_prompt(proj, loglb, l1mlb, gain):
    B, T, _ = proj.shape
    lvl = jnp.asarray(_hgrn_level_table())
    width = HG_HP * A_DK

    def col(base):
        return pl.BlockSpec((1, HG_TT, width), lambda b, h, t: (b, t, base // width + h))

    vec = pl.BlockSpec((1, width), lambda b, h, t: (0, h))
    return pl.pallas_call(
        _hgrn_prompt_kernel,
        grid=(B, A_HEADS // HG_HP, T // HG_TT),
        in_specs=[col(COL_AQ), col(COL_AF), col(COL_AI), col(COL_AZ), vec, vec, vec,
                  pl.BlockSpec((HG_TT, HG_TT), lambda b, h, t: (0, 0))],
        out_specs=[pl.BlockSpec((1, HG_TT, width), lambda b, h, t: (b, t, h)),
                   pl.BlockSpec((1, HG_HP, A_DK, A_DK), lambda b, h, t: (b, h, 0, 0))],
        out_shape=[jax.ShapeDtypeStruct((B, T, A_HEADS * A_DK), BF16),
                   jax.ShapeDtypeStruct((B, A_HEADS, A_DK, A_DK), F32)],
        scratch_shapes=[pltpu.VMEM((HG_HP, A_DK, A_DK), F32)],
        compiler_params=_cparams("arbitrary", "arbitrary", "arbitrary"),
        name="hgrn_prompt",
    )(proj, proj, proj, proj, loglb, l1mlb, gain, lvl)


HG_SB = 16


def _hgrn_decode_kernel(s_ref, aqc_ref, afc_ref, ai_ref, az_ref, llb_ref, l1_ref, g_ref, so_ref, ya_ref):
    aqc = aqc_ref[0, 0]
    qc = _silu(aqc)
    lf = _log_f(afc_ref[0, 0], llb_ref[0], l1_ref[0])
    ec = jnp.exp(lf)
    kc = 1.0 - ec
    v = ai_ref[...]
    rows = []
    for r in range(HG_SB):
        sn = ec[:, r:r + 1] * s_ref[r, 0] + kc[:, r:r + 1] * v[r:r + 1, :]
        so_ref[r, 0] = sn
        rows.append(jnp.sum(qc[:, r:r + 1] * sn, axis=0, keepdims=True))
    o = jnp.concatenate(rows, axis=0)
    ms = jnp.mean(o * o, axis=-1, keepdims=True)
    ya_ref[...] = ((o * lax.rsqrt(ms + LN_EPS)) * g_ref[...] * _silu(az_ref[...])).astype(BF16)


def _hgrn_decode(state_all, layer, proj_s, loglb_c, l1mlb_c, gain):
    nb = proj_s.shape[0]
    nblk = nb // HG_SB

    def cols(base):
        a = proj_s[:, base:base + A_HEADS * A_DK].reshape(nblk, HG_SB, A_HEADS, A_DK)
        return a.transpose(2, 0, 3, 1)

    st_spec = pl.BlockSpec((None, HG_SB, 1, A_DK, A_DK), lambda h, i: (layer, i, h, 0, 0))
    col_spec = pl.BlockSpec((1, 1, A_DK, HG_SB), lambda h, i: (h, i, 0, 0))
    cvec = pl.BlockSpec((1, A_DK, 1), lambda h, i: (h, 0, 0))
    return pl.pallas_call(
        _hgrn_decode_kernel,
        grid=(A_HEADS, nblk),
        in_specs=[st_spec, col_spec, col_spec,
                  pl.BlockSpec((HG_SB, A_DK), lambda h, i: (i, COL_AI // A_DK + h)),
                  pl.BlockSpec((HG_SB, A_DK), lambda h, i: (i, COL_AZ // A_DK + h)),
                  cvec, cvec,
                  pl.BlockSpec((1, A_DK), lambda h, i: (0, h))],
        out_specs=[pl.BlockSpec((HG_SB, 1, A_DK, A_DK), lambda h, i: (i, h, 0, 0)),
                   pl.BlockSpec((HG_SB, A_DK), lambda h, i: (i, h))],
        out_shape=[jax.ShapeDtypeStruct((nb, A_HEADS, A_DK, A_DK), F32),
                   jax.ShapeDtypeStruct((nb, A_HEADS * A_DK), BF16)],
        compiler_params=_cparams("arbitrary", "arbitrary"),
        name="hgrn_decode",
    )(state_all, cols(COL_AQ), cols(COL_AF), proj_s, proj_s, loglb_c, l1mlb_c, gain)


RG_TT = 256


def _rglru_gates(u, ga_ref, gab_ref, gx_ref, gxb_ref, lam_ref):
    ub = u.astype(BF16)
    r = _sigmoid(jnp.dot(ub, ga_ref[...], preferred_element_type=F32) + gab_ref[...])
    ig = _sigmoid(jnp.dot(ub, gx_ref[...], preferred_element_type=F32) + gxb_ref[...])
    nl = -lam_ref[...]
    softplus = jnp.maximum(nl, 0.0) + jnp.log1p(jnp.exp(-jnp.abs(nl)))
    a = jnp.exp(-LRU_C * r * softplus)
    return a, jnp.sqrt(1.0 - a * a) * (ig * u)


def _rglru_prompt_kernel(bx_ref, bz_ref, cw_ref, cb_ref, ga_ref, gab_ref, gx_ref, gxb_ref, lam_ref,
                         yb_ref, h_ref, cv_ref, ext_ref, hc_ref):
    t = pl.program_id(1)

    @pl.when(t == 0)
    def _():
        ext_ref[0:8, :] = jnp.zeros((8, B_WIDTH), F32)
        hc_ref[...] = jnp.zeros_like(hc_ref)

    x = bx_ref[0]
    ext_ref[8:8 + RG_TT, :] = x
    u = cb_ref[...] + (ext_ref[5:5 + RG_TT, :] * cw_ref[0:1, :] + ext_ref[6:6 + RG_TT, :] * cw_ref[1:2, :]
                       + ext_ref[7:7 + RG_TT, :] * cw_ref[2:3, :] + x * cw_ref[3:4, :])
    ext_ref[0:8, :] = x[RG_TT - 8:RG_TT, :]
    a, bt = _rglru_gates(u, ga_ref, gab_ref, gx_ref, gxb_ref, lam_ref)
    row = lax.broadcasted_iota(jnp.int32, (RG_TT, 128), 0)
    z = bz_ref[0]
    for c in range(B_WIDTH // 128):
        cs = slice(c * 128, (c + 1) * 128)
        ac = a[:, cs]
        bc = bt[:, cs] + jnp.where(row == 0, ac * hc_ref[0:1, cs], 0.0)
        s = 1
        while s < RG_TT:
            keep = row >= s
            bc = jnp.where(keep, ac * pltpu.roll(bc, s, 0) + bc, bc)
            ac = jnp.where(keep, ac * pltpu.roll(ac, s, 0), ac)
            s *= 2
        hc_ref[0:1, cs] = bc[RG_TT - 1:RG_TT, :]
        yb_ref[0, :, cs] = (bc * _silu(z[:, cs])).astype(BF16)

    @pl.when(t == pl.num_programs(1) - 1)
    def _():
        h_ref[0] = hc_ref[0:1, :]
        cv_ref[0] = x[RG_TT - (CONV_W - 1):RG_TT, :]


def _rglru_weight_specs(idx):
    return [pl.BlockSpec((CONV_W, B_WIDTH), idx), pl.BlockSpec((1, B_WIDTH), idx),
            pl.BlockSpec((B_WIDTH, B_WIDTH), idx), pl.BlockSpec((1, B_WIDTH), idx),
            pl.BlockSpec((B_WIDTH, B_WIDTH), idx), pl.BlockSpec((1, B_WIDTH), idx),
            pl.BlockSpec((1, B_WIDTH), idx)]


def _rglru_prompt(proj, w):
    B, T, _ = proj.shape
    return pl.pallas_call(
        _rglru_prompt_kernel,
        grid=(B, T // RG_TT),
        in_specs=[pl.BlockSpec((1, RG_TT, B_WIDTH), lambda b, t: (b, t, COL_BX // B_WIDTH)),
                  pl.BlockSpec((1, RG_TT, B_WIDTH), lambda b, t: (b, t, COL_BZ // B_WIDTH))]
        + _rglru_weight_specs(lambda b, t: (0, 0)),
        out_specs=[pl.BlockSpec((1, RG_TT, B_WIDTH), lambda b, t: (b, t, 0)),
                   pl.BlockSpec((1, 1, B_WIDTH), lambda b, t: (b, 0, 0)),
                   pl.BlockSpec((1, CONV_W - 1, B_WIDTH), lambda b, t: (b, 0, 0))],
        out_shape=[jax.ShapeDtypeStruct((B, T, B_WIDTH), BF16),
                   jax.ShapeDtypeStruct((B, 1, B_WIDTH), F32),
                   jax.ShapeDtypeStruct((B, CONV_W - 1, B_WIDTH), F32)],
        scratch_shapes=[pltpu.VMEM((8 + RG_TT, B_WIDTH), F32), pltpu.VMEM((8, B_WIDTH), F32)],
        compiler_params=_cparams("arbitrary", "arbitrary"),
        name="rglru_prompt",
    )(proj, proj, *w)


def _rglru_decode_kernel(bx_ref, bz_ref, cv0_ref, h0_ref, cw_ref, cb_ref, ga_ref, gab_ref, gx_ref, gxb_ref,
                         lam_ref, yb_ref, h_ref, cv_ref):
    x = bx_ref[...]
    c0 = cv0_ref[:, 0:B_WIDTH]
    c1 = cv0_ref[:, B_WIDTH:2 * B_WIDTH]
    c2 = cv0_ref[:, 2 * B_WIDTH:3 * B_WIDTH]
    u = cb_ref[...] + (c0 * cw_ref[0:1, :] + c1 * cw_ref[1:2, :] + c2 * cw_ref[2:3, :] + x * cw_ref[3:4, :])
    a, bt = _rglru_gates(u, ga_ref, gab_ref, gx_ref, gxb_ref, lam_ref)
    h = a * h0_ref[...] + bt
    h_ref[...] = h
    yb_ref[...] = (h * _silu(bz_ref[...])).astype(BF16)
    cv_ref[:, 0:B_WIDTH] = c1
    cv_ref[:, B_WIDTH:2 * B_WIDTH] = c2
    cv_ref[:, 2 * B_WIDTH:3 * B_WIDTH] = x


def _rglru_decode(proj_s, conv0, h0, w):
    nb = proj_s.shape[0]
    z2 = lambda i: (0, 0)
    return pl.pallas_call(
        _rglru_decode_kernel,
        grid=(1,),
        in_specs=[pl.BlockSpec((nb, B_WIDTH), lambda i: (0, COL_BX // B_WIDTH)),
                  pl.BlockSpec((nb, B_WIDTH), lambda i: (0, COL_BZ // B_WIDTH)),
                  pl.BlockSpec((nb, 3 * B_WIDTH), z2), pl.BlockSpec((nb, B_WIDTH), z2)]
        + _rglru_weight_specs(z2),
        out_specs=[pl.BlockSpec((nb, B_WIDTH), z2), pl.BlockSpec((nb, B_WIDTH), z2),
                   pl.BlockSpec((nb, 3 * B_WIDTH), z2)],
        out_shape=[jax.ShapeDtypeStruct((nb, B_WIDTH), BF16), jax.ShapeDtypeStruct((nb, B_WIDTH), F32),
                   jax.ShapeDtypeStruct((nb, 3 * B_WIDTH), F32)],
        compiler_params=_cparams("arbitrary"),
        name="rglru_decode",
    )(proj_s, proj_s, conv0, h0, *w)


def _gelu_tanh(x):
    return 0.5 * x * (1.0 + jnp.tanh(0.7978845608028654 * (x + 0.044715 * (x * x * x))))


def _compress(load_l, pos_ref, w1_ref, w2_ref, m):
    acc_a = jnp.zeros((m, 256), F32)
    acc_b = jnp.zeros((m, 256), F32)
    for l in range(CMP_STRIDE):
        xl = load_l(l)
        acc_a = acc_a + jnp.dot((xl + pos_ref[l:l + 1, :]).astype(BF16), w1_ref[l], preferred_element_type=F32)
        acc_b = acc_b + jnp.dot((xl + pos_ref[CMP_STRIDE + l:CMP_STRIDE + l + 1, :]).astype(BF16),
                                w1_ref[CMP_STRIDE + l], preferred_element_type=F32)
    hid = _gelu_tanh(acc_a + pltpu.roll(acc_b, m - 1, 0))
    return jnp.dot(hid.astype(BF16), w2_ref[...], preferred_element_type=F32)


def _rank_select(score, idx, n_rows, axis):
    rank = jnp.zeros_like(score)
    for jp in range(n_rows):
        other = score[jp:jp + 1, :] if axis == 0 else score[:, jp:jp + 1]
        rank = rank + jnp.where(other > score, 1.0, jnp.where(other == score, jnp.where(idx > jp, 1.0, 0.0), 0.0))
    return jnp.where(rank < float(SLC_TOPN), 1.0, 0.0)


def _forced_score(idx, cur, imp):
    return jnp.where(idx == 0, FORCED_SCORE,
                     jnp.where(idx == cur, FORCED_SCORE,
                               jnp.where(idx == cur - 1, FORCED_SCORE, jnp.where(idx <= cur, imp, -1.0))))


NP_TT = 256


def _nsa_prep_kernel(cq_ref, kv01_ref, kv23_ref, kv45_ref, cos_ref, sin_ref,
                     rows_t_ref, win_t_ref, raw_ref, q_ref, qr_ref, ks_ref, kw_ref):
    cos = cos_ref[...]
    sin = sin_ref[...]
    lane = lax.broadcasted_iota(jnp.int32, cos.shape, 1)
    first = (lane % C_DH) < (C_DH // 2)

    def rope(x):
        return x * cos + jnp.where(first, pltpu.roll(x, 128 - C_DH // 2, 1), pltpu.roll(x, C_DH // 2, 1)) * sin

    def heads(ref, x, h0):
        ref[0, h0] = x[:, 0:C_DH].astype(BF16)
        ref[0, h0 + 1] = x[:, C_DH:2 * C_DH].astype(BF16)

    cq = cq_ref[0]
    for c in range(C_HEADS // 2):
        x = cq[:, c * 128:(c + 1) * 128]
        heads(q_ref, x * ATT_SCALE, 2 * c)
        heads(qr_ref, rope(x * (ATT_SCALE * LOG2_E)), 2 * c)
    kv01 = kv01_ref[0]
    kv23 = kv23_ref[0]
    kv45 = kv45_ref[0]
    ks = rope(kv23[:, 0:128])
    kw = rope(kv45[:, 0:128])
    raw_ref[0] = kv01
    heads(ks_ref, ks, 0)
    heads(kw_ref, kw, 0)
    rows_t_ref[0, 0:128, :] = kv01[:, 0:128].T
    rows_t_ref[0, 128:256, :] = kv01[:, 128:256].T
    rows_t_ref[0, 256:384, :] = ks.T
    rows_t_ref[0, 384:512, :] = kv23[:, 128:256].T
    win_t_ref[0, 0:128, :] = kw.T
    win_t_ref[0, 128:256, :] = kv45[:, 128:256].T


def _nsa_prep(proj, cos128, sin128):
    B, T, _ = proj.shape
    hm = lambda n: pl.BlockSpec((1, n, NP_TT, C_DH), lambda b, t: (b, 0, t, 0))
    hs = lambda n: jax.ShapeDtypeStruct((B, n, T, C_DH), BF16)
    kvb = COL_CKV // 256
    return pl.pallas_call(
        _nsa_prep_kernel,
        grid=(B, T // NP_TT),
        in_specs=[pl.BlockSpec((1, NP_TT, 512), lambda b, t: (b, t, COL_CQ // 512)),
                  pl.BlockSpec((1, NP_TT, 256), lambda b, t: (b, t, kvb)),
                  pl.BlockSpec((1, NP_TT, 256), lambda b, t: (b, t, kvb + 1)),
                  pl.BlockSpec((1, NP_TT, 256), lambda b, t: (b, t, kvb + 2)),
                  pl.BlockSpec((NP_TT, 128), lambda b, t: (t, 0)),
                  pl.BlockSpec((NP_TT, 128), lambda b, t: (t, 0))],
        out_specs=[pl.BlockSpec((1, 512, NP_TT), lambda b, t: (b, 0, t)),
                   pl.BlockSpec((1, 256, NP_TT), lambda b, t: (b, 0, t)),
                   pl.BlockSpec((1, NP_TT, 256), lambda b, t: (b, t, 0)),
                   hm(8), hm(8), hm(2), hm(2)],
        out_shape=[jax.ShapeDtypeStruct((B, 512, T), F32), jax.ShapeDtypeStruct((B, 256, T), F32),
                   jax.ShapeDtypeStruct((B, T, 256), F32), hs(8), hs(8), hs(2), hs(2)],
        compiler_params=_cparams("arbitrary", "arbitrary"),
        name="nsa_prep",
    )(proj, proj, proj, proj, cos128, sin128)


def _compress_prompt_kernel(xk_ref, xv_ref, pos_ref, w1_ref, w2_ref, kc_ref, vct_ref):
    def load_l(l):
        return jnp.concatenate([xk_ref[pl.ds(l, 128, stride=CMP_STRIDE), :],
                                xv_ref[pl.ds(l, 128, stride=CMP_STRIDE), :]], axis=1)

    out = _compress(load_l, pos_ref, w1_ref, w2_ref, 128)
    vt = out[:, 128:256].T
    for g in range(C_KV_HEADS):
        kc_ref[0, g] = out[:, g * C_DH:(g + 1) * C_DH].astype(BF16)
        vct_ref[0, g] = vt[g * C_DH:(g + 1) * C_DH, :].astype(BF16)


def _compress_prompt(raw, pos256, w1bd, w2bd):
    B, T, _ = raw.shape
    return pl.pallas_call(
        _compress_prompt_kernel,
        grid=(B,),
        in_specs=[pl.BlockSpec((None, T, 128), lambda b: (b, 0, 0)),
                  pl.BlockSpec((None, T, 128), lambda b: (b, 0, 1)),
                  pl.BlockSpec((CMP_LEN, 256), lambda b: (0, 0)),
                  pl.BlockSpec((CMP_LEN, 256, 256), lambda b: (0, 0, 0)),
                  pl.BlockSpec((256, 256), lambda b: (0, 0))],
        out_specs=[pl.BlockSpec((1, C_KV_HEADS, 128, C_DH), lambda b: (b, 0, 0, 0)),
                   pl.BlockSpec((1, C_KV_HEADS, C_DH, 128), lambda b: (b, 0, 0, 0))],
        out_shape=[jax.ShapeDtypeStruct((B, C_KV_HEADS, 128, C_DH), BF16),
                   jax.ShapeDtypeStruct((B, C_KV_HEADS, C_DH, 128), BF16)],
        compiler_params=_cparams("arbitrary"),
        name="compress_prompt",
    )(raw, raw, pos256, w1bd, w2bd)


NA_TQ = 256
NA_KT = 256
NA_M = C_GROUP * NA_TQ
FLASH_ACC_ROWS = C_DH + 16


def _lanes4(x):
    return jnp.concatenate([x] * C_GROUP, axis=1)


def _flash_t(qs, k_ref, vt_ref, j_lo, j_hi, bias_fn, s_ref, p_ref, acc_ref):
    groups = range(C_KV_HEADS)
    ones_rows = jnp.ones((FLASH_ACC_ROWS - C_DH, NA_KT), BF16)

    def scores(j, g):
        off = pl.multiple_of(j * NA_KT, NA_KT)
        return lax.dot_general(k_ref[0, g, pl.ds(off, NA_KT), :], qs[g], NT_DIMS, preferred_element_type=F32)

    def values(j, g):
        off = pl.multiple_of(j * NA_KT, NA_KT)
        vt = vt_ref[0, g * C_DH:(g + 1) * C_DH, pl.ds(off, NA_KT)].astype(BF16)
        return jnp.dot(jnp.concatenate([vt, ones_rows], axis=0), p_ref[g], preferred_element_type=F32)

    acc_ref[...] = jnp.zeros_like(acc_ref)
    p_ref[...] = jnp.zeros_like(p_ref)
    for g in groups:
        s_ref[g] = scores(j_lo, g)
    row = lambda v: tuple(jnp.full((1, NA_M), v, F32) for _ in groups)

    def body(j, carry):
        m, alpha = carry
        j_next = jnp.minimum(j + 1, j_hi - 1)
        j_prev = jnp.maximum(j - 1, j_lo)
        m_out, a_out = [], []
        for g in groups:
            s = s_ref[g] + _lanes4(bias_fn(j, g))
            s_ref[g] = scores(j_next, g)
            acc_ref[g] = alpha[g] * acc_ref[g] + values(j_prev, g)
            m_new = jnp.maximum(m[g], jnp.max(s, axis=0, keepdims=True))
            p_ref[g] = jnp.exp2(s - m_new).astype(BF16)
            m_out.append(m_new)
            a_out.append(jnp.exp2(m[g] - m_new))
        return tuple(m_out), tuple(a_out)

    m, alpha = lax.fori_loop(j_lo, j_hi, body, (row(NEG_INF), row(1.0)))
    outs = []
    for g in groups:
        acc = alpha[g] * acc_ref[g] + values(j_hi - 1, g)
        outs.append(acc[0:C_DH] / acc[C_DH:C_DH + 1])
    return outs


def _nsa_prompt_kernel(q_ref, qr_ref, kc_ref, vct_ref, ks_ref, vst_ref, kw_ref, vwt_ref, cg_ref, cz_ref, ovt_ref,
                       yc_ref, s_ref, p_ref, acc_ref, sel_ref):
    qi = pl.program_id(1)
    t0 = qi * NA_TQ
    n_slc = ovt_ref.shape[0]
    sig_t = _sigmoid(cg_ref[0]).T
    cmp_row = lax.broadcasted_iota(jnp.int32, (128, NA_TQ), 0)
    cmp_q = t0 + lax.broadcasted_iota(jnp.int32, (128, NA_TQ), 1)
    cmp_ok = _lanes4(jnp.where(CMP_STRIDE * cmp_row + (CMP_LEN - 1) <= cmp_q, 1.0, 0.0)) > 0.5
    jidx = lax.broadcasted_iota(jnp.int32, (n_slc, NA_TQ), 0)
    cur = (t0 + lax.broadcasted_iota(jnp.int32, (n_slc, NA_TQ), 1)) // SLC_BLK
    qrs, o_cmp = [], []
    for g in range(C_KV_HEADS):
        qg = q_ref[0, C_GROUP * g:C_GROUP * (g + 1)].reshape(NA_M, C_DH)
        qrs.append(qr_ref[0, C_GROUP * g:C_GROUP * (g + 1)].reshape(NA_M, C_DH))
        s = jnp.where(cmp_ok, lax.dot_general(kc_ref[0, g], qg, NT_DIMS, preferred_element_type=F32), NEG_INF)
        e = jnp.where(cmp_ok, jnp.exp(s - jnp.max(s, axis=0, keepdims=True)), 0.0)
        p = e / jnp.maximum(jnp.sum(e, axis=0, keepdims=True), 1e-30)
        o_cmp.append(jnp.dot(vct_ref[0, g], p.astype(BF16), preferred_element_type=F32))
        psum = p[:, 0:NA_TQ] + p[:, NA_TQ:2 * NA_TQ] + p[:, 2 * NA_TQ:3 * NA_TQ] + p[:, 3 * NA_TQ:4 * NA_TQ]
        imp = jnp.dot(ovt_ref[...], psum, precision=lax.Precision.HIGHEST, preferred_element_type=F32)
        sel_t = _rank_select(_forced_score(jidx, cur, imp), jidx, n_slc, 0)
        sel_bias = jnp.where(sel_t > 0.5, 0.0, NEG_INF)
        for jj in range(n_slc):
            sel_ref[g, jj] = sel_bias[jj:jj + 1, :]

    key_row = lax.broadcasted_iota(jnp.int32, (NA_KT, NA_TQ), 0)
    q_pos = t0 + lax.broadcasted_iota(jnp.int32, (NA_KT, NA_TQ), 1)
    blocks_per_tile = NA_KT // SLC_BLK

    def slc_mask(j, g):
        chosen = sel_ref[g, blocks_per_tile * j + blocks_per_tile - 1]
        for i in range(blocks_per_tile - 2, -1, -1):
            chosen = jnp.where(key_row < (i + 1) * SLC_BLK, sel_ref[g, blocks_per_tile * j + i], chosen)
        return jnp.where(j * NA_KT + key_row <= q_pos, chosen, NEG_INF)

    def win_mask(j, g):
        kpos = j * NA_KT + key_row
        return jnp.where(kpos <= q_pos, jnp.where(kpos > q_pos - WINDOW, 0.0, NEG_INF), NEG_INF)

    j_hi = (t0 + NA_TQ + NA_KT - 1) // NA_KT
    o_slc = _flash_t(qrs, ks_ref, vst_ref, 0, j_hi, slc_mask, s_ref, p_ref, acc_ref)
    o_win = _flash_t(qrs, kw_ref, vwt_ref, jnp.maximum(t0 - (WINDOW - 1), 0) // NA_KT, j_hi, win_mask,
                     s_ref, p_ref, acc_ref)
    heads_t = []
    for g in range(C_KV_HEADS):
        for hh in range(C_GROUP):
            hd = C_GROUP * g + hh
            cs = slice(hh * NA_TQ, (hh + 1) * NA_TQ)
            heads_t.append(sig_t[hd:hd + 1, :] * o_cmp[g][:, cs]
                           + sig_t[C_HEADS + hd:C_HEADS + hd + 1, :] * o_slc[g][:, cs]
                           + sig_t[2 * C_HEADS + hd:2 * C_HEADS + hd + 1, :] * o_win[g][:, cs])
    yc_ref[0] = (jnp.concatenate(heads_t, axis=0).T * _silu(cz_ref[0])).astype(BF16)


def _overlap_t(n_slc, n_rows):
    cs = np.arange(128) * CMP_STRIDE
    ss = np.arange(n_rows) * SLC_BLK
    ov = (cs[None, :] < ss[:, None] + SLC_BLK) & (cs[None, :] + CMP_LEN > ss[:, None])
    ov &= (np.arange(128)[None, :] < 127) & (np.arange(n_rows)[:, None] < n_slc)
    return ov.astype(np.float32)


def _nsa_prompt(proj, q_hm, qr_hm, kc, vct, ks_hm, rows_t, kw_hm, win_t):
    B, T, _ = proj.shape
    n_slc = T // SLC_BLK
    ovt = jnp.asarray(_overlap_t(n_slc, n_slc))
    qspec = pl.BlockSpec((1, C_HEADS, NA_TQ, C_DH), lambda b, t: (b, 0, t, 0))
    kspec = pl.BlockSpec((1, C_KV_HEADS, T, C_DH), lambda b, t: (b, 0, 0, 0))
    return pl.pallas_call(
        _nsa_prompt_kernel,
        grid=(B, T // NA_TQ),
        in_specs=[qspec, qspec,
                  pl.BlockSpec((1, C_KV_HEADS, 128, C_DH), lambda b, t: (b, 0, 0, 0)),
                  pl.BlockSpec((1, C_KV_HEADS, C_DH, 128), lambda b, t: (b, 0, 0, 0)),
                  kspec, pl.BlockSpec((1, 128, T), lambda b, t: (b, 3, 0)),
                  kspec, pl.BlockSpec((1, 128, T), lambda b, t: (b, 1, 0)),
                  pl.BlockSpec((1, NA_TQ, 128), lambda b, t: (b, t, COL_CG // 128)),
                  pl.BlockSpec((1, NA_TQ, 512), lambda b, t: (b, t, COL_CZ // 512)),
                  pl.BlockSpec((n_slc, 128), lambda b, t: (0, 0))],
        out_specs=pl.BlockSpec((1, NA_TQ, 512), lambda b, t: (b, t, 0)),
        out_shape=jax.ShapeDtypeStruct((B, T, 512), BF16),
        scratch_shapes=[pltpu.VMEM((C_KV_HEADS, NA_KT, NA_M), F32), pltpu.VMEM((C_KV_HEADS, NA_KT, NA_M), BF16),
                        pltpu.VMEM((C_KV_HEADS, FLASH_ACC_ROWS, NA_M), F32),
                        pltpu.VMEM((C_KV_HEADS, n_slc, 1, NA_TQ), F32)],
        compiler_params=_cparams("arbitrary", "arbitrary"),
        name="nsa_prompt",
    )(q_hm, qr_hm, kc, vct, ks_hm, rows_t, kw_hm, win_t, proj, proj, ovt)


N_PAGES = 16
ND_SB = 2


def _nsa_decode_kernel(pt_ref, *refs):
    npg = ND_SB * N_PAGES
    pages = refs[:npg]
    (q_ref, kv_ref, cg_ref, cz_ref, wb_ref, cos_ref, sin_ref, pos_ref, w1_ref, w2_ref, ov_ref, ex_ref,
     yc_ref, rows_ref, wn_ref, xk_ref, xv_ref) = refs[npg:]
    past = N_PAGES * PAGE_SIZE
    n_slc = -(-(past + 1) // SLC_BLK)
    cur = past // SLC_BLK
    cos = cos_ref[...]
    sin = sin_ref[...]

    def rope(x):
        return x * cos + jnp.concatenate([x[:, C_DH // 2:], x[:, :C_DH // 2]], axis=1) * sin

    def row_dot(a_bf, b_row):
        return jnp.sum(a_bf.astype(F32) * b_row.astype(BF16).astype(F32), axis=-1, keepdims=True)

    row8 = lax.broadcasted_iota(jnp.int32, (C_HEADS, 128), 0)
    lane = lax.broadcasted_iota(jnp.int32, (C_HEADS, 128), 1)
    g0 = row8 < C_GROUP
    g0h = g0[:, 0:C_DH]

    def spread(x):
        return jnp.concatenate([jnp.where(g0h, x, 0.0), jnp.where(g0h, 0.0, x)], axis=1).astype(BF16)

    def gather(x):
        return jnp.where(g0h, x[:, 0:C_DH], x[:, C_DH:2 * C_DH])

    def column_tile(row, reps):
        return jnp.concatenate([jnp.broadcast_to(row, (128, 128)).T] * reps, axis=1)

    rows_per_page = PAGE_SIZE // CMP_STRIDE

    def by_offset(x_t):
        return jnp.swapaxes(x_t.T.reshape(rows_per_page, CMP_STRIDE, 128), 0, 1)

    for i, pg in enumerate(pages):
        xk_ref[:, i * rows_per_page:(i + 1) * rows_per_page, :] = by_offset(pg[0:128, :])
        xv_ref[:, i * rows_per_page:(i + 1) * rows_per_page, :] = by_offset(pg[128:256, :])
    m_rows = ND_SB * 128

    def load_l(l):
        return jnp.concatenate([xk_ref[l], xv_ref[l]], axis=1)

    kcvc_all = _compress(load_l, pos_ref, w1_ref, w2_ref, m_rows).astype(BF16)

    for sb in range(ND_SB):
        spages = pages[sb * N_PAGES:(sb + 1) * N_PAGES]
        kcvc = kcvc_all[sb * 128:(sb + 1) * 128]
        q8 = q_ref[sb] * ATT_SCALE
        kv = kv_ref[sb]
        ks_new = rope(kv[4:6])
        kw_new = rope(kv[8:10])
        rows_ref[sb, 0:4, :] = kv[0:4]
        rows_ref[sb, 4:6, :] = ks_new
        rows_ref[sb, 6:8, :] = kv[6:8]
        qs = spread(q8)
        qrs = spread(rope(q8))

        s = lax.dot_general(qs, kcvc[:, 0:128], NT_DIMS, preferred_element_type=F32)
        ok = (CMP_STRIDE * lane + (CMP_LEN - 1)) <= past
        s = jnp.where(ok, s, NEG_INF)
        e = jnp.where(ok, jnp.exp(s - jnp.max(s, axis=-1, keepdims=True)), 0.0)
        p = e / jnp.maximum(jnp.sum(e, axis=-1, keepdims=True), 1e-30)
        o_cmp = gather(jnp.dot(p.astype(BF16), kcvc[:, 128:256], preferred_element_type=F32))

        ps0 = jnp.sum(jnp.where(g0, p, 0.0), axis=0, keepdims=True)
        ps1 = jnp.sum(jnp.where(g0, 0.0, p), axis=0, keepdims=True)
        imp = jnp.dot(jnp.where(g0, ps0, ps1), ov_ref[...], precision=lax.Precision.HIGHEST,
                      preferred_element_type=F32)
        score = jnp.where(lane < n_slc, _forced_score(lane, cur, imp), -2.0)
        sel = _rank_select(score, lane, n_slc, 1)
        chosen = jnp.dot(sel.astype(BF16), ex_ref[...], preferred_element_type=F32)

        s_pg, ok_pg = [], []
        for i, pg in enumerate(spages):
            ok_pg.append(chosen[:, i * PAGE_SIZE:(i + 1) * PAGE_SIZE] > 0.5)
            sp = jnp.dot(qrs, pg[256:384, :].astype(BF16), preferred_element_type=F32)
            s_pg.append(jnp.where(ok_pg[i], sp, NEG_INF))
        new_ok = sel[:, cur:cur + 1] > 0.5
        s_new = jnp.where(new_ok, row_dot(qrs, jnp.concatenate([ks_new[0:1], ks_new[1:2]], axis=1)), NEG_INF)
        mx = s_new
        for sp in s_pg:
            mx = jnp.maximum(mx, jnp.max(sp, axis=-1, keepdims=True))
        p_new = jnp.where(new_ok, jnp.exp(s_new - mx), 0.0)
        den = p_new
        acc = p_new * jnp.concatenate([kv[6:7], kv[7:8]], axis=1)
        for i, pg in enumerate(spages):
            pp = jnp.where(ok_pg[i], jnp.exp(s_pg[i] - mx), 0.0)
            den = den + jnp.sum(pp, axis=-1, keepdims=True)
            acc = acc + lax.dot_general(pp.astype(BF16), pg[384:512, :].astype(BF16), NT_DIMS,
                                        preferred_element_type=F32)
        o_slc = gather(acc / den)

        wb = wb_ref[sb]
        nw = wb.shape[1]
        sw = jnp.dot(qrs, wb[0:128, :].astype(BF16), preferred_element_type=F32)
        wpos = (past - nw) + lax.broadcasted_iota(jnp.int32, (C_HEADS, nw), 1)
        w_ok = jnp.where(wpos <= past, jnp.where(wpos > past - WINDOW, 1.0, 0.0), 0.0) > 0.5
        sw = jnp.where(w_ok, sw, NEG_INF)
        kw_row = jnp.concatenate([kw_new[0:1], kw_new[1:2]], axis=1)
        vw_row = jnp.concatenate([kv[10:11], kv[11:12]], axis=1)
        sw_new = row_dot(qrs, kw_row)
        mw = jnp.maximum(jnp.max(sw, axis=-1, keepdims=True), sw_new)
        pw = jnp.where(w_ok, jnp.exp(sw - mw), 0.0)
        pw_new = jnp.exp(sw_new - mw)
        accw = lax.dot_general(pw.astype(BF16), wb[128:256, :].astype(BF16), NT_DIMS,
                               preferred_element_type=F32) + pw_new * vw_row
        o_win = gather(accw / (jnp.sum(pw, axis=-1, keepdims=True) + pw_new))

        sig = _sigmoid(cg_ref[sb])
        oc = sig[:, 0:1] * o_cmp + sig[:, 1:2] * o_slc + sig[:, 2:3] * o_win
        yc_ref[sb] = (oc * _silu(cz_ref[sb])).astype(BF16)

        new_cols = jnp.concatenate([column_tile(kw_row, nw // 128), column_tile(vw_row, nw // 128)], axis=0)
        wlane = lax.broadcasted_iota(jnp.int32, wb.shape, 1)
        wn_ref[sb] = jnp.where(wlane == nw - 1, new_cols, pltpu.roll(wb, nw - 1, 1))


def _nsa_decode(cache_t, layer, page_table, win_t, q8, kv12, cg83, cz8, cos64, sin64, pos256, w1bd, w2bd):
    nb = q8.shape[0]
    nw = win_t.shape[3]
    past = N_PAGES * PAGE_SIZE
    n_slc = -(-(past + 1) // SLC_BLK)
    ov = jnp.asarray(_overlap_t(n_slc, 128).T)
    keys = np.arange(past)
    ex = jnp.asarray((np.arange(128)[:, None] == keys[None, :] // SLC_BLK).astype(np.float32)).astype(BF16)

    def page_spec(sb, i):
        return pl.BlockSpec((None, None, 512, PAGE_SIZE), lambda b, pt: (layer, pt[b * ND_SB + sb, i], 0, 0))

    c3 = lambda n, d: pl.BlockSpec((ND_SB, n, d), lambda b, pt: (b, 0, 0))
    z2 = lambda b, pt: (0, 0)
    grid_spec = pltpu.PrefetchScalarGridSpec(
        num_scalar_prefetch=1,
        grid=(nb // ND_SB,),
        in_specs=[page_spec(sb, i) for sb in range(ND_SB) for i in range(N_PAGES)] + [
            c3(C_HEADS, C_DH), c3(12, C_DH), c3(C_HEADS, 3), c3(C_HEADS, C_DH),
            pl.BlockSpec((None, ND_SB, 256, nw), lambda b, pt: (layer, b, 0, 0)),
            pl.BlockSpec((1, C_DH), z2), pl.BlockSpec((1, C_DH), z2),
            pl.BlockSpec((CMP_LEN, 256), z2),
            pl.BlockSpec((CMP_LEN, 256, 256), lambda b, pt: (0, 0, 0)),
            pl.BlockSpec((256, 256), z2),
            pl.BlockSpec((128, 128), z2),
            pl.BlockSpec((128, past), z2)],
        out_specs=[c3(C_HEADS, C_DH), c3(C_HEADS, C_DH), pl.BlockSpec((ND_SB, 256, nw), lambda b, pt: (b, 0, 0))],
        scratch_shapes=[pltpu.VMEM((CMP_STRIDE, ND_SB * 128, 128), F32),
                        pltpu.VMEM((CMP_STRIDE, ND_SB * 128, 128), F32)],
    )
    return pl.pallas_call(
        _nsa_decode_kernel,
        grid_spec=grid_spec,
        out_shape=[jax.ShapeDtypeStruct((nb, C_HEADS, C_DH), BF16),
                   jax.ShapeDtypeStruct((nb, C_HEADS, C_DH), F32),
                   jax.ShapeDtypeStruct((nb, 256, nw), F32)],
        compiler_params=_cparams("arbitrary"),
        name="nsa_decode",
    )(page_table, *([cache_t] * (ND_SB * N_PAGES)), q8, kv12, cg83, cz8, win_t, cos64, sin64, pos256, w1bd, w2bd,
      ov, ex)


def _merge_kernel(ya_ref, yb_ref, yc_ref, mg_ref, x_ref, gate_ref, wb_ref, wo_ref, lg_ref, lb_ref, o_ref):
    merged = None
    for n, y_ref in enumerate((ya_ref, yb_ref, yc_ref)):
        br = jnp.dot(y_ref[0], wb_ref[n], preferred_element_type=F32)
        term = _sigmoid(mg_ref[0, :, n * D_MODEL:(n + 1) * D_MODEL]) * br
        merged = term if merged is None else merged + term
    out = jnp.dot(merged.astype(BF16), wo_ref[...], preferred_element_type=F32)
    z = ALPHA * x_ref[0] + gate_ref[0] * out
    mu = jnp.mean(z, axis=-1, keepdims=True)
    zc = z - mu
    var = jnp.mean(zc * zc, axis=-1, keepdims=True)
    o_ref[0] = zc * lax.rsqrt(var + LN_EPS) * lg_ref[...] + lb_ref[...]


def _merge(ya, yb, yc, proj, x, gate, wb_bf, wo_bf, ln_g, ln_b, tm):
    B, T, _ = x.shape
    ys = pl.BlockSpec((1, tm, 512), lambda b, t: (b, t, 0))
    gs = (pl.BlockSpec((1, 1, D_MODEL), lambda b, t: (b, 0, 0)) if gate.shape[1] == 1
          else pl.BlockSpec((1, tm, D_MODEL), lambda b, t: (b, t, 0)))
    return pl.pallas_call(
        _merge_kernel,
        grid=(B, T // tm),
        in_specs=[ys, ys, ys,
                  pl.BlockSpec((1, tm, 3 * D_MODEL), lambda b, t: (b, t, COL_MG // (3 * D_MODEL))),
                  pl.BlockSpec((1, tm, D_MODEL), lambda b, t: (b, t, 0)),
                  gs,
                  pl.BlockSpec((3, 512, D_MODEL), lambda b, t: (0, 0, 0)),
                  pl.BlockSpec((D_MODEL, D_MODEL), lambda b, t: (0, 0)),
                  pl.BlockSpec((1, D_MODEL), lambda b, t: (0, 0)),
                  pl.BlockSpec((1, D_MODEL), lambda b, t: (0, 0))],
        out_specs=pl.BlockSpec((1, tm, D_MODEL), lambda b, t: (b, t, 0)),
        out_shape=jax.ShapeDtypeStruct((B, T, D_MODEL), F32),
        compiler_params=_cparams("arbitrary", "arbitrary"),
        name="merge",
    )(ya, yb, yc, proj, x, gate, wb_bf, wo_bf, ln_g, ln_b)


def _block_diag(blocks):
    n = len(blocks)
    z = jnp.zeros_like(blocks[0])
    return jnp.concatenate(
        [jnp.concatenate([b if j == i else z for j in range(n)], axis=-1) for i, b in enumerate(blocks)], axis=-2)


def _rope_tables(pos, width):
    inv = ROPE_THETA ** (-jnp.arange(0, C_DH, 2, dtype=F32) / C_DH)
    ang = pos.astype(F32)[:, None] * inv[None, :]
    cos = jnp.cos(ang)
    sin = jnp.sin(ang)
    reps = width // C_DH
    return (jnp.tile(jnp.concatenate([cos, cos], axis=1), (1, reps)),
            jnp.tile(jnp.concatenate([-sin, sin], axis=1), (1, reps)))


def _token_major(x_t, lead):
    kinds = x_t.shape[-2] // (C_KV_HEADS * C_DH)
    x = x_t.reshape(*x_t.shape[:-2], kinds, C_KV_HEADS, C_DH, x_t.shape[-1])
    return jnp.moveaxis(x, -1, lead)


def kernel(x_prompt, x_sample, c_prompt, c_sample, cache_nsa_kv, page_table, state_win_kv, state_hgrn,
           state_rglru, state_conv, ada_w, ada_b, w_in, a_lb, a_norm_g, b_conv_w, b_conv_b, b_gate_a_w,
           b_gate_a_b, b_gate_x_w, b_gate_x_b, b_lambda, c_cmp_pos, c_cmp_w1, c_cmp_w2, w_branch, w_out,
           ln_g, ln_b):
    bp, seq, _ = x_prompt.shape
    bs = x_sample.shape[0]
    n_pages = page_table.shape[1]
    n_pool = cache_nsa_kv.shape[1]
    nw = state_win_kv.shape[2]
    past_len = n_pages * cache_nsa_kv.shape[2]
    assert n_pages == N_PAGES and cache_nsa_kv.shape[2] == PAGE_SIZE and x_sample.shape[1] == 1

    lb_all = jnp.cumsum(jax.nn.softmax(a_lb.astype(F32), axis=0), axis=0)
    lb_all = lb_all - lb_all[0:1]
    loglb = jnp.log(lb_all)
    l1mlb = jnp.log1p(-lb_all)
    w_in_p = jnp.concatenate(
        [w_in[:, :, 4888:7960], w_in[:, :, 0:3584], w_in[:, :, 4376:4888], w_in[:, :, 3584:4352],
         w_in[:, :, 4352:4376], jnp.zeros((DEPTH, D_MODEL, N_PROJ - 7960), w_in.dtype)], axis=2).astype(BF16)
    wb_bf = w_branch.astype(BF16)
    wo_bf = w_out.astype(BF16)
    cache_t = cache_nsa_kv.transpose(0, 1, 3, 4, 5, 2).reshape(DEPTH, n_pool, 512, PAGE_SIZE)
    win_t = state_win_kv.transpose(0, 1, 3, 4, 5, 2).reshape(DEPTH, bs, 256, nw)
    cos_p, sin_p = _rope_tables(jnp.arange(seq), 128)
    cos_s, sin_s = _rope_tables(past_len + jnp.arange(1), C_DH)

    ada = _ada_all(jnp.concatenate([c_prompt, c_sample], axis=0), ada_w, ada_b)

    xp, xs = x_prompt, x_sample.reshape(1, bs, D_MODEL)
    outs = {k: [] for k in ("nsa_p", "nsa_s", "win_p", "win_s", "hg_p", "hg_s", "rg_p", "rg_s", "cv_p", "cv_s")}
    for l in range(DEPTH):
        shift, scale, gate = ada[l, :, 0:D_MODEL], ada[l, :, D_MODEL:2 * D_MODEL], ada[l, :, 2 * D_MODEL:]
        rg_w = (b_conv_w[l], b_conv_b[l].reshape(1, -1),
                _block_diag([b_gate_a_w[l, i] for i in range(B_BLOCKS)]).astype(BF16), b_gate_a_b[l].reshape(1, -1),
                _block_diag([b_gate_x_w[l, i] for i in range(B_BLOCKS)]).astype(BF16), b_gate_x_b[l].reshape(1, -1),
                b_lambda[l].reshape(1, -1))
        pos256 = jnp.concatenate([c_cmp_pos[l, 0], c_cmp_pos[l, 0], c_cmp_pos[l, 1], c_cmp_pos[l, 1]], axis=1)
        w1r = c_cmp_w1[l].reshape(2, CMP_LEN, C_DH, C_DH)
        w1bd = _block_diag([w1r[0], w1r[0], w1r[1], w1r[1]]).astype(BF16)
        w2bd = _block_diag([c_cmp_w2[l, 0], c_cmp_w2[l, 0], c_cmp_w2[l, 1], c_cmp_w2[l, 1]]).astype(BF16)
        gain = a_norm_g[l].reshape(1, -1)
        lg, lbias = ln_g[l].reshape(1, -1), ln_b[l].reshape(1, -1)

        proj = _inproj(xp, scale[:bp, None], shift[:bp, None], w_in_p[l], 1024)
        ya, s_new = _hgrn_prompt(proj, loglb[l].reshape(1, -1), l1mlb[l].reshape(1, -1), gain)
        yb, h_new, cv_new = _rglru_prompt(proj, rg_w)
        rows_t, wrow_t, raw, q_hm, qr_hm, ks_hm, kw_hm = _nsa_prep(proj, cos_p, sin_p)
        kc, vct = _compress_prompt(raw, pos256, w1bd, w2bd)
        yc = _nsa_prompt(proj, q_hm, qr_hm, kc, vct, ks_hm, rows_t, kw_hm, wrow_t)
        xp = _merge(ya, yb, yc, proj, xp, gate[:bp, None], wb_bf[l], wo_bf[l], lg, lbias, 256)
        outs["nsa_p"].append(rows_t)
        outs["win_p"].append(wrow_t[:, :, seq - min(WINDOW, seq):])
        outs["hg_p"].append(s_new)
        outs["rg_p"].append(h_new.reshape(bp, B_WIDTH))
        outs["cv_p"].append(cv_new)

        proj_s = _inproj(xs, scale[None, bp:], shift[None, bp:], w_in_p[l], bs).reshape(bs, N_PROJ)
        s_s, ya_s = _hgrn_decode(state_hgrn, l, proj_s, loglb[l].reshape(A_HEADS, A_DK, 1),
                                 l1mlb[l].reshape(A_HEADS, A_DK, 1), gain)
        yb_s, h_s, cv_s = _rglru_decode(proj_s, state_conv[l].reshape(bs, -1), state_rglru[l], rg_w)
        q8 = proj_s[:, COL_CQ:COL_CQ + 512].reshape(bs, C_HEADS, C_DH)
        cz8 = proj_s[:, COL_CZ:COL_CZ + 512].reshape(bs, C_HEADS, C_DH)
        kv12 = proj_s[:, COL_CKV:COL_CKV + 768].reshape(bs, 12, C_DH)
        cg83 = proj_s[:, COL_CG:COL_CG + 3 * C_HEADS].reshape(bs, 3, C_HEADS).transpose(0, 2, 1)
        yc_s, rows_s, wnew_t = _nsa_decode(cache_t, l, page_table, win_t, q8, kv12, cg83, cz8, cos_s, sin_s,
                                           pos256, w1bd, w2bd)
        xs = _merge(ya_s.reshape(1, bs, 512), yb_s.reshape(1, bs, 512), yc_s.reshape(1, bs, 512),
                    proj_s.reshape(1, bs, N_PROJ), xs, gate[None, bp:], wb_bf[l], wo_bf[l], lg, lbias, bs)
        outs["nsa_s"].append(rows_s.reshape(bs, 1, 4, C_KV_HEADS, C_DH))
        outs["win_s"].append(wnew_t)
        outs["hg_s"].append(s_s)
        outs["rg_s"].append(h_s)
        outs["cv_s"].append(cv_s.reshape(bs, CONV_W - 1, B_WIDTH))

    st = {k: jnp.stack(v, axis=0) for k, v in outs.items()}
    return (xp, xs.reshape(bs, 1, D_MODEL), _token_major(st["nsa_p"], 2), st["nsa_s"],
            _token_major(st["win_p"], 2), _token_major(st["win_s"], 2),
            st["hg_p"], st["hg_s"], st["rg_p"], st["rg_s"], st["cv_p"], st["cv_s"])
```

```python
import numpy as np
import jax
import jax.numpy as jnp
from jax import lax
from jax.experimental import pallas as pl
from jax.experimental.pallas import tpu as pltpu

F32 = jnp.float32
BF16 = jnp.bfloat16

D_MODEL = 1024
DEPTH = 4
PAGE_SIZE = 128
A_HEADS = 4
A_DK = 128
A_CHUNK = 64
B_WIDTH = 512
B_BLOCKS = 8
CONV_W = 4
LRU_C = 8.0
C_HEADS = 8
C_KV_HEADS = 2
C_GROUP = 4
C_DH = 64
CMP_LEN = 32
CMP_STRIDE = 16
SLC_BLK = 64
SLC_TOPN = 16
WINDOW = 512
ROPE_THETA = 10000.0
ATT_SCALE = C_DH ** -0.5
LOG2_E = 1.4426950408889634
ALPHA = (2 * DEPTH) ** 0.25
LN_EPS = 1e-5
NEG_INF = -1e30
FORCED_SCORE = 1e9

COL_MG, COL_AQ, COL_AF, COL_AI, COL_AZ = 0, 3072, 3584, 4096, 4608
COL_BX, COL_BZ, COL_CQ, COL_CZ, COL_CKV, COL_CG = 5120, 5632, 6144, 6656, 7168, 7936
N_PROJ = 8064
PROJ_TN = 2688

VMEM_LIMIT = 56 * 1024 * 1024

NT_DIMS = (((1,), (1,)), ((), ()))
TN_DIMS = (((0,), (0,)), ((), ()))


def _cparams(*sem):
    return pltpu.CompilerParams(dimension_semantics=sem, vmem_limit_bytes=VMEM_LIMIT)


def _sigmoid(x):
    return jax.nn.sigmoid(x)


def _silu(x):
    return x * jax.nn.sigmoid(x)


def _log_f(af, loglb, l1mlb):
    ls = jnp.minimum(af, 0.0) - jnp.log1p(jnp.exp(-jnp.abs(af)))
    y = l1mlb + ls
    return jnp.maximum(loglb, y) + jnp.log1p(jnp.exp(-jnp.abs(loglb - y)))


def _ada_kernel(c_ref, w_ref, b_ref, o_ref):
    o_ref[0] = jnp.dot(c_ref[...], w_ref[0].astype(BF16), preferred_element_type=F32) + b_ref[0]


def _ada_all(c_all, ada_w, ada_b):
    nb = c_all.shape[0]
    return pl.pallas_call(
        _ada_kernel,
        grid=(DEPTH, 3),
        in_specs=[
            pl.BlockSpec((nb, D_MODEL), lambda l, j: (0, 0)),
            pl.BlockSpec((1, D_MODEL, D_MODEL), lambda l, j: (l, 0, j)),
            pl.BlockSpec((1, 1, D_MODEL), lambda l, j: (l, 0, j)),
        ],
        out_specs=pl.BlockSpec((1, nb, D_MODEL), lambda l, j: (l, 0, j)),
        out_shape=jax.ShapeDtypeStruct((DEPTH, nb, 3 * D_MODEL), F32),
        compiler_params=_cparams("arbitrary", "arbitrary"),
        name="ada",
    )(c_all.astype(BF16), ada_w, ada_b.reshape(DEPTH, 1, 3 * D_MODEL))


def _inproj_kernel(x_ref, sc_ref, sh_ref, w_ref, o_ref, h_ref):
    @pl.when(pl.program_id(2) == 0)
    def _():
        h_ref[...] = (x_ref[0] * (1.0 + sc_ref[0]) + sh_ref[0]).astype(BF16)

    o_ref[0] = lax.dot_general(h_ref[...], w_ref[...], NT_DIMS, preferred_element_type=F32)


def _inproj(x, scale, shift, w_bf, tt):
    B, T, _ = x.shape
    ts = scale.shape[1]
    mod_spec = (pl.BlockSpec((1, 1, D_MODEL), lambda b, t, n: (b, 0, 0)) if ts == 1
                else pl.BlockSpec((1, tt, D_MODEL), lambda b, t, n: (b, t, 0)))
    return pl.pallas_call(
        _inproj_kernel,
        grid=(B, T // tt, N_PROJ // PROJ_TN),
        in_specs=[
            pl.BlockSpec((1, tt, D_MODEL), lambda b, t, n: (b, t, 0)),
            mod_spec, mod_spec,
            pl.BlockSpec((PROJ_TN, D_MODEL), lambda b, t, n: (n, 0)),
        ],
        out_specs=pl.BlockSpec((1, tt, PROJ_TN), lambda b, t, n: (b, t, n)),
        out_shape=jax.ShapeDtypeStruct((B, T, N_PROJ), F32),
        scratch_shapes=[pltpu.VMEM((tt, D_MODEL), BF16)],
        compiler_params=_cparams("arbitrary", "arbitrary", "arbitrary"),
        name="inproj",
    )(x, scale, shift, w_bf)


HG_TT = 256
HG_HP = 4
HG_LEVELS = (1, 2, 4, 8, 16, 32)


def _hgrn_level_table():
    t = np.arange(HG_TT)[:, None]
    s = np.arange(HG_TT)[None, :]
    lvl = np.full((HG_TT, HG_TT), -1, np.int32)
    same = (t // A_CHUNK) == (s // A_CHUNK)
    for i, h in enumerate(HG_LEVELS):
        m = same & ((t // h) == (s // h) + 1) & (((s // h) % 2) == 0)
        lvl[m] = i
    lvl[t == s] = len(HG_LEVELS)
    return lvl


def _hgrn_prompt_kernel(aq_ref, af_ref, ai_ref, az_ref, llb_ref, l1_ref, g_ref, lvl_ref,
                        ya_ref, so_ref, s_ref):
    t = pl.program_id(2)

    @pl.when(t == 0)
    def _():
        s_ref[...] = jnp.zeros_like(s_ref)

    for hh in range(HG_HP):
        cs = slice(hh * A_DK, (hh + 1) * A_DK)
        ya, s_new = _hgrn_head(aq_ref[0, :, cs], af_ref[0, :, cs], ai_ref[0, :, cs], az_ref[0, :, cs],
                               llb_ref[:, cs], l1_ref[:, cs], g_ref[:, cs], lvl_ref[...], s_ref[hh])
        ya_ref[0, :, cs] = ya
        s_ref[hh] = s_new

    @pl.when(t == pl.num_programs(2) - 1)
    def _():
        so_ref[0] = s_ref[...]


def _hgrn_head(aq, af, v, az, loglb, l1mlb, gain, lvl, s):
    q = _silu(aq)
    lf = _log_f(af, loglb, l1mlb)
    k = 1.0 - jnp.exp(lf)
    vb = v.astype(BF16)
    row = lax.broadcasted_iota(jnp.int32, (HG_TT, A_DK), 0)

    cin = lf
    rex = jnp.zeros_like(lf)
    att = jnp.where(lvl == len(HG_LEVELS), jnp.sum(q * k, axis=-1, keepdims=True), 0.0)
    for i, h in enumerate(HG_LEVELS):
        qh = (q * jnp.exp(cin)).astype(BF16)
        kh = (k * jnp.exp(rex)).astype(BF16)
        a = lax.dot_general(qh, kh, NT_DIMS, preferred_element_type=F32)
        att = jnp.where(lvl == i, a, att)
        tot = cin + rex
        upper = ((row // h) % 2) == 1
        cin = cin + jnp.where(upper, pltpu.roll(tot, h, 0), 0.0)
        rex = rex + jnp.where(upper, 0.0, pltpu.roll(tot, HG_TT - h, 0))

    o = jnp.dot(att.astype(BF16), vb, preferred_element_type=F32)
    qb = (q * jnp.exp(cin)).astype(BF16)
    kd = (k * jnp.exp(rex)).astype(BF16)
    tot = cin + rex
    inter = []
    for c in range(HG_TT // A_CHUNK):
        sl = slice(c * A_CHUNK, (c + 1) * A_CHUNK)
        inter.append(jnp.dot(qb[sl], s.astype(BF16), preferred_element_type=F32))
        u = lax.dot_general(kd[sl], vb[sl], TN_DIMS, preferred_element_type=F32)
        e_rows = jnp.broadcast_to(jnp.exp(tot[c * A_CHUNK:c * A_CHUNK + 1]), (A_DK, A_DK))
        s = e_rows.T * s + u
    o = o + jnp.concatenate(inter, axis=0)

    ms = jnp.mean(o * o, axis=-1, keepdims=True)
    return ((o * lax.rsqrt(ms + LN_EPS)) * gain * _silu(az)).astype(BF16), s


def _hgrn_prompt(proj, loglb, l1mlb, gain):
    B, T, _ = proj.shape
    lvl = jnp.asarray(_hgrn_level_table())
    width = HG_HP * A_DK

    def col(base):
        return pl.BlockSpec((1, HG_TT, width), lambda b, h, t: (b, t, base // width + h))

    vec = pl.BlockSpec((1, width), lambda b, h, t: (0, h))
    return pl.pallas_call(
        _hgrn_prompt_kernel,
        grid=(B, A_HEADS // HG_HP, T // HG_TT),
        in_specs=[col(COL_AQ), col(COL_AF), col(COL_AI), col(COL_AZ), vec, vec, vec,
                  pl.BlockSpec((HG_TT, HG_TT), lambda b, h, t: (0, 0))],
        out_specs=[pl.BlockSpec((1, HG_TT, width), lambda b, h, t: (b, t, h)),
                   pl.BlockSpec((1, HG_HP, A_DK, A_DK), lambda b, h, t: (b, h, 0, 0))],
        out_shape=[jax.ShapeDtypeStruct((B, T, A_HEADS * A_DK), BF16),
                   jax.ShapeDtypeStruct((B, A_HEADS, A_DK, A_DK), F32)],
        scratch_shapes=[pltpu.VMEM((HG_HP, A_DK, A_DK), F32)],
        compiler_params=_cparams("arbitrary", "arbitrary", "arbitrary"),
        name="hgrn_prompt",
    )(proj, proj, proj, proj, loglb, l1mlb, gain, lvl)


HG_SB = 16


def _hgrn_decode_kernel(s_ref, aqc_ref, afc_ref, ai_ref, az_ref, llb_ref, l1_ref, g_ref, so_ref, ya_ref):
    aqc = aqc_ref[0, 0]
    qc = _silu(aqc)
    lf = _log_f(afc_ref[0, 0], llb_ref[0], l1_ref[0])
    ec = jnp.exp(lf)
    kc = 1.0 - ec
    v = ai_ref[...]
    rows = []
    for r in range(HG_SB):
        sn = ec[:, r:r + 1] * s_ref[r, 0] + kc[:, r:r + 1] * v[r:r + 1, :]
        so_ref[r, 0] = sn
        rows.append(jnp.sum(qc[:, r:r + 1] * sn, axis=0, keepdims=True))
    o = jnp.concatenate(rows, axis=0)
    ms = jnp.mean(o * o, axis=-1, keepdims=True)
    ya_ref[...] = ((o * lax.rsqrt(ms + LN_EPS)) * g_ref[...] * _silu(az_ref[...])).astype(BF16)


def _skip_ref(kernel_fn, pos):
    def wrapped(*refs):
        return kernel_fn(*refs[:pos], *refs[pos + 1:])

    return wrapped


def _hgrn_decode(state_all, layer, proj_s, loglb_c, l1mlb_c, gain, new_state_all):
    nb = proj_s.shape[0]
    nblk = nb // HG_SB

    def cols(base):
        a = proj_s[:, base:base + A_HEADS * A_DK].reshape(nblk, HG_SB, A_HEADS, A_DK)
        return a.transpose(2, 0, 3, 1)

    st_spec = pl.BlockSpec((None, HG_SB, 1, A_DK, A_DK), lambda h, i: (layer, i, h, 0, 0))
    col_spec = pl.BlockSpec((1, 1, A_DK, HG_SB), lambda h, i: (h, i, 0, 0))
    cvec = pl.BlockSpec((1, A_DK, 1), lambda h, i: (h, 0, 0))
    return pl.pallas_call(
        _skip_ref(_hgrn_decode_kernel, 8),
        grid=(A_HEADS, nblk),
        in_specs=[st_spec, col_spec, col_spec,
                  pl.BlockSpec((HG_SB, A_DK), lambda h, i: (i, COL_AI // A_DK + h)),
                  pl.BlockSpec((HG_SB, A_DK), lambda h, i: (i, COL_AZ // A_DK + h)),
                  cvec, cvec,
                  pl.BlockSpec((1, A_DK), lambda h, i: (0, h)),
                  pl.BlockSpec(memory_space=pl.ANY)],
        out_specs=[st_spec, pl.BlockSpec((HG_SB, A_DK), lambda h, i: (i, h))],
        out_shape=[jax.ShapeDtypeStruct(new_state_all.shape, F32),
                   jax.ShapeDtypeStruct((nb, A_HEADS * A_DK), BF16)],
        input_output_aliases={8: 0},
        compiler_params=_cparams("arbitrary", "arbitrary"),
        name="hgrn_decode",
    )(state_all, cols(COL_AQ), cols(COL_AF), proj_s, proj_s, loglb_c, l1mlb_c, gain, new_state_all)


RG_TT = 256


def _rglru_gates(u, ga_ref, gab_ref, gx_ref, gxb_ref, lam_ref):
    ub = u.astype(BF16)
    r = _sigmoid(jnp.dot(ub, ga_ref[...], preferred_element_type=F32) + gab_ref[...])
    ig = _sigmoid(jnp.dot(ub, gx_ref[...], preferred_element_type=F32) + gxb_ref[...])
    nl = -lam_ref[...]
    softplus = jnp.maximum(nl, 0.0) + jnp.log1p(jnp.exp(-jnp.abs(nl)))
    a = jnp.exp(-LRU_C * r * softplus)
    return a, jnp.sqrt(1.0 - a * a) * (ig * u)


def _rglru_prompt_kernel(bx_ref, bz_ref, cw_ref, cb_ref, ga_ref, gab_ref, gx_ref, gxb_ref, lam_ref,
                         yb_ref, h_ref, cv_ref, ext_ref, hc_ref):
    t = pl.program_id(1)

    @pl.when(t == 0)
    def _():
        ext_ref[0:8, :] = jnp.zeros((8, B_WIDTH), F32)
        hc_ref[...] = jnp.zeros_like(hc_ref)

    x = bx_ref[0]
    ext_ref[8:8 + RG_TT, :] = x
    u = cb_ref[...] + (ext_ref[5:5 + RG_TT, :] * cw_ref[0:1, :] + ext_ref[6:6 + RG_TT, :] * cw_ref[1:2, :]
                       + ext_ref[7:7 + RG_TT, :] * cw_ref[2:3, :] + x * cw_ref[3:4, :])
    ext_ref[0:8, :] = x[RG_TT - 8:RG_TT, :]
    a, bt = _rglru_gates(u, ga_ref, gab_ref, gx_ref, gxb_ref, lam_ref)
    row = lax.broadcasted_iota(jnp.int32, (RG_TT, 128), 0)
    z = bz_ref[0]
    for c in range(B_WIDTH // 128):
        cs = slice(c * 128, (c + 1) * 128)
        ac = a[:, cs]
        bc = bt[:, cs] + jnp.where(row == 0, ac * hc_ref[0:1, cs], 0.0)
        s = 1
        while s < RG_TT:
            keep = row >= s
            bc = jnp.where(keep, ac * pltpu.roll(bc, s, 0) + bc, bc)
            ac = jnp.where(keep, ac * pltpu.roll(ac, s, 0), ac)
            s *= 2
        hc_ref[0:1, cs] = bc[RG_TT - 1:RG_TT, :]
        yb_ref[0, :, cs] = (bc * _silu(z[:, cs])).astype(BF16)

    @pl.when(t == pl.num_programs(1) - 1)
    def _():
        h_ref[0] = hc_ref[0:1, :]
        cv_ref[0] = x[RG_TT - (CONV_W - 1):RG_TT, :]


def _rglru_weight_specs(idx):
    return [pl.BlockSpec((CONV_W, B_WIDTH), idx), pl.BlockSpec((1, B_WIDTH), idx),
            pl.BlockSpec((B_WIDTH, B_WIDTH), idx), pl.BlockSpec((1, B_WIDTH), idx),
            pl.BlockSpec((B_WIDTH, B_WIDTH), idx), pl.BlockSpec((1, B_WIDTH), idx),
            pl.BlockSpec((1, B_WIDTH), idx)]


def _rglru_prompt(proj, w):
    B, T, _ = proj.shape
    return pl.pallas_call(
        _rglru_prompt_kernel,
        grid=(B, T // RG_TT),
        in_specs=[pl.BlockSpec((1, RG_TT, B_WIDTH), lambda b, t: (b, t, COL_BX // B_WIDTH)),
                  pl.BlockSpec((1, RG_TT, B_WIDTH), lambda b, t: (b, t, COL_BZ // B_WIDTH))]
        + _rglru_weight_specs(lambda b, t: (0, 0)),
        out_specs=[pl.BlockSpec((1, RG_TT, B_WIDTH), lambda b, t: (b, t, 0)),
                   pl.BlockSpec((1, 1, B_WIDTH), lambda b, t: (b, 0, 0)),
                   pl.BlockSpec((1, CONV_W - 1, B_WIDTH), lambda b, t: (b, 0, 0))],
        out_shape=[jax.ShapeDtypeStruct((B, T, B_WIDTH), BF16),
                   jax.ShapeDtypeStruct((B, 1, B_WIDTH), F32),
                   jax.ShapeDtypeStruct((B, CONV_W - 1, B_WIDTH), F32)],
        scratch_shapes=[pltpu.VMEM((8 + RG_TT, B_WIDTH), F32), pltpu.VMEM((8, B_WIDTH), F32)],
        compiler_params=_cparams("arbitrary", "arbitrary"),
        name="rglru_prompt",
    )(proj, proj, *w)


def _rglru_decode_kernel(bx_ref, bz_ref, cv0_ref, h0_ref, cw_ref, cb_ref, ga_ref, gab_ref, gx_ref, gxb_ref,
                         lam_ref, yb_ref, h_ref, cv_ref):
    x = bx_ref[...]
    c0 = cv0_ref[:, 0:B_WIDTH]
    c1 = cv0_ref[:, B_WIDTH:2 * B_WIDTH]
    c2 = cv0_ref[:, 2 * B_WIDTH:3 * B_WIDTH]
    u = cb_ref[...] + (c0 * cw_ref[0:1, :] + c1 * cw_ref[1:2, :] + c2 * cw_ref[2:3, :] + x * cw_ref[3:4, :])
    a, bt = _rglru_gates(u, ga_ref, gab_ref, gx_ref, gxb_ref, lam_ref)
    h = a * h0_ref[...] + bt
    h_ref[...] = h
    yb_ref[...] = (h * _silu(bz_ref[...])).astype(BF16)
    cv_ref[:, 0:B_WIDTH] = c1
    cv_ref[:, B_WIDTH:2 * B_WIDTH] = c2
    cv_ref[:, 2 * B_WIDTH:3 * B_WIDTH] = x


def _rglru_decode(proj_s, conv0, h0, w):
    nb = proj_s.shape[0]
    z2 = lambda i: (0, 0)
    return pl.pallas_call(
        _rglru_decode_kernel,
        grid=(1,),
        in_specs=[pl.BlockSpec((nb, B_WIDTH), lambda i: (0, COL_BX // B_WIDTH)),
                  pl.BlockSpec((nb, B_WIDTH), lambda i: (0, COL_BZ // B_WIDTH)),
                  pl.BlockSpec((nb, 3 * B_WIDTH), z2), pl.BlockSpec((nb, B_WIDTH), z2)]
        + _rglru_weight_specs(z2),
        out_specs=[pl.BlockSpec((nb, B_WIDTH), z2), pl.BlockSpec((nb, B_WIDTH), z2),
                   pl.BlockSpec((nb, 3 * B_WIDTH), z2)],
        out_shape=[jax.ShapeDtypeStruct((nb, B_WIDTH), BF16), jax.ShapeDtypeStruct((nb, B_WIDTH), F32),
                   jax.ShapeDtypeStruct((nb, 3 * B_WIDTH), F32)],
        compiler_params=_cparams("arbitrary"),
        name="rglru_decode",
    )(proj_s, proj_s, conv0, h0, *w)


def _gelu_tanh(x):
    return 0.5 * x * (1.0 + jnp.tanh(0.7978845608028654 * (x + 0.044715 * (x * x * x))))


def _compress(load_l, pos_ref, w1_ref, w2_ref, m):
    xs = [load_l(l) for l in range(CMP_STRIDE)]
    xa = jnp.concatenate([(x + pos_ref[l:l + 1, :]).astype(BF16) for l, x in enumerate(xs)], axis=1)
    xb = jnp.concatenate([(x + pos_ref[CMP_STRIDE + l:CMP_STRIDE + l + 1, :]).astype(BF16)
                          for l, x in enumerate(xs)], axis=1)
    acc_a = jnp.dot(xa, w1_ref[0], preferred_element_type=F32)
    acc_b = jnp.dot(xb, w1_ref[1], preferred_element_type=F32)
    hid = _gelu_tanh(acc_a + pltpu.roll(acc_b, m - 1, 0))
    return jnp.dot(hid.astype(BF16), w2_ref[...], preferred_element_type=F32)


def _rank_select(score, idx, n_rows, axis):
    rank = jnp.zeros_like(score)
    for jp in range(n_rows):
        other = score[jp:jp + 1, :] if axis == 0 else score[:, jp:jp + 1]
        rank = rank + jnp.where(other > score, 1.0, jnp.where(other == score, jnp.where(idx > jp, 1.0, 0.0), 0.0))
    return jnp.where(rank < float(SLC_TOPN), 1.0, 0.0)


def _forced_score(idx, cur, imp):
    return jnp.where(idx == 0, FORCED_SCORE,
                     jnp.where(idx == cur, FORCED_SCORE,
                               jnp.where(idx == cur - 1, FORCED_SCORE, jnp.where(idx <= cur, imp, -1.0))))


NP_TT = 256


def _nsa_prep_kernel(cq_ref, kv01_ref, kv23_ref, kv45_ref, cos_ref, sin_ref,
                     rows_t_ref, win_t_ref, raw_ref, q_ref, qr_ref, ks_ref, kw_ref):
    cos = cos_ref[...]
    sin = sin_ref[...]
    lane = lax.broadcasted_iota(jnp.int32, cos.shape, 1)
    first = (lane % C_DH) < (C_DH // 2)

    def rope(x):
        return x * cos + jnp.where(first, pltpu.roll(x, 128 - C_DH // 2, 1), pltpu.roll(x, C_DH // 2, 1)) * sin

    def heads(ref, x, h0):
        ref[0, h0] = x[:, 0:C_DH].astype(BF16)
        ref[0, h0 + 1] = x[:, C_DH:2 * C_DH].astype(BF16)

    cq = cq_ref[0]
    for c in range(C_HEADS // 2):
        x = cq[:, c * 128:(c + 1) * 128]
        heads(q_ref, x * ATT_SCALE, 2 * c)
        heads(qr_ref, rope(x * (ATT_SCALE * LOG2_E)), 2 * c)
    kv01 = kv01_ref[0]
    kv23 = kv23_ref[0]
    kv45 = kv45_ref[0]
    ks = rope(kv23[:, 0:128])
    kw = rope(kv45[:, 0:128])
    raw_ref[0] = kv01
    heads(ks_ref, ks, 0)
    heads(kw_ref, kw, 0)
    rows_t_ref[0, 0:128, :] = kv01[:, 0:128].T
    rows_t_ref[0, 128:256, :] = kv01[:, 128:256].T
    rows_t_ref[0, 256:384, :] = ks.T
    rows_t_ref[0, 384:512, :] = kv23[:, 128:256].T
    win_t_ref[0, 0:128, :] = kw.T
    win_t_ref[0, 128:256, :] = kv45[:, 128:256].T


def _nsa_prep(proj, cos128, sin128, rows_all, layer):
    B, T, _ = proj.shape
    hm = lambda n: pl.BlockSpec((1, n, NP_TT, C_DH), lambda b, t: (b, 0, t, 0))
    hs = lambda n: jax.ShapeDtypeStruct((B, n, T, C_DH), BF16)
    kvb = COL_CKV // 256
    return pl.pallas_call(
        _skip_ref(_nsa_prep_kernel, 6),
        grid=(B, T // NP_TT),
        in_specs=[pl.BlockSpec((1, NP_TT, 512), lambda b, t: (b, t, COL_CQ // 512)),
                  pl.BlockSpec((1, NP_TT, 256), lambda b, t: (b, t, kvb)),
                  pl.BlockSpec((1, NP_TT, 256), lambda b, t: (b, t, kvb + 1)),
                  pl.BlockSpec((1, NP_TT, 256), lambda b, t: (b, t, kvb + 2)),
                  pl.BlockSpec((NP_TT, 128), lambda b, t: (t, 0)),
                  pl.BlockSpec((NP_TT, 128), lambda b, t: (t, 0)),
                  pl.BlockSpec(memory_space=pl.ANY)],
        out_specs=[pl.BlockSpec((None, 1, 512, NP_TT), lambda b, t: (layer, b, 0, t)),
                   pl.BlockSpec((1, 256, NP_TT), lambda b, t: (b, 0, t)),
                   pl.BlockSpec((1, NP_TT, 256), lambda b, t: (b, t, 0)),
                   hm(8), hm(8), hm(2), hm(2)],
        out_shape=[jax.ShapeDtypeStruct(rows_all.shape, F32), jax.ShapeDtypeStruct((B, 256, T), F32),
                   jax.ShapeDtypeStruct((B, T, 256), F32), hs(8), hs(8), hs(2), hs(2)],
        input_output_aliases={6: 0},
        compiler_params=_cparams("arbitrary", "arbitrary"),
        name="nsa_prep",
    )(proj, proj, proj, proj, cos128, sin128, rows_all)


def _compress_prompt_kernel(xk_ref, xv_ref, pos_ref, w1_ref, w2_ref, kc_ref, vct_ref):
    def load_l(l):
        return jnp.concatenate([xk_ref[pl.ds(l, 128, stride=CMP_STRIDE), :],
                                xv_ref[pl.ds(l, 128, stride=CMP_STRIDE), :]], axis=1)

    out = _compress(load_l, pos_ref, w1_ref, w2_ref, 128)
    vt = out[:, 128:256].T
    for g in range(C_KV_HEADS):
        kc_ref[0, g] = out[:, g * C_DH:(g + 1) * C_DH].astype(BF16)
        vct_ref[0, g] = vt[g * C_DH:(g + 1) * C_DH, :].astype(BF16)


def _compress_prompt(raw, pos256, w1bd, w2bd):
    B, T, _ = raw.shape
    return pl.pallas_call(
        _compress_prompt_kernel,
        grid=(B,),
        in_specs=[pl.BlockSpec((None, T, 128), lambda b: (b, 0, 0)),
                  pl.BlockSpec((None, T, 128), lambda b: (b, 0, 1)),
                  pl.BlockSpec((CMP_LEN, 256), lambda b: (0, 0)),
                  pl.BlockSpec((2, CMP_STRIDE * 256, 256), lambda b: (0, 0, 0)),
                  pl.BlockSpec((256, 256), lambda b: (0, 0))],
        out_specs=[pl.BlockSpec((1, C_KV_HEADS, 128, C_DH), lambda b: (b, 0, 0, 0)),
                   pl.BlockSpec((1, C_KV_HEADS, C_DH, 128), lambda b: (b, 0, 0, 0))],
        out_shape=[jax.ShapeDtypeStruct((B, C_KV_HEADS, 128, C_DH), BF16),
                   jax.ShapeDtypeStruct((B, C_KV_HEADS, C_DH, 128), BF16)],
        compiler_params=_cparams("arbitrary"),
        name="compress_prompt",
    )(raw, raw, pos256, w1bd, w2bd)


NA_TQ = 256
NA_KT = 256
NA_M = C_GROUP * NA_TQ
FLASH_ACC_ROWS = C_DH + 16


def _lanes4(x):
    return jnp.concatenate([x] * C_GROUP, axis=1)


def _flash_t(qs, k_ref, vt_ref, j_lo, j_hi, bias_fn, s_ref, p_ref, acc_ref):
    groups = range(C_KV_HEADS)
    ones_rows = jnp.ones((FLASH_ACC_ROWS - C_DH, NA_KT), BF16)

    def scores(j, g):
        off = pl.multiple_of(j * NA_KT, NA_KT)
        return lax.dot_general(k_ref[0, g, pl.ds(off, NA_KT), :], qs[g], NT_DIMS, preferred_element_type=F32)

    def values(j, g):
        off = pl.multiple_of(j * NA_KT, NA_KT)
        vt = vt_ref[0, g * C_DH:(g + 1) * C_DH, pl.ds(off, NA_KT)].astype(BF16)
        return jnp.dot(jnp.concatenate([vt, ones_rows], axis=0), p_ref[g], preferred_element_type=F32)

    acc_ref[...] = jnp.zeros_like(acc_ref)
    p_ref[...] = jnp.zeros_like(p_ref)
    for g in groups:
        s_ref[g] = scores(j_lo, g)
    row = lambda v: tuple(jnp.full((1, NA_M), v, F32) for _ in groups)

    def body(j, carry):
        m, alpha = carry
        j_next = jnp.minimum(j + 1, j_hi - 1)
        j_prev = jnp.maximum(j - 1, j_lo)
        m_out, a_out = [], []
        for g in groups:
            s = s_ref[g] + _lanes4(bias_fn(j, g))
            s_ref[g] = scores(j_next, g)
            acc_ref[g] = alpha[g] * acc_ref[g] + values(j_prev, g)
            m_new = jnp.maximum(m[g], jnp.max(s, axis=0, keepdims=True))
            p_ref[g] = jnp.exp2(s - m_new).astype(BF16)
            m_out.append(m_new)
            a_out.append(jnp.exp2(m[g] - m_new))
        return tuple(m_out), tuple(a_out)

    m, alpha = lax.fori_loop(j_lo, j_hi, body, (row(NEG_INF), row(1.0)))
    outs = []
    for g in groups:
        acc = alpha[g] * acc_ref[g] + values(j_hi - 1, g)
        outs.append(acc[0:C_DH] / acc[C_DH:C_DH + 1])
    return outs


def _nsa_prompt_kernel(q_ref, qr_ref, kc_ref, vct_ref, ks_ref, vst_ref, kw_ref, vwt_ref, cg_ref, cz_ref, ovt_ref,
                       yc_ref, s_ref, p_ref, acc_ref, sel_ref):
    qi = pl.program_id(1)
    t0 = qi * NA_TQ
    n_slc = ovt_ref.shape[0]
    sig_t = _sigmoid(cg_ref[0]).T
    cmp_row = lax.broadcasted_iota(jnp.int32, (128, NA_TQ), 0)
    cmp_q = t0 + lax.broadcasted_iota(jnp.int32, (128, NA_TQ), 1)
    cmp_ok = _lanes4(jnp.where(CMP_STRIDE * cmp_row + (CMP_LEN - 1) <= cmp_q, 1.0, 0.0)) > 0.5
    jidx = lax.broadcasted_iota(jnp.int32, (n_slc, NA_TQ), 0)
    cur = (t0 + lax.broadcasted_iota(jnp.int32, (n_slc, NA_TQ), 1)) // SLC_BLK
    qrs, o_cmp = [], []
    for g in range(C_KV_HEADS):
        qg = q_ref[0, C_GROUP * g:C_GROUP * (g + 1)].reshape(NA_M, C_DH)
        qrs.append(qr_ref[0, C_GROUP * g:C_GROUP * (g + 1)].reshape(NA_M, C_DH))
        s = jnp.where(cmp_ok, lax.dot_general(kc_ref[0, g], qg, NT_DIMS, preferred_element_type=F32), NEG_INF)
        e = jnp.where(cmp_ok, jnp.exp(s - jnp.max(s, axis=0, keepdims=True)), 0.0)
        p = e / jnp.maximum(jnp.sum(e, axis=0, keepdims=True), 1e-30)
        o_cmp.append(jnp.dot(vct_ref[0, g], p.astype(BF16), preferred_element_type=F32))
        psum = p[:, 0:NA_TQ] + p[:, NA_TQ:2 * NA_TQ] + p[:, 2 * NA_TQ:3 * NA_TQ] + p[:, 3 * NA_TQ:4 * NA_TQ]
        imp = jnp.dot(ovt_ref[...], psum, precision=lax.Precision.HIGHEST, preferred_element_type=F32)
        sel_t = _rank_select(_forced_score(jidx, cur, imp), jidx, n_slc, 0)
        sel_bias = jnp.where(sel_t > 0.5, 0.0, NEG_INF)
        for jj in range(n_slc):
            sel_ref[g, jj] = sel_bias[jj:jj + 1, :]

    key_row = lax.broadcasted_iota(jnp.int32, (NA_KT, NA_TQ), 0)
    q_pos = t0 + lax.broadcasted_iota(jnp.int32, (NA_KT, NA_TQ), 1)
    blocks_per_tile = NA_KT // SLC_BLK

    def slc_mask(j, g):
        chosen = sel_ref[g, blocks_per_tile * j + blocks_per_tile - 1]
        for i in range(blocks_per_tile - 2, -1, -1):
            chosen = jnp.where(key_row < (i + 1) * SLC_BLK, sel_ref[g, blocks_per_tile * j + i], chosen)
        return jnp.where(j * NA_KT + key_row <= q_pos, chosen, NEG_INF)

    def win_mask(j, g):
        kpos = j * NA_KT + key_row
        return jnp.where(kpos <= q_pos, jnp.where(kpos > q_pos - WINDOW, 0.0, NEG_INF), NEG_INF)

    j_hi = (t0 + NA_TQ + NA_KT - 1) // NA_KT
    o_slc = _flash_t(qrs, ks_ref, vst_ref, 0, j_hi, slc_mask, s_ref, p_ref, acc_ref)
    o_win = _flash_t(qrs, kw_ref, vwt_ref, jnp.maximum(t0 - (WINDOW - 1), 0) // NA_KT, j_hi, win_mask,
                     s_ref, p_ref, acc_ref)
    heads_t = []
    for g in range(C_KV_HEADS):
        for hh in range(C_GROUP):
            hd = C_GROUP * g + hh
            cs = slice(hh * NA_TQ, (hh + 1) * NA_TQ)
            heads_t.append(sig_t[hd:hd + 1, :] * o_cmp[g][:, cs]
                           + sig_t[C_HEADS + hd:C_HEADS + hd + 1, :] * o_slc[g][:, cs]
                           + sig_t[2 * C_HEADS + hd:2 * C_HEADS + hd + 1, :] * o_win[g][:, cs])
    yc_ref[0] = (jnp.concatenate(heads_t, axis=0).T * _silu(cz_ref[0])).astype(BF16)


def _overlap_t(n_slc, n_rows):
    cs = np.arange(128) * CMP_STRIDE
    ss = np.arange(n_rows) * SLC_BLK
    ov = (cs[None, :] < ss[:, None] + SLC_BLK) & (cs[None, :] + CMP_LEN > ss[:, None])
    ov &= (np.arange(128)[None, :] < 127) & (np.arange(n_rows)[:, None] < n_slc)
    return ov.astype(np.float32)


def _nsa_prompt(proj, q_hm, qr_hm, kc, vct, ks_hm, rows_all, layer, kw_hm, win_t):
    B, T, _ = proj.shape
    n_slc = T // SLC_BLK
    ovt = jnp.asarray(_overlap_t(n_slc, n_slc))
    qspec = pl.BlockSpec((1, C_HEADS, NA_TQ, C_DH), lambda b, t: (b, 0, t, 0))
    kspec = pl.BlockSpec((1, C_KV_HEADS, T, C_DH), lambda b, t: (b, 0, 0, 0))
    return pl.pallas_call(
        _nsa_prompt_kernel,
        grid=(B, T // NA_TQ),
        in_specs=[qspec, qspec,
                  pl.BlockSpec((1, C_KV_HEADS, 128, C_DH), lambda b, t: (b, 0, 0, 0)),
                  pl.BlockSpec((1, C_KV_HEADS, C_DH, 128), lambda b, t: (b, 0, 0, 0)),
                  kspec, pl.BlockSpec((None, 1, 128, T), lambda b, t: (layer, b, 3, 0)),
                  kspec, pl.BlockSpec((1, 128, T), lambda b, t: (b, 1, 0)),
                  pl.BlockSpec((1, NA_TQ, 128), lambda b, t: (b, t, COL_CG // 128)),
                  pl.BlockSpec((1, NA_TQ, 512), lambda b, t: (b, t, COL_CZ // 512)),
                  pl.BlockSpec((n_slc, 128), lambda b, t: (0, 0))],
        out_specs=pl.BlockSpec((1, NA_TQ, 512), lambda b, t: (b, t, 0)),
        out_shape=jax.ShapeDtypeStruct((B, T, 512), BF16),
        scratch_shapes=[pltpu.VMEM((C_KV_HEADS, NA_KT, NA_M), F32), pltpu.VMEM((C_KV_HEADS, NA_KT, NA_M), BF16),
                        pltpu.VMEM((C_KV_HEADS, FLASH_ACC_ROWS, NA_M), F32),
                        pltpu.VMEM((C_KV_HEADS, n_slc, 1, NA_TQ), F32)],
        compiler_params=_cparams("arbitrary", "arbitrary"),
        name="nsa_prompt",
    )(q_hm, qr_hm, kc, vct, ks_hm, rows_all, kw_hm, win_t, proj, proj, ovt)


N_PAGES = 16
ND_SB = 2


def _nsa_decode_kernel(pt_ref, *refs):
    npg = ND_SB * N_PAGES
    pages = refs[:npg]
    (q_ref, kv_ref, cg_ref, cz_ref, wb_ref, cos_ref, sin_ref, pos_ref, w1_ref, w2_ref, ov_ref, ex_ref, _,
     yc_ref, rows_ref, wn_ref, xk_ref, xv_ref) = refs[npg:]
    past = N_PAGES * PAGE_SIZE
    n_slc = -(-(past + 1) // SLC_BLK)
    cur = past // SLC_BLK
    cos = cos_ref[...]
    sin = sin_ref[...]

    def rope(x):
        return x * cos + jnp.concatenate([x[:, C_DH // 2:], x[:, :C_DH // 2]], axis=1) * sin

    def row_dot(a_bf, b_row):
        return jnp.sum(a_bf.astype(F32) * b_row.astype(BF16).astype(F32), axis=-1, keepdims=True)

    row8 = lax.broadcasted_iota(jnp.int32, (C_HEADS, 128), 0)
    lane = lax.broadcasted_iota(jnp.int32, (C_HEADS, 128), 1)
    g0 = row8 < C_GROUP
    g0h = g0[:, 0:C_DH]

    def spread(x):
        return jnp.concatenate([jnp.where(g0h, x, 0.0), jnp.where(g0h, 0.0, x)], axis=1).astype(BF16)

    def gather(x):
        return jnp.where(g0h, x[:, 0:C_DH], x[:, C_DH:2 * C_DH])

    def column_tile(row, reps):
        return jnp.concatenate([jnp.broadcast_to(row, (128, 128)).T] * reps, axis=1)

    rows_per_page = PAGE_SIZE // CMP_STRIDE

    def by_offset(x_t):
        return jnp.swapaxes(x_t.T.reshape(rows_per_page, CMP_STRIDE, 128), 0, 1)

    for i, pg in enumerate(pages):
        xk_ref[:, i * rows_per_page:(i + 1) * rows_per_page, :] = by_offset(pg[0:128, :])
        xv_ref[:, i * rows_per_page:(i + 1) * rows_per_page, :] = by_offset(pg[128:256, :])
    m_rows = ND_SB * 128

    def load_l(l):
        return jnp.concatenate([xk_ref[l], xv_ref[l]], axis=1)

    kcvc_all = _compress(load_l, pos_ref, w1_ref, w2_ref, m_rows).astype(BF16)

    for sb in range(ND_SB):
        spages = pages[sb * N_PAGES:(sb + 1) * N_PAGES]
        kcvc = kcvc_all[sb * 128:(sb + 1) * 128]
        q8 = q_ref[sb] * ATT_SCALE
        kv = kv_ref[sb]
        ks_new = rope(kv[4:6])
        kw_new = rope(kv[8:10])
        rows_ref[sb, 0:4, :] = kv[0:4]
        rows_ref[sb, 4:6, :] = ks_new
        rows_ref[sb, 6:8, :] = kv[6:8]
        qs = spread(q8)
        qrs = spread(rope(q8))

        s = lax.dot_general(qs, kcvc[:, 0:128], NT_DIMS, preferred_element_type=F32)
        ok = (CMP_STRIDE * lane + (CMP_LEN - 1)) <= past
        s = jnp.where(ok, s, NEG_INF)
        e = jnp.where(ok, jnp.exp(s - jnp.max(s, axis=-1, keepdims=True)), 0.0)
        p = e / jnp.maximum(jnp.sum(e, axis=-1, keepdims=True), 1e-30)
        o_cmp = gather(jnp.dot(p.astype(BF16), kcvc[:, 128:256], preferred_element_type=F32))

        ps0 = jnp.sum(jnp.where(g0, p, 0.0), axis=0, keepdims=True)
        ps1 = jnp.sum(jnp.where(g0, 0.0, p), axis=0, keepdims=True)
        imp = jnp.dot(jnp.where(g0, ps0, ps1), ov_ref[...], precision=lax.Precision.HIGHEST,
                      preferred_element_type=F32)
        score = jnp.where(lane < n_slc, _forced_score(lane, cur, imp), -2.0)
        sel = _rank_select(score, lane, n_slc, 1)
        chosen = jnp.dot(sel.astype(BF16), ex_ref[...], preferred_element_type=F32)

        s_pg, ok_pg = [], []
        for i, pg in enumerate(spages):
            ok_pg.append(chosen[:, i * PAGE_SIZE:(i + 1) * PAGE_SIZE] > 0.5)
            sp = jnp.dot(qrs, pg[256:384, :].astype(BF16), preferred_element_type=F32)
            s_pg.append(jnp.where(ok_pg[i], sp, NEG_INF))
        new_ok = sel[:, cur:cur + 1] > 0.5
        s_new = jnp.where(new_ok, row_dot(qrs, jnp.concatenate([ks_new[0:1], ks_new[1:2]], axis=1)), NEG_INF)
        mx = s_new
        for sp in s_pg:
            mx = jnp.maximum(mx, jnp.max(sp, axis=-1, keepdims=True))
        p_new = jnp.where(new_ok, jnp.exp(s_new - mx), 0.0)
        den = p_new
        acc = p_new * jnp.concatenate([kv[6:7], kv[7:8]], axis=1)
        for i, pg in enumerate(spages):
            pp = jnp.where(ok_pg[i], jnp.exp(s_pg[i] - mx), 0.0)
            den = den + jnp.sum(pp, axis=-1, keepdims=True)
            acc = acc + lax.dot_general(pp.astype(BF16), pg[384:512, :].astype(BF16), NT_DIMS,
                                        preferred_element_type=F32)
        o_slc = gather(acc / den)

        wb = wb_ref[sb]
        nw = wb.shape[1]
        sw = jnp.dot(qrs, wb[0:128, :].astype(BF16), preferred_element_type=F32)
        wpos = (past - nw) + lax.broadcasted_iota(jnp.int32, (C_HEADS, nw), 1)
        w_ok = jnp.where(wpos <= past, jnp.where(wpos > past - WINDOW, 1.0, 0.0), 0.0) > 0.5
        sw = jnp.where(w_ok, sw, NEG_INF)
        kw_row = jnp.concatenate([kw_new[0:1], kw_new[1:2]], axis=1)
        vw_row = jnp.concatenate([kv[10:11], kv[11:12]], axis=1)
        sw_new = row_dot(qrs, kw_row)
        mw = jnp.maximum(jnp.max(sw, axis=-1, keepdims=True), sw_new)
        pw = jnp.where(w_ok, jnp.exp(sw - mw), 0.0)
        pw_new = jnp.exp(sw_new - mw)
        accw = lax.dot_general(pw.astype(BF16), wb[128:256, :].astype(BF16), NT_DIMS,
                               preferred_element_type=F32) + pw_new * vw_row
        o_win = gather(accw / (jnp.sum(pw, axis=-1, keepdims=True) + pw_new))

        sig = _sigmoid(cg_ref[sb])
        oc = sig[:, 0:1] * o_cmp + sig[:, 1:2] * o_slc + sig[:, 2:3] * o_win
        yc_ref[sb] = (oc * _silu(cz_ref[sb])).astype(BF16)

        new_cols = jnp.concatenate([column_tile(kw_row, nw // 128), column_tile(vw_row, nw // 128)], axis=0)
        wlane = lax.broadcasted_iota(jnp.int32, wb.shape, 1)
        wn_ref[sb] = jnp.where(wlane == nw - 1, new_cols, pltpu.roll(wb, nw - 1, 1))


def _nsa_decode(cache_t, layer, page_table, win_t, q8, kv12, cg83, cz8, cos64, sin64, pos256, w1bd, w2bd,
                new_win_t):
    nb = q8.shape[0]
    nw = win_t.shape[3]
    past = N_PAGES * PAGE_SIZE
    n_slc = -(-(past + 1) // SLC_BLK)
    ov = jnp.asarray(_overlap_t(n_slc, 128).T)
    keys = np.arange(past)
    ex = jnp.asarray((np.arange(128)[:, None] == keys[None, :] // SLC_BLK).astype(np.float32)).astype(BF16)

    def page_spec(sb, i):
        return pl.BlockSpec((None, None, 512, PAGE_SIZE), lambda b, pt: (layer, pt[b * ND_SB + sb, i], 0, 0))

    c3 = lambda n, d: pl.BlockSpec((ND_SB, n, d), lambda b, pt: (b, 0, 0))
    z2 = lambda b, pt: (0, 0)
    grid_spec = pltpu.PrefetchScalarGridSpec(
        num_scalar_prefetch=1,
        grid=(nb // ND_SB,),
        in_specs=[page_spec(sb, i) for sb in range(ND_SB) for i in range(N_PAGES)] + [
            c3(C_HEADS, C_DH), c3(12, C_DH), c3(C_HEADS, 3), c3(C_HEADS, C_DH),
            pl.BlockSpec((None, ND_SB, 256, nw), lambda b, pt: (layer, b, 0, 0)),
            pl.BlockSpec((1, C_DH), z2), pl.BlockSpec((1, C_DH), z2),
            pl.BlockSpec((CMP_LEN, 256), z2),
            pl.BlockSpec((2, CMP_STRIDE * 256, 256), lambda b, pt: (0, 0, 0)),
            pl.BlockSpec((256, 256), z2),
            pl.BlockSpec((128, 128), z2),
            pl.BlockSpec((128, past), z2),
            pl.BlockSpec(memory_space=pl.ANY)],
        out_specs=[c3(C_HEADS, C_DH), c3(C_HEADS, C_DH),
                   pl.BlockSpec((None, ND_SB, 256, nw), lambda b, pt: (layer, b, 0, 0))],
        scratch_shapes=[pltpu.VMEM((CMP_STRIDE, ND_SB * 128, 128), F32),
                        pltpu.VMEM((CMP_STRIDE, ND_SB * 128, 128), F32)],
    )
    return pl.pallas_call(
        _nsa_decode_kernel,
        grid_spec=grid_spec,
        out_shape=[jax.ShapeDtypeStruct((nb, C_HEADS, C_DH), BF16),
                   jax.ShapeDtypeStruct((nb, C_HEADS, C_DH), F32),
                   jax.ShapeDtypeStruct(new_win_t.shape, F32)],
        input_output_aliases={1 + ND_SB * N_PAGES + 12: 2},
        compiler_params=_cparams("arbitrary"),
        name="nsa_decode",
    )(page_table, *([cache_t] * (ND_SB * N_PAGES)), q8, kv12, cg83, cz8, win_t, cos64, sin64, pos256, w1bd, w2bd,
      ov, ex, new_win_t)


def _merge_kernel(ya_ref, yb_ref, yc_ref, mg_ref, x_ref, gate_ref, wb_ref, wo_ref, lg_ref, lb_ref, o_ref):
    merged = None
    for n, y_ref in enumerate((ya_ref, yb_ref, yc_ref)):
        br = jnp.dot(y_ref[0], wb_ref[n], preferred_element_type=F32)
        term = _sigmoid(mg_ref[0, :, n * D_MODEL:(n + 1) * D_MODEL]) * br
        merged = term if merged is None else merged + term
    out = jnp.dot(merged.astype(BF16), wo_ref[...], preferred_element_type=F32)
    z = ALPHA * x_ref[0] + gate_ref[0] * out
    mu = jnp.mean(z, axis=-1, keepdims=True)
    zc = z - mu
    var = jnp.mean(zc * zc, axis=-1, keepdims=True)
    o_ref[0] = zc * lax.rsqrt(var + LN_EPS) * lg_ref[...] + lb_ref[...]


def _merge(ya, yb, yc, proj, x, gate, wb_bf, wo_bf, ln_g, ln_b, tm):
    B, T, _ = x.shape
    ys = pl.BlockSpec((1, tm, 512), lambda b, t: (b, t, 0))
    gs = (pl.BlockSpec((1, 1, D_MODEL), lambda b, t: (b, 0, 0)) if gate.shape[1] == 1
          else pl.BlockSpec((1, tm, D_MODEL), lambda b, t: (b, t, 0)))
    return pl.pallas_call(
        _merge_kernel,
        grid=(B, T // tm),
        in_specs=[ys, ys, ys,
                  pl.BlockSpec((1, tm, 3 * D_MODEL), lambda b, t: (b, t, COL_MG // (3 * D_MODEL))),
                  pl.BlockSpec((1, tm, D_MODEL), lambda b, t: (b, t, 0)),
                  gs,
                  pl.BlockSpec((3, 512, D_MODEL), lambda b, t: (0, 0, 0)),
                  pl.BlockSpec((D_MODEL, D_MODEL), lambda b, t: (0, 0)),
                  pl.BlockSpec((1, D_MODEL), lambda b, t: (0, 0)),
                  pl.BlockSpec((1, D_MODEL), lambda b, t: (0, 0))],
        out_specs=pl.BlockSpec((1, tm, D_MODEL), lambda b, t: (b, t, 0)),
        out_shape=jax.ShapeDtypeStruct((B, T, D_MODEL), F32),
        compiler_params=_cparams("arbitrary", "arbitrary"),
        name="merge",
    )(ya, yb, yc, proj, x, gate, wb_bf, wo_bf, ln_g, ln_b)


def _block_diag(blocks):
    n = len(blocks)
    z = jnp.zeros_like(blocks[0])
    return jnp.concatenate(
        [jnp.concatenate([b if j == i else z for j in range(n)], axis=-1) for i, b in enumerate(blocks)], axis=-2)


def _rope_tables(pos, width):
    inv = ROPE_THETA ** (-jnp.arange(0, C_DH, 2, dtype=F32) / C_DH)
    ang = pos.astype(F32)[:, None] * inv[None, :]
    cos = jnp.cos(ang)
    sin = jnp.sin(ang)
    reps = width // C_DH
    return (jnp.tile(jnp.concatenate([cos, cos], axis=1), (1, reps)),
            jnp.tile(jnp.concatenate([-sin, sin], axis=1), (1, reps)))


def _token_major(x_t, lead):
    kinds = x_t.shape[-2] // (C_KV_HEADS * C_DH)
    x = x_t.reshape(*x_t.shape[:-2], kinds, C_KV_HEADS, C_DH, x_t.shape[-1])
    return jnp.moveaxis(x, -1, lead)


def kernel(x_prompt, x_sample, c_prompt, c_sample, cache_nsa_kv, page_table, state_win_kv, state_hgrn,
           state_rglru, state_conv, ada_w, ada_b, w_in, a_lb, a_norm_g, b_conv_w, b_conv_b, b_gate_a_w,
           b_gate_a_b, b_gate_x_w, b_gate_x_b, b_lambda, c_cmp_pos, c_cmp_w1, c_cmp_w2, w_branch, w_out,
           ln_g, ln_b):
    bp, seq, _ = x_prompt.shape
    bs = x_sample.shape[0]
    n_pages = page_table.shape[1]
    n_pool = cache_nsa_kv.shape[1]
    nw = state_win_kv.shape[2]
    past_len = n_pages * cache_nsa_kv.shape[2]
    assert n_pages == N_PAGES and cache_nsa_kv.shape[2] == PAGE_SIZE and x_sample.shape[1] == 1

    lb_all = jnp.cumsum(jax.nn.softmax(a_lb.astype(F32), axis=0), axis=0)
    lb_all = lb_all - lb_all[0:1]
    loglb = jnp.log(lb_all)
    l1mlb = jnp.log1p(-lb_all)
    w_in_t = jnp.swapaxes(w_in, 1, 2).astype(BF16)
    w_in_p = jnp.concatenate(
        [w_in_t[:, 4888:7960], w_in_t[:, 0:3584], w_in_t[:, 4376:4888], w_in_t[:, 3584:4352],
         w_in_t[:, 4352:4376], jnp.zeros((DEPTH, N_PROJ - 7960, D_MODEL), BF16)], axis=1)
    wb_bf = w_branch.astype(BF16)
    wo_bf = w_out.astype(BF16)
    cache_t = cache_nsa_kv.transpose(0, 1, 3, 4, 5, 2).reshape(DEPTH, n_pool, 512, PAGE_SIZE)
    win_t = state_win_kv.transpose(0, 1, 3, 4, 5, 2).reshape(DEPTH, bs, 256, nw)
    cos_p, sin_p = _rope_tables(jnp.arange(seq), 128)
    cos_s, sin_s = _rope_tables(past_len + jnp.arange(1), C_DH)

    ada = _ada_all(jnp.concatenate([c_prompt, c_sample], axis=0), ada_w, ada_b)

    xp, xs = x_prompt, x_sample.reshape(1, bs, D_MODEL)
    outs = {k: [] for k in ("nsa_s", "win_p", "hg_p", "rg_p", "rg_s", "cv_p", "cv_s")}
    nsa_p_all = jnp.zeros((DEPTH, bp, 512, seq), F32)
    win_s_all = jnp.zeros((DEPTH, bs, 256, nw), F32)
    hg_s_all = jnp.zeros(state_hgrn.shape, F32)
    for l in range(DEPTH):
        shift, scale, gate = ada[l, :, 0:D_MODEL], ada[l, :, D_MODEL:2 * D_MODEL], ada[l, :, 2 * D_MODEL:]
        rg_w = (b_conv_w[l], b_conv_b[l].reshape(1, -1),
                _block_diag([b_gate_a_w[l, i] for i in range(B_BLOCKS)]).astype(BF16), b_gate_a_b[l].reshape(1, -1),
                _block_diag([b_gate_x_w[l, i] for i in range(B_BLOCKS)]).astype(BF16), b_gate_x_b[l].reshape(1, -1),
                b_lambda[l].reshape(1, -1))
        pos256 = jnp.concatenate([c_cmp_pos[l, 0], c_cmp_pos[l, 0], c_cmp_pos[l, 1], c_cmp_pos[l, 1]], axis=1)
        w1r = c_cmp_w1[l].reshape(2, CMP_LEN, C_DH, C_DH)
        w1bd = _block_diag([w1r[0], w1r[0], w1r[1], w1r[1]]).astype(BF16).reshape(2, CMP_STRIDE * 256, 256)
        w2bd = _block_diag([c_cmp_w2[l, 0], c_cmp_w2[l, 0], c_cmp_w2[l, 1], c_cmp_w2[l, 1]]).astype(BF16)
        gain = a_norm_g[l].reshape(1, -1)
        lg, lbias = ln_g[l].reshape(1, -1), ln_b[l].reshape(1, -1)

        proj = _inproj(xp, scale[:bp, None], shift[:bp, None], w_in_p[l], 1024)
        ya, s_new = _hgrn_prompt(proj, loglb[l].reshape(1, -1), l1mlb[l].reshape(1, -1), gain)
        yb, h_new, cv_new = _rglru_prompt(proj, rg_w)
        nsa_p_all, wrow_t, raw, q_hm, qr_hm, ks_hm, kw_hm = _nsa_prep(proj, cos_p, sin_p, nsa_p_all, l)
        kc, vct = _compress_prompt(raw, pos256, w1bd, w2bd)
        yc = _nsa_prompt(proj, q_hm, qr_hm, kc, vct, ks_hm, nsa_p_all, l, kw_hm, wrow_t)
        xp = _merge(ya, yb, yc, proj, xp, gate[:bp, None], wb_bf[l], wo_bf[l], lg, lbias, 256)
        outs["win_p"].append(wrow_t[:, :, seq - min(WINDOW, seq):])
        outs["hg_p"].append(s_new)
        outs["rg_p"].append(h_new.reshape(bp, B_WIDTH))
        outs["cv_p"].append(cv_new)

        proj_s = _inproj(xs, scale[None, bp:], shift[None, bp:], w_in_p[l], bs).reshape(bs, N_PROJ)
        hg_s_all, ya_s = _hgrn_decode(state_hgrn, l, proj_s, loglb[l].reshape(A_HEADS, A_DK, 1),
                                      l1mlb[l].reshape(A_HEADS, A_DK, 1), gain, hg_s_all)
        yb_s, h_s, cv_s = _rglru_decode(proj_s, state_conv[l].reshape(bs, -1), state_rglru[l], rg_w)
        q8 = proj_s[:, COL_CQ:COL_CQ + 512].reshape(bs, C_HEADS, C_DH)
        cz8 = proj_s[:, COL_CZ:COL_CZ + 512].reshape(bs, C_HEADS, C_DH)
        kv12 = proj_s[:, COL_CKV:COL_CKV + 768].reshape(bs, 12, C_DH)
        cg83 = proj_s[:, COL_CG:COL_CG + 3 * C_HEADS].reshape(bs, 3, C_HEADS).transpose(0, 2, 1)
        yc_s, rows_s, win_s_all = _nsa_decode(cache_t, l, page_table, win_t, q8, kv12, cg83, cz8, cos_s, sin_s,
                                              pos256, w1bd, w2bd, win_s_all)
        xs = _merge(ya_s.reshape(1, bs, 512), yb_s.reshape(1, bs, 512), yc_s.reshape(1, bs, 512),
                    proj_s.reshape(1, bs, N_PROJ), xs, gate[None, bp:], wb_bf[l], wo_bf[l], lg, lbias, bs)
        outs["nsa_s"].append(rows_s.reshape(bs, 1, 4, C_KV_HEADS, C_DH))
        outs["rg_s"].append(h_s)
        outs["cv_s"].append(cv_s.reshape(bs, CONV_W - 1, B_WIDTH))

    st = {k: jnp.stack(v, axis=0) for k, v in outs.items()}
    return (xp, xs.reshape(bs, 1, D_MODEL), _token_major(nsa_p_all, 2), st["nsa_s"],
            _token_major(st["win_p"], 2), _token_major(win_s_all, 2),
            st["hg_p"], hg_s_all, st["rg_p"], st["rg_s"], st["cv_p"], st["cv_s"])
```

```python
import numpy as np
import jax
import jax.numpy as jnp
from jax import lax
from jax.experimental import pallas as pl
from jax.experimental.pallas import tpu as pltpu

F32 = jnp.float32
BF16 = jnp.bfloat16

D_MODEL = 1024
DEPTH = 4
PAGE_SIZE = 128
A_HEADS = 4
A_DK = 128
A_CHUNK = 64
B_WIDTH = 512
B_BLOCKS = 8
CONV_W = 4
LRU_C = 8.0
C_HEADS = 8
C_KV_HEADS = 2
C_GROUP = 4
C_DH = 64
CMP_LEN = 32
CMP_STRIDE = 16
SLC_BLK = 64
SLC_TOPN = 16
WINDOW = 512
ROPE_THETA = 10000.0
ATT_SCALE = C_DH ** -0.5
LOG2_E = 1.4426950408889634
ALPHA = (2 * DEPTH) ** 0.25
LN_EPS = 1e-5
NEG_INF = -1e30
FORCED_SCORE = 1e9

COL_MG, COL_AQ, COL_AF, COL_AI, COL_AZ = 0, 3072, 3584, 4096, 4608
COL_BX, COL_BZ, COL_CQ, COL_CZ, COL_CKV, COL_CG = 5120, 5632, 6144, 6656, 7168, 7936
N_PROJ = 8064
PROJ_TN = 2688

VMEM_LIMIT = 56 * 1024 * 1024

NT_DIMS = (((1,), (1,)), ((), ()))
TN_DIMS = (((0,), (0,)), ((), ()))


def _cparams(*sem):
    return pltpu.CompilerParams(dimension_semantics=sem, vmem_limit_bytes=VMEM_LIMIT)


def _sigmoid(x):
    return jax.nn.sigmoid(x)


def _silu(x):
    return x * jax.nn.sigmoid(x)


def _forget_gate(af, lb, one_minus_lb):
    e = jnp.exp(-jnp.abs(af))
    r = 1.0 / (1.0 + e)
    er = e * r
    pos = af >= 0.0
    return lb + one_minus_lb * jnp.where(pos, r, er), one_minus_lb * jnp.where(pos, er, r)


def _ada_kernel(c_ref, w_ref, b_ref, o_ref):
    o_ref[0] = jnp.dot(c_ref[...], w_ref[0].astype(BF16), preferred_element_type=F32) + b_ref[0]


def _ada_all(c_all, ada_w, ada_b):
    nb = c_all.shape[0]
    return pl.pallas_call(
        _ada_kernel,
        grid=(DEPTH, 3),
        in_specs=[
            pl.BlockSpec((nb, D_MODEL), lambda l, j: (0, 0)),
            pl.BlockSpec((1, D_MODEL, D_MODEL), lambda l, j: (l, 0, j)),
            pl.BlockSpec((1, 1, D_MODEL), lambda l, j: (l, 0, j)),
        ],
        out_specs=pl.BlockSpec((1, nb, D_MODEL), lambda l, j: (l, 0, j)),
        out_shape=jax.ShapeDtypeStruct((DEPTH, nb, 3 * D_MODEL), F32),
        compiler_params=_cparams("arbitrary", "arbitrary"),
        name="ada",
    )(c_all.astype(BF16), ada_w, ada_b.reshape(DEPTH, 1, 3 * D_MODEL))


def _inproj_kernel(x_ref, sc_ref, sh_ref, w_ref, o_ref, h_ref):
    @pl.when(pl.program_id(2) == 0)
    def _():
        h_ref[...] = (x_ref[0] * (1.0 + sc_ref[0]) + sh_ref[0]).astype(BF16)

    o_ref[0] = lax.dot_general(h_ref[...], w_ref[...], NT_DIMS, preferred_element_type=F32)


def _inproj(x, scale, shift, w_bf, tt):
    B, T, _ = x.shape
    ts = scale.shape[1]
    mod_spec = (pl.BlockSpec((1, 1, D_MODEL), lambda b, t, n: (b, 0, 0)) if ts == 1
                else pl.BlockSpec((1, tt, D_MODEL), lambda b, t, n: (b, t, 0)))
    return pl.pallas_call(
        _inproj_kernel,
        grid=(B, T // tt, N_PROJ // PROJ_TN),
        in_specs=[
            pl.BlockSpec((1, tt, D_MODEL), lambda b, t, n: (b, t, 0)),
            mod_spec, mod_spec,
            pl.BlockSpec((PROJ_TN, D_MODEL), lambda b, t, n: (n, 0)),
        ],
        out_specs=pl.BlockSpec((1, tt, PROJ_TN), lambda b, t, n: (b, t, n)),
        out_shape=jax.ShapeDtypeStruct((B, T, N_PROJ), F32),
        scratch_shapes=[pltpu.VMEM((tt, D_MODEL), BF16)],
        compiler_params=_cparams("arbitrary", "arbitrary", "arbitrary"),
        name="inproj",
    )(x, scale, shift, w_bf)


HG_TT = 256
HG_HP = 4
HG_LEVELS = (1, 2, 4, 8, 16, 32)


def _hgrn_level_table():
    t = np.arange(HG_TT)[:, None]
    s = np.arange(HG_TT)[None, :]
    lvl = np.full((HG_TT, HG_TT), -1, np.int32)
    same = (t // A_CHUNK) == (s // A_CHUNK)
    for i, h in enumerate(HG_LEVELS):
        m = same & ((t // h) == (s // h) + 1) & (((s // h) % 2) == 0)
        lvl[m] = i
    lvl[t == s] = len(HG_LEVELS)
    return lvl


def _hgrn_prompt_kernel(aq_ref, af_ref, ai_ref, az_ref, lb_ref, oml_ref, g_ref, lvl_ref,
                        ya_ref, so_ref, s_ref):
    t = pl.program_id(2)

    @pl.when(t == 0)
    def _():
        s_ref[...] = jnp.zeros_like(s_ref)

    heads = range(HG_HP)
    cols = [slice(hh * A_DK, (hh + 1) * A_DK) for hh in heads]
    lvl = lvl_ref[...]
    row = lax.broadcasted_iota(jnp.int32, (HG_TT, A_DK), 0)
    q = [_silu(aq_ref[0, :, cs]) for cs in cols]
    gates = [_forget_gate(af_ref[0, :, cs], lb_ref[:, cs], oml_ref[:, cs]) for cs in cols]
    k = [gk for _, gk in gates]
    vb = [ai_ref[0, :, cs].astype(BF16) for cs in cols]

    cin = [jnp.log(gf) for gf, _ in gates]
    rex = [jnp.zeros_like(c) for c in cin]
    att = [jnp.where(lvl == len(HG_LEVELS), jnp.sum(q[hh] * k[hh], axis=-1, keepdims=True), 0.0) for hh in heads]
    for i, h in enumerate(HG_LEVELS):
        upper = ((row // h) % 2) == 1
        qh = [(q[hh] * jnp.exp(cin[hh])).astype(BF16) for hh in heads]
        kh = [(k[hh] * jnp.exp(rex[hh])).astype(BF16) for hh in heads]
        a = [lax.dot_general(qh[hh], kh[hh], NT_DIMS, preferred_element_type=F32) for hh in heads]
        for hh in heads:
            att[hh] = jnp.where(lvl == i, a[hh], att[hh])
            tot = cin[hh] + rex[hh]
            cin[hh] = cin[hh] + jnp.where(upper, pltpu.roll(tot, h, 0), 0.0)
            rex[hh] = rex[hh] + jnp.where(upper, 0.0, pltpu.roll(tot, HG_TT - h, 0))

    o = [jnp.dot(att[hh].astype(BF16), vb[hh], preferred_element_type=F32) for hh in heads]
    qb = [(q[hh] * jnp.exp(cin[hh])).astype(BF16) for hh in heads]
    kd = [(k[hh] * jnp.exp(rex[hh])).astype(BF16) for hh in heads]
    tot = [cin[hh] + rex[hh] for hh in heads]
    s = [s_ref[hh] for hh in heads]
    inter = [[] for _ in heads]
    for c in range(HG_TT // A_CHUNK):
        sl = slice(c * A_CHUNK, (c + 1) * A_CHUNK)
        for hh in heads:
            inter[hh].append(jnp.dot(qb[hh][sl], s[hh].astype(BF16), preferred_element_type=F32))
        u = [lax.dot_general(kd[hh][sl], vb[hh][sl], TN_DIMS, preferred_element_type=F32) for hh in heads]
        for hh in heads:
            e_rows = jnp.broadcast_to(jnp.exp(tot[hh][c * A_CHUNK:c * A_CHUNK + 1]), (A_DK, A_DK))
            s[hh] = e_rows.T * s[hh] + u[hh]

    for hh, cs in enumerate(cols):
        oh = o[hh] + jnp.concatenate(inter[hh], axis=0)
        ms = jnp.mean(oh * oh, axis=-1, keepdims=True)
        ya_ref[0, :, cs] = ((oh * lax.rsqrt(ms + LN_EPS)) * g_ref[:, cs] * _silu(az_ref[0, :, cs])).astype(BF16)
        s_ref[hh] = s[hh]

    @pl.when(t == pl.num_programs(2) - 1)
    def _():
        so_ref[0] = s_ref[...]


def _hgrn_prompt(proj, lower_bound, one_minus_lb, gain):
    B, T, _ = proj.shape
    lvl = jnp.asarray(_hgrn_level_table())
    width = HG_HP * A_DK

    def col(base):
        return pl.BlockSpec((1, HG_TT, width), lambda b, h, t: (b, t, base // width + h))

    vec = pl.BlockSpec((1, width), lambda b, h, t: (0, h))
    return pl.pallas_call(
        _hgrn_prompt_kernel,
        grid=(B, A_HEADS // HG_HP, T // HG_TT),
        in_specs=[col(COL_AQ), col(COL_AF), col(COL_AI), col(COL_AZ), vec, vec, vec,
                  pl.BlockSpec((HG_TT, HG_TT), lambda b, h, t: (0, 0))],
        out_specs=[pl.BlockSpec((1, HG_TT, width), lambda b, h, t: (b, t, h)),
                   pl.BlockSpec((1, HG_HP, A_DK, A_DK), lambda b, h, t: (b, h, 0, 0))],
        out_shape=[jax.ShapeDtypeStruct((B, T, A_HEADS * A_DK), BF16),
                   jax.ShapeDtypeStruct((B, A_HEADS, A_DK, A_DK), F32)],
        scratch_shapes=[pltpu.VMEM((HG_HP, A_DK, A_DK), F32)],
        compiler_params=_cparams("arbitrary", "arbitrary", "arbitrary"),
        name="hgrn_prompt",
    )(proj, proj, proj, proj, lower_bound, one_minus_lb, gain, lvl)


HG_SB = 16


def _hgrn_decode_kernel(s_ref, aqc_ref, afc_ref, ai_ref, az_ref, lb_ref, oml_ref, g_ref, so_ref, ya_ref):
    aqc = aqc_ref[0, 0]
    qc = _silu(aqc)
    ec, kc = _forget_gate(afc_ref[0, 0], lb_ref[0], oml_ref[0])
    v = ai_ref[...]
    rows = []
    for r in range(HG_SB):
        sn = ec[:, r:r + 1] * s_ref[r, 0] + kc[:, r:r + 1] * v[r:r + 1, :]
        so_ref[r, 0] = sn
        rows.append(jnp.sum(qc[:, r:r + 1] * sn, axis=0, keepdims=True))
    o = jnp.concatenate(rows, axis=0)
    ms = jnp.mean(o * o, axis=-1, keepdims=True)
    ya_ref[...] = ((o * lax.rsqrt(ms + LN_EPS)) * g_ref[...] * _silu(az_ref[...])).astype(BF16)


def _skip_ref(kernel_fn, pos):
    def wrapped(*refs):
        return kernel_fn(*refs[:pos], *refs[pos + 1:])

    return wrapped


def _hgrn_decode(state_all, layer, proj_s, lower_bound_c, one_minus_lb_c, gain, new_state_all):
    nb = proj_s.shape[0]
    nblk = nb // HG_SB

    def cols(base):
        a = proj_s[:, base:base + A_HEADS * A_DK].reshape(nblk, HG_SB, A_HEADS, A_DK)
        return a.transpose(2, 0, 3, 1)

    st_spec = pl.BlockSpec((None, HG_SB, 1, A_DK, A_DK), lambda h, i: (layer, i, h, 0, 0))
    col_spec = pl.BlockSpec((1, 1, A_DK, HG_SB), lambda h, i: (h, i, 0, 0))
    cvec = pl.BlockSpec((1, A_DK, 1), lambda h, i: (h, 0, 0))
    return pl.pallas_call(
        _skip_ref(_hgrn_decode_kernel, 8),
        grid=(A_HEADS, nblk),
        in_specs=[st_spec, col_spec, col_spec,
                  pl.BlockSpec((HG_SB, A_DK), lambda h, i: (i, COL_AI // A_DK + h)),
                  pl.BlockSpec((HG_SB, A_DK), lambda h, i: (i, COL_AZ // A_DK + h)),
                  cvec, cvec,
                  pl.BlockSpec((1, A_DK), lambda h, i: (0, h)),
                  pl.BlockSpec(memory_space=pl.ANY)],
        out_specs=[st_spec, pl.BlockSpec((HG_SB, A_DK), lambda h, i: (i, h))],
        out_shape=[jax.ShapeDtypeStruct(new_state_all.shape, F32),
                   jax.ShapeDtypeStruct((nb, A_HEADS * A_DK), BF16)],
        input_output_aliases={8: 0},
        compiler_params=_cparams("arbitrary", "arbitrary"),
        name="hgrn_decode",
    )(state_all, cols(COL_AQ), cols(COL_AF), proj_s, proj_s, lower_bound_c, one_minus_lb_c, gain, new_state_all)


RG_TT = 256


def _rglru_gates(u, ga_ref, gab_ref, gx_ref, gxb_ref, lam_ref):
    ub = u.astype(BF16)
    r = _sigmoid(jnp.dot(ub, ga_ref[...], preferred_element_type=F32) + gab_ref[...])
    ig = _sigmoid(jnp.dot(ub, gx_ref[...], preferred_element_type=F32) + gxb_ref[...])
    nl = -lam_ref[...]
    softplus = jnp.maximum(nl, 0.0) + jnp.log1p(jnp.exp(-jnp.abs(nl)))
    a = jnp.exp(-LRU_C * r * softplus)
    return a, jnp.sqrt(1.0 - a * a) * (ig * u)


def _rglru_prompt_kernel(bx_ref, bz_ref, cw_ref, cb_ref, ga_ref, gab_ref, gx_ref, gxb_ref, lam_ref,
                         yb_ref, h_ref, cv_ref, ext_ref, hc_ref):
    t = pl.program_id(1)

    @pl.when(t == 0)
    def _():
        ext_ref[0:8, :] = jnp.zeros((8, B_WIDTH), F32)
        hc_ref[...] = jnp.zeros_like(hc_ref)

    x = bx_ref[0]
    ext_ref[8:8 + RG_TT, :] = x
    u = cb_ref[...] + (ext_ref[5:5 + RG_TT, :] * cw_ref[0:1, :] + ext_ref[6:6 + RG_TT, :] * cw_ref[1:2, :]
                       + ext_ref[7:7 + RG_TT, :] * cw_ref[2:3, :] + x * cw_ref[3:4, :])
    ext_ref[0:8, :] = x[RG_TT - 8:RG_TT, :]
    a, bt = _rglru_gates(u, ga_ref, gab_ref, gx_ref, gxb_ref, lam_ref)
    row = lax.broadcasted_iota(jnp.int32, (RG_TT, 128), 0)
    z = bz_ref[0]
    for c in range(B_WIDTH // 128):
        cs = slice(c * 128, (c + 1) * 128)
        ac = a[:, cs]
        bc = bt[:, cs] + jnp.where(row == 0, ac * hc_ref[0:1, cs], 0.0)
        s = 1
        while s < RG_TT:
            keep = row >= s
            bc = jnp.where(keep, ac * pltpu.roll(bc, s, 0) + bc, bc)
            ac = jnp.where(keep, ac * pltpu.roll(ac, s, 0), ac)
            s *= 2
        hc_ref[0:1, cs] = bc[RG_TT - 1:RG_TT, :]
        yb_ref[0, :, cs] = (bc * _silu(z[:, cs])).astype(BF16)

    @pl.when(t == pl.num_programs(1) - 1)
    def _():
        h_ref[0] = hc_ref[0:1, :]
        cv_ref[0] = x[RG_TT - (CONV_W - 1):RG_TT, :]


def _rglru_weight_specs(idx):
    return [pl.BlockSpec((CONV_W, B_WIDTH), idx), pl.BlockSpec((1, B_WIDTH), idx),
            pl.BlockSpec((B_WIDTH, B_WIDTH), idx), pl.BlockSpec((1, B_WIDTH), idx),
            pl.BlockSpec((B_WIDTH, B_WIDTH), idx), pl.BlockSpec((1, B_WIDTH), idx),
            pl.BlockSpec((1, B_WIDTH), idx)]


def _rglru_prompt(proj, w):
    B, T, _ = proj.shape
    return pl.pallas_call(
        _rglru_prompt_kernel,
        grid=(B, T // RG_TT),
        in_specs=[pl.BlockSpec((1, RG_TT, B_WIDTH), lambda b, t: (b, t, COL_BX // B_WIDTH)),
                  pl.BlockSpec((1, RG_TT, B_WIDTH), lambda b, t: (b, t, COL_BZ // B_WIDTH))]
        + _rglru_weight_specs(lambda b, t: (0, 0)),
        out_specs=[pl.BlockSpec((1, RG_TT, B_WIDTH), lambda b, t: (b, t, 0)),
                   pl.BlockSpec((1, 1, B_WIDTH), lambda b, t: (b, 0, 0)),
                   pl.BlockSpec((1, CONV_W - 1, B_WIDTH), lambda b, t: (b, 0, 0))],
        out_shape=[jax.ShapeDtypeStruct((B, T, B_WIDTH), BF16),
                   jax.ShapeDtypeStruct((B, 1, B_WIDTH), F32),
                   jax.ShapeDtypeStruct((B, CONV_W - 1, B_WIDTH), F32)],
        scratch_shapes=[pltpu.VMEM((8 + RG_TT, B_WIDTH), F32), pltpu.VMEM((8, B_WIDTH), F32)],
        compiler_params=_cparams("arbitrary", "arbitrary"),
        name="rglru_prompt",
    )(proj, proj, *w)


def _rglru_decode_kernel(bx_ref, bz_ref, cv0_ref, h0_ref, cw_ref, cb_ref, ga_ref, gab_ref, gx_ref, gxb_ref,
                         lam_ref, yb_ref, h_ref, cv_ref):
    x = bx_ref[...]
    c0 = cv0_ref[:, 0:B_WIDTH]
    c1 = cv0_ref[:, B_WIDTH:2 * B_WIDTH]
    c2 = cv0_ref[:, 2 * B_WIDTH:3 * B_WIDTH]
    u = cb_ref[...] + (c0 * cw_ref[0:1, :] + c1 * cw_ref[1:2, :] + c2 * cw_ref[2:3, :] + x * cw_ref[3:4, :])
    a, bt = _rglru_gates(u, ga_ref, gab_ref, gx_ref, gxb_ref, lam_ref)
    h = a * h0_ref[...] + bt
    h_ref[...] = h
    yb_ref[...] = (h * _silu(bz_ref[...])).astype(BF16)
    cv_ref[:, 0:B_WIDTH] = c1
    cv_ref[:, B_WIDTH:2 * B_WIDTH] = c2
    cv_ref[:, 2 * B_WIDTH:3 * B_WIDTH] = x


def _rglru_decode(proj_s, conv0, h0, w):
    nb = proj_s.shape[0]
    z2 = lambda i: (0, 0)
    return pl.pallas_call(
        _rglru_decode_kernel,
        grid=(1,),
        in_specs=[pl.BlockSpec((nb, B_WIDTH), lambda i: (0, COL_BX // B_WIDTH)),
                  pl.BlockSpec((nb, B_WIDTH), lambda i: (0, COL_BZ // B_WIDTH)),
                  pl.BlockSpec((nb, 3 * B_WIDTH), z2), pl.BlockSpec((nb, B_WIDTH), z2)]
        + _rglru_weight_specs(z2),
        out_specs=[pl.BlockSpec((nb, B_WIDTH), z2), pl.BlockSpec((nb, B_WIDTH), z2),
                   pl.BlockSpec((nb, 3 * B_WIDTH), z2)],
        out_shape=[jax.ShapeDtypeStruct((nb, B_WIDTH), BF16), jax.ShapeDtypeStruct((nb, B_WIDTH), F32),
                   jax.ShapeDtypeStruct((nb, 3 * B_WIDTH), F32)],
        compiler_params=_cparams("arbitrary"),
        name="rglru_decode",
    )(proj_s, proj_s, conv0, h0, *w)


def _gelu_tanh(x):
    return 0.5 * x * (1.0 + jnp.tanh(0.7978845608028654 * (x + 0.044715 * (x * x * x))))


def _compress(load_l, bias_ref, w1_ref, w2_ref, m):
    x = jnp.concatenate([load_l(l).astype(BF16) for l in range(CMP_STRIDE)], axis=1)
    acc = jnp.dot(x, w1_ref[...], preferred_element_type=F32)
    acc_a = acc[:, 0:256] + bias_ref[0:1, 0:256]
    acc_b = acc[:, 256:512] + bias_ref[1:2, 256:512]
    hid = _gelu_tanh(acc_a + pltpu.roll(acc_b, m - 1, 0))
    return jnp.dot(hid.astype(BF16), w2_ref[...], preferred_element_type=F32)


def _compress_bias_kernel(pos_ref, w1_ref, o_ref):
    o_ref[...] = jnp.dot(pos_ref[...], w1_ref[...].astype(F32), precision=lax.Precision.HIGHEST,
                         preferred_element_type=F32)


def _compress_bias(pos_rows, w1cat):
    return pl.pallas_call(
        _compress_bias_kernel,
        grid=(1,),
        in_specs=[pl.BlockSpec(pos_rows.shape, lambda i: (0, 0)), pl.BlockSpec(w1cat.shape, lambda i: (0, 0))],
        out_specs=pl.BlockSpec((8, 512), lambda i: (0, 0)),
        out_shape=jax.ShapeDtypeStruct((8, 512), F32),
        compiler_params=_cparams("arbitrary"),
        name="compress_bias",
    )(pos_rows, w1cat)


def _rank_select(score, idx, n_rows, axis):
    rank = jnp.zeros_like(score)
    for jp in range(n_rows):
        other = score[jp:jp + 1, :] if axis == 0 else score[:, jp:jp + 1]
        rank = rank + jnp.where(other > score, 1.0, jnp.where(other == score, jnp.where(idx > jp, 1.0, 0.0), 0.0))
    return jnp.where(rank < float(SLC_TOPN), 1.0, 0.0)


def _forced_score(idx, cur, imp):
    return jnp.where(idx == 0, FORCED_SCORE,
                     jnp.where(idx == cur, FORCED_SCORE,
                               jnp.where(idx == cur - 1, FORCED_SCORE, jnp.where(idx <= cur, imp, -1.0))))


NP_TT = 256


def _nsa_prep_kernel(cq_ref, kv01_ref, kv23_ref, kv45_ref, cos_ref, sin_ref,
                     rows_t_ref, win_t_ref, raw_ref, q_ref, qr_ref, ks_ref, kw_ref):
    cos = cos_ref[...]
    sin = sin_ref[...]
    lane = lax.broadcasted_iota(jnp.int32, cos.shape, 1)
    first = (lane % C_DH) < (C_DH // 2)

    def rope(x):
        return x * cos + jnp.where(first, pltpu.roll(x, 128 - C_DH // 2, 1), pltpu.roll(x, C_DH // 2, 1)) * sin

    def heads(ref, x, h0):
        ref[0, h0] = x[:, 0:C_DH].astype(BF16)
        ref[0, h0 + 1] = x[:, C_DH:2 * C_DH].astype(BF16)

    cq = cq_ref[0]
    for c in range(C_HEADS // 2):
        x = cq[:, c * 128:(c + 1) * 128]
        heads(q_ref, x * ATT_SCALE, 2 * c)
        heads(qr_ref, rope(x * (ATT_SCALE * LOG2_E)), 2 * c)
    kv01 = kv01_ref[0]
    kv23 = kv23_ref[0]
    kv45 = kv45_ref[0]
    ks = rope(kv23[:, 0:128])
    kw = rope(kv45[:, 0:128])
    raw_ref[0] = kv01
    heads(ks_ref, ks, 0)
    heads(kw_ref, kw, 0)
    rows_t_ref[0, 0:128, :] = kv01[:, 0:128].T
    rows_t_ref[0, 128:256, :] = kv01[:, 128:256].T
    rows_t_ref[0, 256:384, :] = ks.T
    rows_t_ref[0, 384:512, :] = kv23[:, 128:256].T
    win_t_ref[0, 0:128, :] = kw.T
    win_t_ref[0, 128:256, :] = kv45[:, 128:256].T


def _nsa_prep(proj, cos128, sin128, rows_all, layer):
    B, T, _ = proj.shape
    hm = lambda n: pl.BlockSpec((1, n, NP_TT, C_DH), lambda b, t: (b, 0, t, 0))
    hs = lambda n: jax.ShapeDtypeStruct((B, n, T, C_DH), BF16)
    kvb = COL_CKV // 256
    return pl.pallas_call(
        _skip_ref(_nsa_prep_kernel, 6),
        grid=(B, T // NP_TT),
        in_specs=[pl.BlockSpec((1, NP_TT, 512), lambda b, t: (b, t, COL_CQ // 512)),
                  pl.BlockSpec((1, NP_TT, 256), lambda b, t: (b, t, kvb)),
                  pl.BlockSpec((1, NP_TT, 256), lambda b, t: (b, t, kvb + 1)),
                  pl.BlockSpec((1, NP_TT, 256), lambda b, t: (b, t, kvb + 2)),
                  pl.BlockSpec((NP_TT, 128), lambda b, t: (t, 0)),
                  pl.BlockSpec((NP_TT, 128), lambda b, t: (t, 0)),
                  pl.BlockSpec(memory_space=pl.ANY)],
        out_specs=[pl.BlockSpec((None, 1, 512, NP_TT), lambda b, t: (layer, b, 0, t)),
                   pl.BlockSpec((1, 256, NP_TT), lambda b, t: (b, 0, t)),
                   pl.BlockSpec((1, NP_TT, 256), lambda b, t: (b, t, 0)),
                   hm(8), hm(8), hm(2), hm(2)],
        out_shape=[jax.ShapeDtypeStruct(rows_all.shape, F32), jax.ShapeDtypeStruct((B, 256, T), F32),
                   jax.ShapeDtypeStruct((B, T, 256), F32), hs(8), hs(8), hs(2), hs(2)],
        input_output_aliases={6: 0},
        compiler_params=_cparams("arbitrary", "arbitrary"),
        name="nsa_prep",
    )(proj, proj, proj, proj, cos128, sin128, rows_all)


def _compress_prompt_kernel(xk_ref, xv_ref, pos_ref, w1_ref, w2_ref, kc_ref, vct_ref):
    def load_l(l):
        return jnp.concatenate([xk_ref[pl.ds(l, 128, stride=CMP_STRIDE), :],
                                xv_ref[pl.ds(l, 128, stride=CMP_STRIDE), :]], axis=1)

    out = _compress(load_l, pos_ref, w1_ref, w2_ref, 128)
    vt = out[:, 128:256].T
    for g in range(C_KV_HEADS):
        kc_ref[0, g] = out[:, g * C_DH:(g + 1) * C_DH].astype(BF16)
        vct_ref[0, g] = vt[g * C_DH:(g + 1) * C_DH, :].astype(BF16)


def _compress_prompt(raw, pos256, w1bd, w2bd):
    B, T, _ = raw.shape
    return pl.pallas_call(
        _compress_prompt_kernel,
        grid=(B,),
        in_specs=[pl.BlockSpec((None, T, 128), lambda b: (b, 0, 0)),
                  pl.BlockSpec((None, T, 128), lambda b: (b, 0, 1)),
                  pl.BlockSpec((8, 512), lambda b: (0, 0)),
                  pl.BlockSpec((CMP_STRIDE * 256, 512), lambda b: (0, 0)),
                  pl.BlockSpec((256, 256), lambda b: (0, 0))],
        out_specs=[pl.BlockSpec((1, C_KV_HEADS, 128, C_DH), lambda b: (b, 0, 0, 0)),
                   pl.BlockSpec((1, C_KV_HEADS, C_DH, 128), lambda b: (b, 0, 0, 0))],
        out_shape=[jax.ShapeDtypeStruct((B, C_KV_HEADS, 128, C_DH), BF16),
                   jax.ShapeDtypeStruct((B, C_KV_HEADS, C_DH, 128), BF16)],
        compiler_params=_cparams("arbitrary"),
        name="compress_prompt",
    )(raw, raw, pos256, w1bd, w2bd)


NA_TQ = 256
NA_KT = 256
NA_M = C_GROUP * NA_TQ
FLASH_ACC_ROWS = C_DH + 16


def _lanes4(x):
    return jnp.concatenate([x] * C_GROUP, axis=1)


def _flash_t(qs, k_ref, vt_ref, j_lo, j_hi, bias_fn, s_ref, p_ref, acc_ref):
    groups = range(C_KV_HEADS)
    ones_rows = jnp.ones((FLASH_ACC_ROWS - C_DH, NA_KT), BF16)

    def scores(j, g):
        off = pl.multiple_of(j * NA_KT, NA_KT)
        return lax.dot_general(k_ref[0, g, pl.ds(off, NA_KT), :], qs[g], NT_DIMS, preferred_element_type=F32)

    def values(j, g):
        off = pl.multiple_of(j * NA_KT, NA_KT)
        vt = vt_ref[0, g * C_DH:(g + 1) * C_DH, pl.ds(off, NA_KT)].astype(BF16)
        return jnp.dot(jnp.concatenate([vt, ones_rows], axis=0), p_ref[g], preferred_element_type=F32)

    acc_ref[...] = jnp.zeros_like(acc_ref)
    p_ref[...] = jnp.zeros_like(p_ref)
    for g in groups:
        s_ref[g] = scores(j_lo, g)
    row = lambda v: tuple(jnp.full((1, NA_M), v, F32) for _ in groups)

    def body(j, carry):
        m, alpha = carry
        j_next = jnp.minimum(j + 1, j_hi - 1)
        j_prev = jnp.maximum(j - 1, j_lo)
        m_out, a_out = [], []
        for g in groups:
            s = s_ref[g] + _lanes4(bias_fn(j, g))
            s_ref[g] = scores(j_next, g)
            acc_ref[g] = alpha[g] * acc_ref[g] + values(j_prev, g)
            m_new = jnp.maximum(m[g], jnp.max(s, axis=0, keepdims=True))
            p_ref[g] = jnp.exp2(s - m_new).astype(BF16)
            m_out.append(m_new)
            a_out.append(jnp.exp2(m[g] - m_new))
        return tuple(m_out), tuple(a_out)

    m, alpha = lax.fori_loop(j_lo, j_hi, body, (row(NEG_INF), row(1.0)))
    outs = []
    for g in groups:
        acc = alpha[g] * acc_ref[g] + values(j_hi - 1, g)
        outs.append(acc[0:C_DH] / acc[C_DH:C_DH + 1])
    return outs


def _nsa_prompt_kernel(q_ref, qr_ref, kc_ref, vct_ref, ks_ref, vst_ref, kw_ref, vwt_ref, cg_ref, cz_ref, ovt_ref,
                       yc_ref, s_ref, p_ref, acc_ref, sel_ref):
    qi = pl.program_id(1)
    t0 = qi * NA_TQ
    n_slc = ovt_ref.shape[0]
    sig_t = _sigmoid(cg_ref[0]).T
    cmp_row = lax.broadcasted_iota(jnp.int32, (128, NA_TQ), 0)
    cmp_q = t0 + lax.broadcasted_iota(jnp.int32, (128, NA_TQ), 1)
    cmp_ok = _lanes4(jnp.where(CMP_STRIDE * cmp_row + (CMP_LEN - 1) <= cmp_q, 1.0, 0.0)) > 0.5
    jidx = lax.broadcasted_iota(jnp.int32, (n_slc, NA_TQ), 0)
    cur = (t0 + lax.broadcasted_iota(jnp.int32, (n_slc, NA_TQ), 1)) // SLC_BLK
    groups = range(C_KV_HEADS)
    qrs = [qr_ref[0, C_GROUP * g:C_GROUP * (g + 1)].reshape(NA_M, C_DH) for g in groups]
    s = [lax.dot_general(kc_ref[0, g], q_ref[0, C_GROUP * g:C_GROUP * (g + 1)].reshape(NA_M, C_DH), NT_DIMS,
                         preferred_element_type=F32) for g in groups]
    p = []
    for g in groups:
        sm = jnp.where(cmp_ok, s[g], NEG_INF)
        e = jnp.where(cmp_ok, jnp.exp(sm - jnp.max(sm, axis=0, keepdims=True)), 0.0)
        p.append(e / jnp.maximum(jnp.sum(e, axis=0, keepdims=True), 1e-30))
    o_cmp = [jnp.dot(vct_ref[0, g], p[g].astype(BF16), preferred_element_type=F32) for g in groups]
    imp = [jnp.dot(ovt_ref[...],
                   p[g][:, 0:NA_TQ] + p[g][:, NA_TQ:2 * NA_TQ] + p[g][:, 2 * NA_TQ:3 * NA_TQ] + p[g][:, 3 * NA_TQ:],
                   precision=lax.Precision.HIGHEST, preferred_element_type=F32) for g in groups]
    for g in groups:
        sel_t = _rank_select(_forced_score(jidx, cur, imp[g]), jidx, n_slc, 0)
        sel_bias = jnp.where(sel_t > 0.5, 0.0, NEG_INF)
        for jj in range(n_slc):
            sel_ref[g, jj] = sel_bias[jj:jj + 1, :]

    key_row = lax.broadcasted_iota(jnp.int32, (NA_KT, NA_TQ), 0)
    q_pos = t0 + lax.broadcasted_iota(jnp.int32, (NA_KT, NA_TQ), 1)
    blocks_per_tile = NA_KT // SLC_BLK

    def slc_mask(j, g):
        chosen = sel_ref[g, blocks_per_tile * j + blocks_per_tile - 1]
        for i in range(blocks_per_tile - 2, -1, -1):
            chosen = jnp.where(key_row < (i + 1) * SLC_BLK, sel_ref[g, blocks_per_tile * j + i], chosen)
        return jnp.where(j * NA_KT + key_row <= q_pos, chosen, NEG_INF)

    def win_mask(j, g):
        kpos = j * NA_KT + key_row
        return jnp.where(kpos <= q_pos, jnp.where(kpos > q_pos - WINDOW, 0.0, NEG_INF), NEG_INF)

    j_hi = (t0 + NA_TQ + NA_KT - 1) // NA_KT
    o_slc = _flash_t(qrs, ks_ref, vst_ref, 0, j_hi, slc_mask, s_ref, p_ref, acc_ref)
    o_win = _flash_t(qrs, kw_ref, vwt_ref, jnp.maximum(t0 - (WINDOW - 1), 0) // NA_KT, j_hi, win_mask,
                     s_ref, p_ref, acc_ref)
    heads_t = []
    for g in range(C_KV_HEADS):
        for hh in range(C_GROUP):
            hd = C_GROUP * g + hh
            cs = slice(hh * NA_TQ, (hh + 1) * NA_TQ)
            heads_t.append(sig_t[hd:hd + 1, :] * o_cmp[g][:, cs]
                           + sig_t[C_HEADS + hd:C_HEADS + hd + 1, :] * o_slc[g][:, cs]
                           + sig_t[2 * C_HEADS + hd:2 * C_HEADS + hd + 1, :] * o_win[g][:, cs])
    yc_ref[0] = (jnp.concatenate(heads_t, axis=0).T * _silu(cz_ref[0])).astype(BF16)


def _overlap_t(n_slc, n_rows):
    cs = np.arange(128) * CMP_STRIDE
    ss = np.arange(n_rows) * SLC_BLK
    ov = (cs[None, :] < ss[:, None] + SLC_BLK) & (cs[None, :] + CMP_LEN > ss[:, None])
    ov &= (np.arange(128)[None, :] < 127) & (np.arange(n_rows)[:, None] < n_slc)
    return ov.astype(np.float32)


def _nsa_prompt(proj, q_hm, qr_hm, kc, vct, ks_hm, rows_all, layer, kw_hm, win_t):
    B, T, _ = proj.shape
    n_slc = T // SLC_BLK
    ovt = jnp.asarray(_overlap_t(n_slc, n_slc))
    qspec = pl.BlockSpec((1, C_HEADS, NA_TQ, C_DH), lambda b, t: (b, 0, t, 0))
    kspec = pl.BlockSpec((1, C_KV_HEADS, T, C_DH), lambda b, t: (b, 0, 0, 0))
    return pl.pallas_call(
        _nsa_prompt_kernel,
        grid=(B, T // NA_TQ),
        in_specs=[qspec, qspec,
                  pl.BlockSpec((1, C_KV_HEADS, 128, C_DH), lambda b, t: (b, 0, 0, 0)),
                  pl.BlockSpec((1, C_KV_HEADS, C_DH, 128), lambda b, t: (b, 0, 0, 0)),
                  kspec, pl.BlockSpec((None, 1, 128, T), lambda b, t: (layer, b, 3, 0)),
                  kspec, pl.BlockSpec((1, 128, T), lambda b, t: (b, 1, 0)),
                  pl.BlockSpec((1, NA_TQ, 128), lambda b, t: (b, t, COL_CG // 128)),
                  pl.BlockSpec((1, NA_TQ, 512), lambda b, t: (b, t, COL_CZ // 512)),
                  pl.BlockSpec((n_slc, 128), lambda b, t: (0, 0))],
        out_specs=pl.BlockSpec((1, NA_TQ, 512), lambda b, t: (b, t, 0)),
        out_shape=jax.ShapeDtypeStruct((B, T, 512), BF16),
        scratch_shapes=[pltpu.VMEM((C_KV_HEADS, NA_KT, NA_M), F32), pltpu.VMEM((C_KV_HEADS, NA_KT, NA_M), BF16),
                        pltpu.VMEM((C_KV_HEADS, FLASH_ACC_ROWS, NA_M), F32),
                        pltpu.VMEM((C_KV_HEADS, n_slc, 1, NA_TQ), F32)],
        compiler_params=_cparams("arbitrary", "arbitrary"),
        name="nsa_prompt",
    )(q_hm, qr_hm, kc, vct, ks_hm, rows_all, kw_hm, win_t, proj, proj, ovt)


N_PAGES = 16
ND_SB = 4


def _nsa_decode_kernel(pt_ref, *refs):
    npg = ND_SB * N_PAGES
    pages = refs[:npg]
    (q_ref, kv_ref, cg_ref, cz_ref, wb_ref, cos_ref, sin_ref, pos_ref, w1_ref, w2_ref, ov_ref, ex_ref, _,
     yc_ref, rows_ref, wn_ref) = refs[npg:]
    past = N_PAGES * PAGE_SIZE
    n_slc = -(-(past + 1) // SLC_BLK)
    cur = past // SLC_BLK
    cos = cos_ref[...]
    sin = sin_ref[...]

    def rope(x):
        return x * cos + jnp.concatenate([x[:, C_DH // 2:], x[:, :C_DH // 2]], axis=1) * sin

    def row_dot(a_bf, b_row):
        return jnp.sum(a_bf.astype(F32) * b_row.astype(BF16).astype(F32), axis=-1, keepdims=True)

    row8 = lax.broadcasted_iota(jnp.int32, (C_HEADS, 128), 0)
    lane = lax.broadcasted_iota(jnp.int32, (C_HEADS, 128), 1)
    g0 = row8 < C_GROUP
    g0h = g0[:, 0:C_DH]

    def spread(x):
        return jnp.concatenate([jnp.where(g0h, x, 0.0), jnp.where(g0h, 0.0, x)], axis=1).astype(BF16)

    def gather(x):
        return jnp.where(g0h, x[:, 0:C_DH], x[:, C_DH:2 * C_DH])

    def column_tile(row, reps):
        return jnp.concatenate([jnp.broadcast_to(row, (128, 128)).T] * reps, axis=1)

    rows_per_page = PAGE_SIZE // CMP_STRIDE

    def by_offset(x_t):
        return jnp.swapaxes(x_t.T.reshape(rows_per_page, CMP_STRIDE, 128), 0, 1)

    keys_by_offset = [by_offset(pg[0:128, :]) for pg in pages]
    vals_by_offset = [by_offset(pg[128:256, :]) for pg in pages]
    m_rows = ND_SB * 128

    def load_l(l):
        return jnp.concatenate([jnp.concatenate([t[l] for t in keys_by_offset], axis=0),
                                jnp.concatenate([t[l] for t in vals_by_offset], axis=0)], axis=1)

    kcvc_all = _compress(load_l, pos_ref, w1_ref, w2_ref, m_rows).astype(BF16)

    sbs = range(ND_SB)
    spages = [pages[sb * N_PAGES:(sb + 1) * N_PAGES] for sb in sbs]
    kcvc = [kcvc_all[sb * 128:(sb + 1) * 128] for sb in sbs]
    kv = [kv_ref[sb] for sb in sbs]
    ks_new = [rope(kv[sb][4:6]) for sb in sbs]
    kw_new = [rope(kv[sb][8:10]) for sb in sbs]
    q8 = [q_ref[sb] * ATT_SCALE for sb in sbs]
    qs = [spread(q8[sb]) for sb in sbs]
    qrs = [spread(rope(q8[sb])) for sb in sbs]
    ks_row = [jnp.concatenate([ks_new[sb][0:1], ks_new[sb][1:2]], axis=1) for sb in sbs]
    vs_row = [jnp.concatenate([kv[sb][6:7], kv[sb][7:8]], axis=1) for sb in sbs]
    kw_row = [jnp.concatenate([kw_new[sb][0:1], kw_new[sb][1:2]], axis=1) for sb in sbs]
    vw_row = [jnp.concatenate([kv[sb][10:11], kv[sb][11:12]], axis=1) for sb in sbs]

    ok = (CMP_STRIDE * lane + (CMP_LEN - 1)) <= past
    s = [lax.dot_general(qs[sb], kcvc[sb][:, 0:128], NT_DIMS, preferred_element_type=F32) for sb in sbs]
    p = []
    for sb in sbs:
        sm = jnp.where(ok, s[sb], NEG_INF)
        e = jnp.where(ok, jnp.exp(sm - jnp.max(sm, axis=-1, keepdims=True)), 0.0)
        p.append(e / jnp.maximum(jnp.sum(e, axis=-1, keepdims=True), 1e-30))
    o_cmp = [gather(jnp.dot(p[sb].astype(BF16), kcvc[sb][:, 128:256], preferred_element_type=F32)) for sb in sbs]

    imp = []
    for sb in sbs:
        ps0 = jnp.sum(jnp.where(g0, p[sb], 0.0), axis=0, keepdims=True)
        ps1 = jnp.sum(jnp.where(g0, 0.0, p[sb]), axis=0, keepdims=True)
        imp.append(jnp.dot(jnp.where(g0, ps0, ps1), ov_ref[...], precision=lax.Precision.HIGHEST,
                           preferred_element_type=F32))
    sel = [_rank_select(jnp.where(lane < n_slc, _forced_score(lane, cur, imp[sb]), -2.0), lane, n_slc, 1)
           for sb in sbs]
    chosen = [jnp.dot(sel[sb].astype(BF16), ex_ref[...], preferred_element_type=F32) for sb in sbs]

    s_pg = [[None] * N_PAGES for _ in sbs]
    ok_pg = [[None] * N_PAGES for _ in sbs]
    for i in range(N_PAGES):
        for sb in sbs:
            ok_pg[sb][i] = chosen[sb][:, i * PAGE_SIZE:(i + 1) * PAGE_SIZE] > 0.5
            sp = jnp.dot(qrs[sb], spages[sb][i][256:384, :].astype(BF16), preferred_element_type=F32)
            s_pg[sb][i] = jnp.where(ok_pg[sb][i], sp, NEG_INF)
    new_ok = [sel[sb][:, cur:cur + 1] > 0.5 for sb in sbs]
    mx, den, acc = [], [], []
    for sb in sbs:
        s_new = jnp.where(new_ok[sb], row_dot(qrs[sb], ks_row[sb]), NEG_INF)
        m = s_new
        for sp in s_pg[sb]:
            m = jnp.maximum(m, jnp.max(sp, axis=-1, keepdims=True))
        p_new = jnp.where(new_ok[sb], jnp.exp(s_new - m), 0.0)
        mx.append(m)
        den.append(p_new)
        acc.append(p_new * vs_row[sb])
    for i in range(N_PAGES):
        for sb in sbs:
            pp = jnp.where(ok_pg[sb][i], jnp.exp(s_pg[sb][i] - mx[sb]), 0.0)
            den[sb] = den[sb] + jnp.sum(pp, axis=-1, keepdims=True)
            acc[sb] = acc[sb] + lax.dot_general(pp.astype(BF16), spages[sb][i][384:512, :].astype(BF16), NT_DIMS,
                                                preferred_element_type=F32)
    o_slc = [gather(acc[sb] / den[sb]) for sb in sbs]

    nw = wb_ref.shape[2]
    wpos = (past - nw) + lax.broadcasted_iota(jnp.int32, (C_HEADS, nw), 1)
    w_ok = jnp.where(wpos <= past, jnp.where(wpos > past - WINDOW, 1.0, 0.0), 0.0) > 0.5
    sw = [jnp.dot(qrs[sb], wb_ref[sb, 0:128, :].astype(BF16), preferred_element_type=F32) for sb in sbs]
    pw, pw_new = [], []
    for sb in sbs:
        swm = jnp.where(w_ok, sw[sb], NEG_INF)
        sw_new = row_dot(qrs[sb], kw_row[sb])
        mw = jnp.maximum(jnp.max(swm, axis=-1, keepdims=True), sw_new)
        pw.append(jnp.where(w_ok, jnp.exp(swm - mw), 0.0))
        pw_new.append(jnp.exp(sw_new - mw))
    accw = [lax.dot_general(pw[sb].astype(BF16), wb_ref[sb, 128:256, :].astype(BF16), NT_DIMS,
                            preferred_element_type=F32) for sb in sbs]
    wlane = lax.broadcasted_iota(jnp.int32, (256, nw), 1)
    for sb in sbs:
        o_win = gather((accw[sb] + pw_new[sb] * vw_row[sb]) / (jnp.sum(pw[sb], axis=-1, keepdims=True) + pw_new[sb]))
        sig = _sigmoid(cg_ref[sb])
        oc = sig[:, 0:1] * o_cmp[sb] + sig[:, 1:2] * o_slc[sb] + sig[:, 2:3] * o_win
        yc_ref[sb] = (oc * _silu(cz_ref[sb])).astype(BF16)
        rows_ref[sb, 0:4, :] = kv[sb][0:4]
        rows_ref[sb, 4:6, :] = ks_new[sb]
        rows_ref[sb, 6:8, :] = kv[sb][6:8]
        new_cols = jnp.concatenate([column_tile(kw_row[sb], nw // 128), column_tile(vw_row[sb], nw // 128)], axis=0)
        wn_ref[sb] = jnp.where(wlane == nw - 1, new_cols, pltpu.roll(wb_ref[sb], nw - 1, 1))


def _nsa_decode(cache_t, layer, page_table, win_t, q8, kv12, cg83, cz8, cos64, sin64, pos256, w1bd, w2bd,
                new_win_t):
    nb = q8.shape[0]
    nw = win_t.shape[3]
    past = N_PAGES * PAGE_SIZE
    n_slc = -(-(past + 1) // SLC_BLK)
    ov = jnp.asarray(_overlap_t(n_slc, 128).T)
    keys = np.arange(past)
    ex = jnp.asarray((np.arange(128)[:, None] == keys[None, :] // SLC_BLK).astype(np.float32)).astype(BF16)

    def page_spec(sb, i):
        return pl.BlockSpec((None, None, 512, PAGE_SIZE), lambda b, pt: (layer, pt[b * ND_SB + sb, i], 0, 0))

    c3 = lambda n, d: pl.BlockSpec((ND_SB, n, d), lambda b, pt: (b, 0, 0))
    z2 = lambda b, pt: (0, 0)
    grid_spec = pltpu.PrefetchScalarGridSpec(
        num_scalar_prefetch=1,
        grid=(nb // ND_SB,),
        in_specs=[page_spec(sb, i) for sb in range(ND_SB) for i in range(N_PAGES)] + [
            c3(C_HEADS, C_DH), c3(12, C_DH), c3(C_HEADS, 3), c3(C_HEADS, C_DH),
            pl.BlockSpec((None, ND_SB, 256, nw), lambda b, pt: (layer, b, 0, 0)),
            pl.BlockSpec((1, C_DH), z2), pl.BlockSpec((1, C_DH), z2),
            pl.BlockSpec((8, 512), z2),
            pl.BlockSpec((CMP_STRIDE * 256, 512), z2),
            pl.BlockSpec((256, 256), z2),
            pl.BlockSpec((128, 128), z2),
            pl.BlockSpec((128, past), z2),
            pl.BlockSpec(memory_space=pl.ANY)],
        out_specs=[c3(C_HEADS, C_DH), c3(C_HEADS, C_DH),
                   pl.BlockSpec((None, ND_SB, 256, nw), lambda b, pt: (layer, b, 0, 0))],
    )
    return pl.pallas_call(
        _nsa_decode_kernel,
        grid_spec=grid_spec,
        out_shape=[jax.ShapeDtypeStruct((nb, C_HEADS, C_DH), BF16),
                   jax.ShapeDtypeStruct((nb, C_HEADS, C_DH), F32),
                   jax.ShapeDtypeStruct(new_win_t.shape, F32)],
        input_output_aliases={1 + ND_SB * N_PAGES + 12: 2},
        compiler_params=_cparams("arbitrary"),
        name="nsa_decode",
    )(page_table, *([cache_t] * (ND_SB * N_PAGES)), q8, kv12, cg83, cz8, win_t, cos64, sin64, pos256, w1bd, w2bd,
      ov, ex, new_win_t)


def _merge_kernel(ya_ref, yb_ref, yc_ref, mg_ref, x_ref, gate_ref, wb_ref, wo_ref, lg_ref, lb_ref, o_ref):
    merged = None
    for n, y_ref in enumerate((ya_ref, yb_ref, yc_ref)):
        br = jnp.dot(y_ref[0], wb_ref[n], preferred_element_type=F32)
        term = _sigmoid(mg_ref[0, :, n * D_MODEL:(n + 1) * D_MODEL]) * br
        merged = term if merged is None else merged + term
    out = jnp.dot(merged.astype(BF16), wo_ref[...], preferred_element_type=F32)
    z = ALPHA * x_ref[0] + gate_ref[0] * out
    mu = jnp.mean(z, axis=-1, keepdims=True)
    zc = z - mu
    var = jnp.mean(zc * zc, axis=-1, keepdims=True)
    o_ref[0] = zc * lax.rsqrt(var + LN_EPS) * lg_ref[...] + lb_ref[...]


def _merge(ya, yb, yc, proj, x, gate, wb_bf, wo_bf, ln_g, ln_b, tm):
    B, T, _ = x.shape
    ys = pl.BlockSpec((1, tm, 512), lambda b, t: (b, t, 0))
    gs = (pl.BlockSpec((1, 1, D_MODEL), lambda b, t: (b, 0, 0)) if gate.shape[1] == 1
          else pl.BlockSpec((1, tm, D_MODEL), lambda b, t: (b, t, 0)))
    return pl.pallas_call(
        _merge_kernel,
        grid=(B, T // tm),
        in_specs=[ys, ys, ys,
                  pl.BlockSpec((1, tm, 3 * D_MODEL), lambda b, t: (b, t, COL_MG // (3 * D_MODEL))),
                  pl.BlockSpec((1, tm, D_MODEL), lambda b, t: (b, t, 0)),
                  gs,
                  pl.BlockSpec((3, 512, D_MODEL), lambda b, t: (0, 0, 0)),
                  pl.BlockSpec((D_MODEL, D_MODEL), lambda b, t: (0, 0)),
                  pl.BlockSpec((1, D_MODEL), lambda b, t: (0, 0)),
                  pl.BlockSpec((1, D_MODEL), lambda b, t: (0, 0))],
        out_specs=pl.BlockSpec((1, tm, D_MODEL), lambda b, t: (b, t, 0)),
        out_shape=jax.ShapeDtypeStruct((B, T, D_MODEL), F32),
        compiler_params=_cparams("arbitrary", "arbitrary"),
        name="merge",
    )(ya, yb, yc, proj, x, gate, wb_bf, wo_bf, ln_g, ln_b)


def _block_diag(blocks):
    n = len(blocks)
    z = jnp.zeros_like(blocks[0])
    return jnp.concatenate(
        [jnp.concatenate([b if j == i else z for j in range(n)], axis=-1) for i, b in enumerate(blocks)], axis=-2)


def _rope_tables(pos, width):
    inv = ROPE_THETA ** (-jnp.arange(0, C_DH, 2, dtype=F32) / C_DH)
    ang = pos.astype(F32)[:, None] * inv[None, :]
    cos = jnp.cos(ang)
    sin = jnp.sin(ang)
    reps = width // C_DH
    return (jnp.tile(jnp.concatenate([cos, cos], axis=1), (1, reps)),
            jnp.tile(jnp.concatenate([-sin, sin], axis=1), (1, reps)))


def _token_major(x_t, lead):
    kinds = x_t.shape[-2] // (C_KV_HEADS * C_DH)
    x = x_t.reshape(*x_t.shape[:-2], kinds, C_KV_HEADS, C_DH, x_t.shape[-1])
    return jnp.moveaxis(x, -1, lead)


def kernel(x_prompt, x_sample, c_prompt, c_sample, cache_nsa_kv, page_table, state_win_kv, state_hgrn,
           state_rglru, state_conv, ada_w, ada_b, w_in, a_lb, a_norm_g, b_conv_w, b_conv_b, b_gate_a_w,
           b_gate_a_b, b_gate_x_w, b_gate_x_b, b_lambda, c_cmp_pos, c_cmp_w1, c_cmp_w2, w_branch, w_out,
           ln_g, ln_b):
    bp, seq, _ = x_prompt.shape
    bs = x_sample.shape[0]
    n_pages = page_table.shape[1]
    n_pool = cache_nsa_kv.shape[1]
    nw = state_win_kv.shape[2]
    past_len = n_pages * cache_nsa_kv.shape[2]
    assert n_pages == N_PAGES and cache_nsa_kv.shape[2] == PAGE_SIZE and x_sample.shape[1] == 1

    lb_all = jnp.cumsum(jax.nn.softmax(a_lb.astype(F32), axis=0), axis=0)
    lower_bound = lb_all - lb_all[0:1]
    one_minus_lb = 1.0 - lower_bound
    w_in_t = jnp.swapaxes(w_in, 1, 2).astype(BF16)
    w_in_p = jnp.concatenate(
        [w_in_t[:, 4888:7960], w_in_t[:, 0:3584], w_in_t[:, 4376:4888], w_in_t[:, 3584:4352],
         w_in_t[:, 4352:4376], jnp.zeros((DEPTH, N_PROJ - 7960, D_MODEL), BF16)], axis=1)
    wb_bf = w_branch.astype(BF16)
    wo_bf = w_out.astype(BF16)
    cache_t = cache_nsa_kv.transpose(0, 1, 3, 4, 5, 2).reshape(DEPTH, n_pool, 512, PAGE_SIZE)
    win_t = state_win_kv.transpose(0, 1, 3, 4, 5, 2).reshape(DEPTH, bs, 256, nw)
    cos_p, sin_p = _rope_tables(jnp.arange(seq), 128)
    cos_s, sin_s = _rope_tables(past_len + jnp.arange(1), C_DH)

    ada = _ada_all(jnp.concatenate([c_prompt, c_sample], axis=0), ada_w, ada_b)

    xp, xs = x_prompt, x_sample.reshape(1, bs, D_MODEL)
    outs = {k: [] for k in ("nsa_s", "win_p", "hg_p", "rg_p", "rg_s", "cv_p", "cv_s")}
    nsa_p_all = jnp.zeros((DEPTH, bp, 512, seq), F32)
    win_s_all = jnp.zeros((DEPTH, bs, 256, nw), F32)
    hg_s_all = jnp.zeros(state_hgrn.shape, F32)
    for l in range(DEPTH):
        shift, scale, gate = ada[l, :, 0:D_MODEL], ada[l, :, D_MODEL:2 * D_MODEL], ada[l, :, 2 * D_MODEL:]
        rg_w = (b_conv_w[l], b_conv_b[l].reshape(1, -1),
                _block_diag([b_gate_a_w[l, i] for i in range(B_BLOCKS)]).astype(BF16), b_gate_a_b[l].reshape(1, -1),
                _block_diag([b_gate_x_w[l, i] for i in range(B_BLOCKS)]).astype(BF16), b_gate_x_b[l].reshape(1, -1),
                b_lambda[l].reshape(1, -1))
        pos_rows = jnp.concatenate([c_cmp_pos[l, 0], c_cmp_pos[l, 0], c_cmp_pos[l, 1], c_cmp_pos[l, 1]], axis=1)
        pos_rows = jnp.pad(pos_rows.reshape(2, CMP_STRIDE * 256), ((0, 6), (0, 0)))
        w1r = c_cmp_w1[l].reshape(2, CMP_LEN, C_DH, C_DH)
        w1bd = _block_diag([w1r[0], w1r[0], w1r[1], w1r[1]]).astype(BF16).reshape(2, CMP_STRIDE * 256, 256)
        w1bd = jnp.concatenate([w1bd[0], w1bd[1]], axis=1)
        pos256 = _compress_bias(pos_rows, w1bd)
        w2bd = _block_diag([c_cmp_w2[l, 0], c_cmp_w2[l, 0], c_cmp_w2[l, 1], c_cmp_w2[l, 1]]).astype(BF16)
        gain = a_norm_g[l].reshape(1, -1)
        lg, lbias = ln_g[l].reshape(1, -1), ln_b[l].reshape(1, -1)

        proj = _inproj(xp, scale[:bp, None], shift[:bp, None], w_in_p[l], 1024)
        ya, s_new = _hgrn_prompt(proj, lower_bound[l].reshape(1, -1), one_minus_lb[l].reshape(1, -1), gain)
        yb, h_new, cv_new = _rglru_prompt(proj, rg_w)
        nsa_p_all, wrow_t, raw, q_hm, qr_hm, ks_hm, kw_hm = _nsa_prep(proj, cos_p, sin_p, nsa_p_all, l)
        kc, vct = _compress_prompt(raw, pos256, w1bd, w2bd)
        yc = _nsa_prompt(proj, q_hm, qr_hm, kc, vct, ks_hm, nsa_p_all, l, kw_hm, wrow_t)
        xp = _merge(ya, yb, yc, proj, xp, gate[:bp, None], wb_bf[l], wo_bf[l], lg, lbias, 256)
        outs["win_p"].append(wrow_t[:, :, seq - min(WINDOW, seq):])
        outs["hg_p"].append(s_new)
        outs["rg_p"].append(h_new.reshape(bp, B_WIDTH))
        outs["cv_p"].append(cv_new)

        proj_s = _inproj(xs, scale[None, bp:], shift[None, bp:], w_in_p[l], bs).reshape(bs, N_PROJ)
        hg_s_all, ya_s = _hgrn_decode(state_hgrn, l, proj_s, lower_bound[l].reshape(A_HEADS, A_DK, 1),
                                      one_minus_lb[l].reshape(A_HEADS, A_DK, 1), gain, hg_s_all)
        yb_s, h_s, cv_s = _rglru_decode(proj_s, state_conv[l].reshape(bs, -1), state_rglru[l], rg_w)
        q8 = proj_s[:, COL_CQ:COL_CQ + 512].reshape(bs, C_HEADS, C_DH)
        cz8 = proj_s[:, COL_CZ:COL_CZ + 512].reshape(bs, C_HEADS, C_DH)
        kv12 = proj_s[:, COL_CKV:COL_CKV + 768].reshape(bs, 12, C_DH)
        cg83 = proj_s[:, COL_CG:COL_CG + 3 * C_HEADS].reshape(bs, 3, C_HEADS).transpose(0, 2, 1)
        yc_s, rows_s, win_s_all = _nsa_decode(cache_t, l, page_table, win_t, q8, kv12, cg83, cz8, cos_s, sin_s,
                                              pos256, w1bd, w2bd, win_s_all)
        xs = _merge(ya_s.reshape(1, bs, 512), yb_s.reshape(1, bs, 512), yc_s.reshape(1, bs, 512),
                    proj_s.reshape(1, bs, N_PROJ), xs, gate[None, bp:], wb_bf[l], wo_bf[l], lg, lbias, bs)
        outs["nsa_s"].append(rows_s.reshape(bs, 1, 4, C_KV_HEADS, C_DH))
        outs["rg_s"].append(h_s)
        outs["cv_s"].append(cv_s.reshape(bs, CONV_W - 1, B_WIDTH))

    st = {k: jnp.stack(v, axis=0) for k, v in outs.items()}
    return (xp, xs.reshape(bs, 1, D_MODEL), _token_major(nsa_p_all, 2), st["nsa_s"],
            _token_major(st["win_p"], 2), _token_major(win_s_all, 2),
            st["hg_p"], hg_s_all, st["rg_p"], st["rg_s"], st["cv_p"], st["cv_s"])
```

```python
import numpy as np
import jax
import jax.numpy as jnp
from jax import lax
from jax.experimental import pallas as pl
from jax.experimental.pallas import tpu as pltpu

F32 = jnp.float32
BF16 = jnp.bfloat16

D_MODEL = 1024
DEPTH = 4
PAGE_SIZE = 128
A_HEADS = 4
A_DK = 128
A_CHUNK = 64
B_WIDTH = 512
B_BLOCKS = 8
CONV_W = 4
LRU_C = 8.0
C_HEADS = 8
C_KV_HEADS = 2
C_GROUP = 4
C_DH = 64
CMP_LEN = 32
CMP_STRIDE = 16
SLC_BLK = 64
SLC_TOPN = 16
WINDOW = 512
ROPE_THETA = 10000.0
ATT_SCALE = C_DH ** -0.5
LOG2_E = 1.4426950408889634
ALPHA = (2 * DEPTH) ** 0.25
LN_EPS = 1e-5
NEG_INF = -1e30
FORCED_SCORE = 1e9

COL_MG, COL_AQ, COL_AF, COL_AI, COL_AZ = 0, 3072, 3584, 4096, 4608
COL_BX, COL_BZ, COL_CQ, COL_CZ, COL_CKV, COL_CG = 5120, 5632, 6144, 6656, 7168, 7936
N_PROJ = 8064
PROJ_TN = 2688

VMEM_LIMIT = 56 * 1024 * 1024

NT_DIMS = (((1,), (1,)), ((), ()))
TN_DIMS = (((0,), (0,)), ((), ()))


def _cparams(*sem):
    return pltpu.CompilerParams(dimension_semantics=sem, vmem_limit_bytes=VMEM_LIMIT)


def _sigmoid(x):
    return jax.nn.sigmoid(x)


def _silu(x):
    return x * jax.nn.sigmoid(x)


def _forget_gate(af, lb, one_minus_lb):
    e = jnp.exp(-jnp.abs(af))
    r = 1.0 / (1.0 + e)
    er = e * r
    pos = af >= 0.0
    return lb + one_minus_lb * jnp.where(pos, r, er), one_minus_lb * jnp.where(pos, er, r)


def _ada_kernel(c_ref, w_ref, b_ref, o_ref):
    o_ref[0] = jnp.dot(c_ref[...], w_ref[0].astype(BF16), preferred_element_type=F32) + b_ref[0]


def _ada_all(c_all, ada_w, ada_b):
    nb = c_all.shape[0]
    return pl.pallas_call(
        _ada_kernel,
        grid=(DEPTH, 3),
        in_specs=[
            pl.BlockSpec((nb, D_MODEL), lambda l, j: (0, 0)),
            pl.BlockSpec((1, D_MODEL, D_MODEL), lambda l, j: (l, 0, j)),
            pl.BlockSpec((1, 1, D_MODEL), lambda l, j: (l, 0, j)),
        ],
        out_specs=pl.BlockSpec((1, nb, D_MODEL), lambda l, j: (l, 0, j)),
        out_shape=jax.ShapeDtypeStruct((DEPTH, nb, 3 * D_MODEL), F32),
        compiler_params=_cparams("arbitrary", "arbitrary"),
        name="ada",
    )(c_all.astype(BF16), ada_w, ada_b.reshape(DEPTH, 1, 3 * D_MODEL))


def _inproj_kernel(x_ref, sc_ref, sh_ref, w_ref, o_ref, h_ref):
    @pl.when(pl.program_id(2) == 0)
    def _():
        h_ref[...] = (x_ref[0] * (1.0 + sc_ref[0]) + sh_ref[0]).astype(BF16)

    o_ref[0] = lax.dot_general(h_ref[...], w_ref[...], NT_DIMS, preferred_element_type=F32)


def _inproj(x, scale, shift, w_bf, tt):
    B, T, _ = x.shape
    ts = scale.shape[1]
    mod_spec = (pl.BlockSpec((1, 1, D_MODEL), lambda b, t, n: (b, 0, 0)) if ts == 1
                else pl.BlockSpec((1, tt, D_MODEL), lambda b, t, n: (b, t, 0)))
    return pl.pallas_call(
        _inproj_kernel,
        grid=(B, T // tt, N_PROJ // PROJ_TN),
        in_specs=[
            pl.BlockSpec((1, tt, D_MODEL), lambda b, t, n: (b, t, 0)),
            mod_spec, mod_spec,
            pl.BlockSpec((PROJ_TN, D_MODEL), lambda b, t, n: (n, 0)),
        ],
        out_specs=pl.BlockSpec((1, tt, PROJ_TN), lambda b, t, n: (b, t, n)),
        out_shape=jax.ShapeDtypeStruct((B, T, N_PROJ), F32),
        scratch_shapes=[pltpu.VMEM((tt, D_MODEL), BF16)],
        compiler_params=_cparams("arbitrary", "arbitrary", "arbitrary"),
        name="inproj",
    )(x, scale, shift, w_bf)


HG_TT = 256
HG_HP = 4
HG_LEVELS = (1, 2, 4, 8, 16, 32)


def _hgrn_level_table():
    t = np.arange(HG_TT)[:, None]
    s = np.arange(HG_TT)[None, :]
    lvl = np.full((HG_TT, HG_TT), -1, np.int32)
    same = (t // A_CHUNK) == (s // A_CHUNK)
    for i, h in enumerate(HG_LEVELS):
        m = same & ((t // h) == (s // h) + 1) & (((s // h) % 2) == 0)
        lvl[m] = i
    lvl[t == s] = len(HG_LEVELS)
    return lvl


def _hgrn_prompt_kernel(aq_ref, af_ref, ai_ref, az_ref, lb_ref, oml_ref, g_ref, lvl_ref,
                        ya_ref, so_ref, s_ref):
    t = pl.program_id(2)

    @pl.when(t == 0)
    def _():
        s_ref[...] = jnp.zeros_like(s_ref)

    heads = range(HG_HP)
    cols = [slice(hh * A_DK, (hh + 1) * A_DK) for hh in heads]
    lvl = lvl_ref[...]
    row = lax.broadcasted_iota(jnp.int32, (HG_TT, A_DK), 0)
    q = [_silu(aq_ref[0, :, cs]) for cs in cols]
    gates = [_forget_gate(af_ref[0, :, cs], lb_ref[:, cs], oml_ref[:, cs]) for cs in cols]
    k = [gk for _, gk in gates]
    vb = [ai_ref[0, :, cs].astype(BF16) for cs in cols]

    cin = [jnp.log(gf) for gf, _ in gates]
    rex = [jnp.zeros_like(c) for c in cin]
    att = [jnp.where(lvl == len(HG_LEVELS), jnp.sum(q[hh] * k[hh], axis=-1, keepdims=True), 0.0) for hh in heads]
    for i, h in enumerate(HG_LEVELS):
        upper = ((row // h) % 2) == 1
        qh = [(q[hh] * jnp.exp(cin[hh])).astype(BF16) for hh in heads]
        kh = [(k[hh] * jnp.exp(rex[hh])).astype(BF16) for hh in heads]
        a = [lax.dot_general(qh[hh], kh[hh], NT_DIMS, preferred_element_type=F32) for hh in heads]
        for hh in heads:
            att[hh] = jnp.where(lvl == i, a[hh], att[hh])
            tot = cin[hh] + rex[hh]
            cin[hh] = cin[hh] + jnp.where(upper, pltpu.roll(tot, h, 0), 0.0)
            rex[hh] = rex[hh] + jnp.where(upper, 0.0, pltpu.roll(tot, HG_TT - h, 0))

    o = [jnp.dot(att[hh].astype(BF16), vb[hh], preferred_element_type=F32) for hh in heads]
    qb = [(q[hh] * jnp.exp(cin[hh])).astype(BF16) for hh in heads]
    kd = [(k[hh] * jnp.exp(rex[hh])).astype(BF16) for hh in heads]
    tot = [cin[hh] + rex[hh] for hh in heads]
    s = [s_ref[hh] for hh in heads]
    inter = [[] for _ in heads]
    for c in range(HG_TT // A_CHUNK):
        sl = slice(c * A_CHUNK, (c + 1) * A_CHUNK)
        for hh in heads:
            inter[hh].append(jnp.dot(qb[hh][sl], s[hh].astype(BF16), preferred_element_type=F32))
        u = [lax.dot_general(kd[hh][sl], vb[hh][sl], TN_DIMS, preferred_element_type=F32) for hh in heads]
        for hh in heads:
            e_rows = jnp.broadcast_to(jnp.exp(tot[hh][c * A_CHUNK:c * A_CHUNK + 1]), (A_DK, A_DK))
            s[hh] = e_rows.T * s[hh] + u[hh]

    for hh, cs in enumerate(cols):
        oh = o[hh] + jnp.concatenate(inter[hh], axis=0)
        ms = jnp.mean(oh * oh, axis=-1, keepdims=True)
        ya_ref[0, :, cs] = ((oh * lax.rsqrt(ms + LN_EPS)) * g_ref[:, cs] * _silu(az_ref[0, :, cs])).astype(BF16)
        s_ref[hh] = s[hh]

    @pl.when(t == pl.num_programs(2) - 1)
    def _():
        so_ref[0] = s_ref[...]


def _hgrn_prompt(proj, lower_bound, one_minus_lb, gain):
    B, T, _ = proj.shape
    lvl = jnp.asarray(_hgrn_level_table())
    width = HG_HP * A_DK

    def col(base):
        return pl.BlockSpec((1, HG_TT, width), lambda b, h, t: (b, t, base // width + h))

    vec = pl.BlockSpec((1, width), lambda b, h, t: (0, h))
    return pl.pallas_call(
        _hgrn_prompt_kernel,
        grid=(B, A_HEADS // HG_HP, T // HG_TT),
        in_specs=[col(COL_AQ), col(COL_AF), col(COL_AI), col(COL_AZ), vec, vec, vec,
                  pl.BlockSpec((HG_TT, HG_TT), lambda b, h, t: (0, 0))],
        out_specs=[pl.BlockSpec((1, HG_TT, width), lambda b, h, t: (b, t, h)),
                   pl.BlockSpec((1, HG_HP, A_DK, A_DK), lambda b, h, t: (b, h, 0, 0))],
        out_shape=[jax.ShapeDtypeStruct((B, T, A_HEADS * A_DK), BF16),
                   jax.ShapeDtypeStruct((B, A_HEADS, A_DK, A_DK), F32)],
        scratch_shapes=[pltpu.VMEM((HG_HP, A_DK, A_DK), F32)],
        compiler_params=_cparams("arbitrary", "arbitrary", "arbitrary"),
        name="hgrn_prompt",
    )(proj, proj, proj, proj, lower_bound, one_minus_lb, gain, lvl)


HG_SB = 16


def _hgrn_decode_kernel(s_ref, aqc_ref, afc_ref, ai_ref, az_ref, lb_ref, oml_ref, g_ref, so_ref, ya_ref):
    aqc = aqc_ref[0, 0]
    qc = _silu(aqc)
    ec, kc = _forget_gate(afc_ref[0, 0], lb_ref[0], oml_ref[0])
    v = ai_ref[...]
    rows = []
    for r in range(HG_SB):
        sn = ec[:, r:r + 1] * s_ref[r, 0] + kc[:, r:r + 1] * v[r:r + 1, :]
        so_ref[r, 0] = sn
        rows.append(jnp.sum(qc[:, r:r + 1] * sn, axis=0, keepdims=True))
    o = jnp.concatenate(rows, axis=0)
    ms = jnp.mean(o * o, axis=-1, keepdims=True)
    ya_ref[...] = ((o * lax.rsqrt(ms + LN_EPS)) * g_ref[...] * _silu(az_ref[...])).astype(BF16)


def _skip_ref(kernel_fn, pos):
    def wrapped(*refs):
        return kernel_fn(*refs[:pos], *refs[pos + 1:])

    return wrapped


def _hgrn_decode(state_all, layer, proj_s, lower_bound_c, one_minus_lb_c, gain, new_state_all):
    nb = proj_s.shape[0]
    nblk = nb // HG_SB

    def cols(base):
        a = proj_s[:, base:base + A_HEADS * A_DK].reshape(nblk, HG_SB, A_HEADS, A_DK)
        return a.transpose(2, 0, 3, 1)

    st_spec = pl.BlockSpec((None, HG_SB, 1, A_DK, A_DK), lambda h, i: (layer, i, h, 0, 0))
    col_spec = pl.BlockSpec((1, 1, A_DK, HG_SB), lambda h, i: (h, i, 0, 0))
    cvec = pl.BlockSpec((1, A_DK, 1), lambda h, i: (h, 0, 0))
    return pl.pallas_call(
        _skip_ref(_hgrn_decode_kernel, 8),
        grid=(A_HEADS, nblk),
        in_specs=[st_spec, col_spec, col_spec,
                  pl.BlockSpec((HG_SB, A_DK), lambda h, i: (i, COL_AI // A_DK + h)),
                  pl.BlockSpec((HG_SB, A_DK), lambda h, i: (i, COL_AZ // A_DK + h)),
                  cvec, cvec,
                  pl.BlockSpec((1, A_DK), lambda h, i: (0, h)),
                  pl.BlockSpec(memory_space=pl.ANY)],
        out_specs=[st_spec, pl.BlockSpec((HG_SB, A_DK), lambda h, i: (i, h))],
        out_shape=[jax.ShapeDtypeStruct(new_state_all.shape, F32),
                   jax.ShapeDtypeStruct((nb, A_HEADS * A_DK), BF16)],
        input_output_aliases={8: 0},
        compiler_params=_cparams("arbitrary", "arbitrary"),
        name="hgrn_decode",
    )(state_all, cols(COL_AQ), cols(COL_AF), proj_s, proj_s, lower_bound_c, one_minus_lb_c, gain, new_state_all)


RG_TT = 256


def _rglru_gates(u, ga_ref, gab_ref, gx_ref, gxb_ref, lam_ref):
    ub = u.astype(BF16)
    r = _sigmoid(jnp.dot(ub, ga_ref[...], preferred_element_type=F32) + gab_ref[...])
    ig = _sigmoid(jnp.dot(ub, gx_ref[...], preferred_element_type=F32) + gxb_ref[...])
    nl = -lam_ref[...]
    softplus = jnp.maximum(nl, 0.0) + jnp.log1p(jnp.exp(-jnp.abs(nl)))
    a = jnp.exp(-LRU_C * r * softplus)
    return a, jnp.sqrt(1.0 - a * a) * (ig * u)


def _rglru_prompt_kernel(bx_ref, bz_ref, cw_ref, cb_ref, ga_ref, gab_ref, gx_ref, gxb_ref, lam_ref,
                         yb_ref, h_ref, cv_ref, ext_ref, hc_ref):
    t = pl.program_id(1)

    @pl.when(t == 0)
    def _():
        ext_ref[0:8, :] = jnp.zeros((8, B_WIDTH), F32)
        hc_ref[...] = jnp.zeros_like(hc_ref)

    x = bx_ref[0]
    ext_ref[8:8 + RG_TT, :] = x
    u = cb_ref[...] + (ext_ref[5:5 + RG_TT, :] * cw_ref[0:1, :] + ext_ref[6:6 + RG_TT, :] * cw_ref[1:2, :]
                       + ext_ref[7:7 + RG_TT, :] * cw_ref[2:3, :] + x * cw_ref[3:4, :])
    ext_ref[0:8, :] = x[RG_TT - 8:RG_TT, :]
    a, bt = _rglru_gates(u, ga_ref, gab_ref, gx_ref, gxb_ref, lam_ref)
    row = lax.broadcasted_iota(jnp.int32, (RG_TT, 128), 0)
    z = bz_ref[0]
    for c in range(B_WIDTH // 128):
        cs = slice(c * 128, (c + 1) * 128)
        ac = a[:, cs]
        bc = bt[:, cs] + jnp.where(row == 0, ac * hc_ref[0:1, cs], 0.0)
        s = 1
        while s < RG_TT:
            keep = row >= s
            bc = jnp.where(keep, ac * pltpu.roll(bc, s, 0) + bc, bc)
            ac = jnp.where(keep, ac * pltpu.roll(ac, s, 0), ac)
            s *= 2
        hc_ref[0:1, cs] = bc[RG_TT - 1:RG_TT, :]
        yb_ref[0, :, cs] = (bc * _silu(z[:, cs])).astype(BF16)

    @pl.when(t == pl.num_programs(1) - 1)
    def _():
        h_ref[0] = hc_ref[0:1, :]
        cv_ref[0] = x[RG_TT - (CONV_W - 1):RG_TT, :]


def _rglru_weight_specs(idx):
    return [pl.BlockSpec((CONV_W, B_WIDTH), idx), pl.BlockSpec((1, B_WIDTH), idx),
            pl.BlockSpec((B_WIDTH, B_WIDTH), idx), pl.BlockSpec((1, B_WIDTH), idx),
            pl.BlockSpec((B_WIDTH, B_WIDTH), idx), pl.BlockSpec((1, B_WIDTH), idx),
            pl.BlockSpec((1, B_WIDTH), idx)]


def _rglru_prompt(proj, w):
    B, T, _ = proj.shape
    return pl.pallas_call(
        _rglru_prompt_kernel,
        grid=(B, T // RG_TT),
        in_specs=[pl.BlockSpec((1, RG_TT, B_WIDTH), lambda b, t: (b, t, COL_BX // B_WIDTH)),
                  pl.BlockSpec((1, RG_TT, B_WIDTH), lambda b, t: (b, t, COL_BZ // B_WIDTH))]
        + _rglru_weight_specs(lambda b, t: (0, 0)),
        out_specs=[pl.BlockSpec((1, RG_TT, B_WIDTH), lambda b, t: (b, t, 0)),
                   pl.BlockSpec((1, 1, B_WIDTH), lambda b, t: (b, 0, 0)),
                   pl.BlockSpec((1, CONV_W - 1, B_WIDTH), lambda b, t: (b, 0, 0))],
        out_shape=[jax.ShapeDtypeStruct((B, T, B_WIDTH), BF16),
                   jax.ShapeDtypeStruct((B, 1, B_WIDTH), F32),
                   jax.ShapeDtypeStruct((B, CONV_W - 1, B_WIDTH), F32)],
        scratch_shapes=[pltpu.VMEM((8 + RG_TT, B_WIDTH), F32), pltpu.VMEM((8, B_WIDTH), F32)],
        compiler_params=_cparams("arbitrary", "arbitrary"),
        name="rglru_prompt",
    )(proj, proj, *w)


def _rglru_decode_kernel(bx_ref, bz_ref, cv0_ref, h0_ref, cw_ref, cb_ref, ga_ref, gab_ref, gx_ref, gxb_ref,
                         lam_ref, yb_ref, h_ref, cv_ref):
    x = bx_ref[...]
    c0 = cv0_ref[:, 0:B_WIDTH]
    c1 = cv0_ref[:, B_WIDTH:2 * B_WIDTH]
    c2 = cv0_ref[:, 2 * B_WIDTH:3 * B_WIDTH]
    u = cb_ref[...] + (c0 * cw_ref[0:1, :] + c1 * cw_ref[1:2, :] + c2 * cw_ref[2:3, :] + x * cw_ref[3:4, :])
    a, bt = _rglru_gates(u, ga_ref, gab_ref, gx_ref, gxb_ref, lam_ref)
    h = a * h0_ref[...] + bt
    h_ref[...] = h
    yb_ref[...] = (h * _silu(bz_ref[...])).astype(BF16)
    cv_ref[:, 0:B_WIDTH] = c1
    cv_ref[:, B_WIDTH:2 * B_WIDTH] = c2
    cv_ref[:, 2 * B_WIDTH:3 * B_WIDTH] = x


def _rglru_decode(proj_s, conv0, h0, w):
    nb = proj_s.shape[0]
    z2 = lambda i: (0, 0)
    return pl.pallas_call(
        _rglru_decode_kernel,
        grid=(1,),
        in_specs=[pl.BlockSpec((nb, B_WIDTH), lambda i: (0, COL_BX // B_WIDTH)),
                  pl.BlockSpec((nb, B_WIDTH), lambda i: (0, COL_BZ // B_WIDTH)),
                  pl.BlockSpec((nb, 3 * B_WIDTH), z2), pl.BlockSpec((nb, B_WIDTH), z2)]
        + _rglru_weight_specs(z2),
        out_specs=[pl.BlockSpec((nb, B_WIDTH), z2), pl.BlockSpec((nb, B_WIDTH), z2),
                   pl.BlockSpec((nb, 3 * B_WIDTH), z2)],
        out_shape=[jax.ShapeDtypeStruct((nb, B_WIDTH), BF16), jax.ShapeDtypeStruct((nb, B_WIDTH), F32),
                   jax.ShapeDtypeStruct((nb, 3 * B_WIDTH), F32)],
        compiler_params=_cparams("arbitrary"),
        name="rglru_decode",
    )(proj_s, proj_s, conv0, h0, *w)


def _gelu_tanh(x):
    return 0.5 * x * (1.0 + jnp.tanh(0.7978845608028654 * (x + 0.044715 * (x * x * x))))


def _compress(load_l, bias_ref, w1_ref, w2_ref, m):
    x = jnp.concatenate([load_l(l).astype(BF16) for l in range(CMP_STRIDE)], axis=1)
    acc = jnp.dot(x, w1_ref[...], preferred_element_type=F32)
    acc_a = acc[:, 0:256] + bias_ref[0:1, 0:256]
    acc_b = acc[:, 256:512] + bias_ref[1:2, 256:512]
    hid = _gelu_tanh(acc_a + pltpu.roll(acc_b, m - 1, 0))
    return jnp.dot(hid.astype(BF16), w2_ref[...], preferred_element_type=F32)


def _compress_bias_kernel(pos_ref, w1_ref, o_ref):
    o_ref[...] = jnp.dot(pos_ref[...], w1_ref[...].astype(F32), precision=lax.Precision.HIGHEST,
                         preferred_element_type=F32)


def _compress_bias(pos_rows, w1cat):
    return pl.pallas_call(
        _compress_bias_kernel,
        grid=(1,),
        in_specs=[pl.BlockSpec(pos_rows.shape, lambda i: (0, 0)), pl.BlockSpec(w1cat.shape, lambda i: (0, 0))],
        out_specs=pl.BlockSpec((8, 512), lambda i: (0, 0)),
        out_shape=jax.ShapeDtypeStruct((8, 512), F32),
        compiler_params=_cparams("arbitrary"),
        name="compress_bias",
    )(pos_rows, w1cat)


def _rank_select(score, idx, n_rows, axis):
    rank = jnp.zeros_like(score)
    for jp in range(n_rows):
        other = score[jp:jp + 1, :] if axis == 0 else score[:, jp:jp + 1]
        rank = rank + jnp.where(other > score, 1.0, jnp.where(other == score, jnp.where(idx > jp, 1.0, 0.0), 0.0))
    return jnp.where(rank < float(SLC_TOPN), 1.0, 0.0)


def _forced_score(idx, cur, imp):
    return jnp.where(idx == 0, FORCED_SCORE,
                     jnp.where(idx == cur, FORCED_SCORE,
                               jnp.where(idx == cur - 1, FORCED_SCORE, jnp.where(idx <= cur, imp, -1.0))))


NP_TT = 512


def _nsa_prep_kernel(cq_ref, kv01_ref, kv23_ref, kv45_ref, cos_ref, sin_ref,
                     rows_t_ref, win_t_ref, raw_ref, q_ref, qr_ref, ks_ref, kw_ref):
    cos = cos_ref[...]
    sin = sin_ref[...]
    lane = lax.broadcasted_iota(jnp.int32, cos.shape, 1)
    first = (lane % C_DH) < (C_DH // 2)

    def rope(x):
        return x * cos + jnp.where(first, pltpu.roll(x, 128 - C_DH // 2, 1), pltpu.roll(x, C_DH // 2, 1)) * sin

    def heads(ref, x, h0):
        ref[0, h0] = x[:, 0:C_DH].astype(BF16)
        ref[0, h0 + 1] = x[:, C_DH:2 * C_DH].astype(BF16)

    cq = cq_ref[0]
    for c in range(C_HEADS // 2):
        x = cq[:, c * 128:(c + 1) * 128]
        heads(q_ref, x * ATT_SCALE, 2 * c)
        heads(qr_ref, rope(x * (ATT_SCALE * LOG2_E)), 2 * c)
    kv01 = kv01_ref[0]
    kv23 = kv23_ref[0]
    kv45 = kv45_ref[0]
    ks = rope(kv23[:, 0:128])
    kw = rope(kv45[:, 0:128])
    raw_ref[0] = kv01
    heads(ks_ref, ks, 0)
    heads(kw_ref, kw, 0)
    rows_t_ref[0, 0:128, :] = kv01[:, 0:128].T
    rows_t_ref[0, 128:256, :] = kv01[:, 128:256].T
    rows_t_ref[0, 256:384, :] = ks.T
    rows_t_ref[0, 384:512, :] = kv23[:, 128:256].T
    win_t_ref[0, 0:128, :] = kw.T
    win_t_ref[0, 128:256, :] = kv45[:, 128:256].T


def _nsa_prep(proj, cos128, sin128, rows_all, layer):
    B, T, _ = proj.shape
    hm = lambda n: pl.BlockSpec((1, n, NP_TT, C_DH), lambda b, t: (b, 0, t, 0))
    hs = lambda n: jax.ShapeDtypeStruct((B, n, T, C_DH), BF16)
    kvb = COL_CKV // 256
    return pl.pallas_call(
        _skip_ref(_nsa_prep_kernel, 6),
        grid=(B, T // NP_TT),
        in_specs=[pl.BlockSpec((1, NP_TT, 512), lambda b, t: (b, t, COL_CQ // 512)),
                  pl.BlockSpec((1, NP_TT, 256), lambda b, t: (b, t, kvb)),
                  pl.BlockSpec((1, NP_TT, 256), lambda b, t: (b, t, kvb + 1)),
                  pl.BlockSpec((1, NP_TT, 256), lambda b, t: (b, t, kvb + 2)),
                  pl.BlockSpec((NP_TT, 128), lambda b, t: (t, 0)),
                  pl.BlockSpec((NP_TT, 128), lambda b, t: (t, 0)),
                  pl.BlockSpec(memory_space=pl.ANY)],
        out_specs=[pl.BlockSpec((None, 1, 512, NP_TT), lambda b, t: (layer, b, 0, t)),
                   pl.BlockSpec((1, 256, NP_TT), lambda b, t: (b, 0, t)),
                   pl.BlockSpec((1, NP_TT, 256), lambda b, t: (b, t, 0)),
                   hm(8), hm(8), hm(2), hm(2)],
        out_shape=[jax.ShapeDtypeStruct(rows_all.shape, F32), jax.ShapeDtypeStruct((B, 256, T), F32),
                   jax.ShapeDtypeStruct((B, T, 256), F32), hs(8), hs(8), hs(2), hs(2)],
        input_output_aliases={6: 0},
        compiler_params=_cparams("arbitrary", "arbitrary"),
        name="nsa_prep",
    )(proj, proj, proj, proj, cos128, sin128, rows_all)


def _compress_prompt_kernel(xk_ref, xv_ref, pos_ref, w1_ref, w2_ref, kc_ref, vct_ref):
    def load_l(l):
        return jnp.concatenate([xk_ref[pl.ds(l, 128, stride=CMP_STRIDE), :],
                                xv_ref[pl.ds(l, 128, stride=CMP_STRIDE), :]], axis=1)

    out = _compress(load_l, pos_ref, w1_ref, w2_ref, 128)
    vt = out[:, 128:256].T
    for g in range(C_KV_HEADS):
        kc_ref[0, g] = out[:, g * C_DH:(g + 1) * C_DH].astype(BF16)
        vct_ref[0, g] = vt[g * C_DH:(g + 1) * C_DH, :].astype(BF16)


def _compress_prompt(raw, pos256, w1bd, w2bd):
    B, T, _ = raw.shape
    return pl.pallas_call(
        _compress_prompt_kernel,
        grid=(B,),
        in_specs=[pl.BlockSpec((None, T, 128), lambda b: (b, 0, 0)),
                  pl.BlockSpec((None, T, 128), lambda b: (b, 0, 1)),
                  pl.BlockSpec((8, 512), lambda b: (0, 0)),
                  pl.BlockSpec((CMP_STRIDE * 256, 512), lambda b: (0, 0)),
                  pl.BlockSpec((256, 256), lambda b: (0, 0))],
        out_specs=[pl.BlockSpec((1, C_KV_HEADS, 128, C_DH), lambda b: (b, 0, 0, 0)),
                   pl.BlockSpec((1, C_KV_HEADS, C_DH, 128), lambda b: (b, 0, 0, 0))],
        out_shape=[jax.ShapeDtypeStruct((B, C_KV_HEADS, 128, C_DH), BF16),
                   jax.ShapeDtypeStruct((B, C_KV_HEADS, C_DH, 128), BF16)],
        compiler_params=_cparams("arbitrary"),
        name="compress_prompt",
    )(raw, raw, pos256, w1bd, w2bd)


NA_TQ = 256
NA_KT = 256
NA_M = C_GROUP * NA_TQ
FLASH_ACC_ROWS = C_DH + 16


def _lanes4(x):
    return jnp.concatenate([x] * C_GROUP, axis=1)


def _flash_t(qs, k_ref, vt_ref, j_lo, j_hi, bias_fn, s_ref, p_ref, acc_ref):
    groups = range(C_KV_HEADS)
    ones_rows = jnp.ones((FLASH_ACC_ROWS - C_DH, NA_KT), BF16)

    def scores(j, g):
        off = pl.multiple_of(j * NA_KT, NA_KT)
        return lax.dot_general(k_ref[0, g, pl.ds(off, NA_KT), :], qs[g], NT_DIMS, preferred_element_type=F32)

    def values(j, g):
        off = pl.multiple_of(j * NA_KT, NA_KT)
        vt = vt_ref[0, g * C_DH:(g + 1) * C_DH, pl.ds(off, NA_KT)].astype(BF16)
        return jnp.dot(jnp.concatenate([vt, ones_rows], axis=0), p_ref[g], preferred_element_type=F32)

    acc_ref[...] = jnp.zeros_like(acc_ref)
    p_ref[...] = jnp.zeros_like(p_ref)
    for g in groups:
        s_ref[g] = scores(j_lo, g)
    row = lambda v: tuple(jnp.full((1, NA_M), v, F32) for _ in groups)

    def body(j, carry):
        m, alpha = carry
        j_next = jnp.minimum(j + 1, j_hi - 1)
        j_prev = jnp.maximum(j - 1, j_lo)
        m_out, a_out = [], []
        for g in groups:
            s = s_ref[g] + _lanes4(bias_fn(j, g))
            s_ref[g] = scores(j_next, g)
            acc_ref[g] = alpha[g] * acc_ref[g] + values(j_prev, g)
            m_new = jnp.maximum(m[g], jnp.max(s, axis=0, keepdims=True))
            p_ref[g] = jnp.exp2(s - m_new).astype(BF16)
            m_out.append(m_new)
            a_out.append(jnp.exp2(m[g] - m_new))
        return tuple(m_out), tuple(a_out)

    m, alpha = lax.fori_loop(j_lo, j_hi, body, (row(NEG_INF), row(1.0)))
    outs = []
    for g in groups:
        acc = alpha[g] * acc_ref[g] + values(j_hi - 1, g)
        outs.append(acc[0:C_DH] / acc[C_DH:C_DH + 1])
    return outs


def _nsa_prompt_kernel(q_ref, qr_ref, kc_ref, vct_ref, ks_ref, vst_ref, kw_ref, vwt_ref, cg_ref, cz_ref, ovt_ref,
                       yc_ref, s_ref, p_ref, acc_ref, sel_ref):
    qi = pl.program_id(1)
    t0 = qi * NA_TQ
    n_slc = ovt_ref.shape[0]
    sig_t = _sigmoid(cg_ref[0]).T
    cmp_row = lax.broadcasted_iota(jnp.int32, (128, NA_TQ), 0)
    cmp_q = t0 + lax.broadcasted_iota(jnp.int32, (128, NA_TQ), 1)
    cmp_ok = _lanes4(jnp.where(CMP_STRIDE * cmp_row + (CMP_LEN - 1) <= cmp_q, 1.0, 0.0)) > 0.5
    jidx = lax.broadcasted_iota(jnp.int32, (n_slc, NA_TQ), 0)
    cur = (t0 + lax.broadcasted_iota(jnp.int32, (n_slc, NA_TQ), 1)) // SLC_BLK
    groups = range(C_KV_HEADS)
    qrs = [qr_ref[0, C_GROUP * g:C_GROUP * (g + 1)].reshape(NA_M, C_DH) for g in groups]
    s = [lax.dot_general(kc_ref[0, g], q_ref[0, C_GROUP * g:C_GROUP * (g + 1)].reshape(NA_M, C_DH), NT_DIMS,
                         preferred_element_type=F32) for g in groups]
    p = []
    for g in groups:
        sm = jnp.where(cmp_ok, s[g], NEG_INF)
        e = jnp.where(cmp_ok, jnp.exp(sm - jnp.max(sm, axis=0, keepdims=True)), 0.0)
        p.append(e / jnp.maximum(jnp.sum(e, axis=0, keepdims=True), 1e-30))
    o_cmp = [jnp.dot(vct_ref[0, g], p[g].astype(BF16), preferred_element_type=F32) for g in groups]
    imp = [jnp.dot(ovt_ref[...],
                   p[g][:, 0:NA_TQ] + p[g][:, NA_TQ:2 * NA_TQ] + p[g][:, 2 * NA_TQ:3 * NA_TQ] + p[g][:, 3 * NA_TQ:],
                   precision=lax.Precision.HIGHEST, preferred_element_type=F32) for g in groups]
    for g in groups:
        sel_t = _rank_select(_forced_score(jidx, cur, imp[g]), jidx, n_slc, 0)
        sel_bias = jnp.where(sel_t > 0.5, 0.0, NEG_INF)
        for jj in range(n_slc):
            sel_ref[g, jj] = sel_bias[jj:jj + 1, :]

    key_row = lax.broadcasted_iota(jnp.int32, (NA_KT, NA_TQ), 0)
    q_pos = t0 + lax.broadcasted_iota(jnp.int32, (NA_KT, NA_TQ), 1)
    blocks_per_tile = NA_KT // SLC_BLK

    def slc_mask(j, g):
        chosen = sel_ref[g, blocks_per_tile * j + blocks_per_tile - 1]
        for i in range(blocks_per_tile - 2, -1, -1):
            chosen = jnp.where(key_row < (i + 1) * SLC_BLK, sel_ref[g, blocks_per_tile * j + i], chosen)
        return jnp.where(j * NA_KT + key_row <= q_pos, chosen, NEG_INF)

    def win_mask(j, g):
        kpos = j * NA_KT + key_row
        return jnp.where(kpos <= q_pos, jnp.where(kpos > q_pos - WINDOW, 0.0, NEG_INF), NEG_INF)

    j_hi = (t0 + NA_TQ + NA_KT - 1) // NA_KT
    o_slc = _flash_t(qrs, ks_ref, vst_ref, 0, j_hi, slc_mask, s_ref, p_ref, acc_ref)
    o_win = _flash_t(qrs, kw_ref, vwt_ref, jnp.maximum(t0 - (WINDOW - 1), 0) // NA_KT, j_hi, win_mask,
                     s_ref, p_ref, acc_ref)
    heads_t = []
    for g in range(C_KV_HEADS):
        for hh in range(C_GROUP):
            hd = C_GROUP * g + hh
            cs = slice(hh * NA_TQ, (hh + 1) * NA_TQ)
            heads_t.append(sig_t[hd:hd + 1, :] * o_cmp[g][:, cs]
                           + sig_t[C_HEADS + hd:C_HEADS + hd + 1, :] * o_slc[g][:, cs]
                           + sig_t[2 * C_HEADS + hd:2 * C_HEADS + hd + 1, :] * o_win[g][:, cs])
    yc_ref[0] = (jnp.concatenate(heads_t, axis=0).T * _silu(cz_ref[0])).astype(BF16)


def _overlap_t(n_slc, n_rows):
    cs = np.arange(128) * CMP_STRIDE
    ss = np.arange(n_rows) * SLC_BLK
    ov = (cs[None, :] < ss[:, None] + SLC_BLK) & (cs[None, :] + CMP_LEN > ss[:, None])
    ov &= (np.arange(128)[None, :] < 127) & (np.arange(n_rows)[:, None] < n_slc)
    return ov.astype(np.float32)


def _nsa_prompt(proj, q_hm, qr_hm, kc, vct, ks_hm, rows_all, layer, kw_hm, win_t):
    B, T, _ = proj.shape
    n_slc = T // SLC_BLK
    ovt = jnp.asarray(_overlap_t(n_slc, n_slc))
    qspec = pl.BlockSpec((1, C_HEADS, NA_TQ, C_DH), lambda b, t: (b, 0, t, 0))
    kspec = pl.BlockSpec((1, C_KV_HEADS, T, C_DH), lambda b, t: (b, 0, 0, 0))
    return pl.pallas_call(
        _nsa_prompt_kernel,
        grid=(B, T // NA_TQ),
        in_specs=[qspec, qspec,
                  pl.BlockSpec((1, C_KV_HEADS, 128, C_DH), lambda b, t: (b, 0, 0, 0)),
                  pl.BlockSpec((1, C_KV_HEADS, C_DH, 128), lambda b, t: (b, 0, 0, 0)),
                  kspec, pl.BlockSpec((None, 1, 128, T), lambda b, t: (layer, b, 3, 0)),
                  kspec, pl.BlockSpec((1, 128, T), lambda b, t: (b, 1, 0)),
                  pl.BlockSpec((1, NA_TQ, 128), lambda b, t: (b, t, COL_CG // 128)),
                  pl.BlockSpec((1, NA_TQ, 512), lambda b, t: (b, t, COL_CZ // 512)),
                  pl.BlockSpec((n_slc, 128), lambda b, t: (0, 0))],
        out_specs=pl.BlockSpec((1, NA_TQ, 512), lambda b, t: (b, t, 0)),
        out_shape=jax.ShapeDtypeStruct((B, T, 512), BF16),
        scratch_shapes=[pltpu.VMEM((C_KV_HEADS, NA_KT, NA_M), F32), pltpu.VMEM((C_KV_HEADS, NA_KT, NA_M), BF16),
                        pltpu.VMEM((C_KV_HEADS, FLASH_ACC_ROWS, NA_M), F32),
                        pltpu.VMEM((C_KV_HEADS, n_slc, 1, NA_TQ), F32)],
        compiler_params=_cparams("arbitrary", "arbitrary"),
        name="nsa_prompt",
    )(q_hm, qr_hm, kc, vct, ks_hm, rows_all, kw_hm, win_t, proj, proj, ovt)


N_PAGES = 16
ND_SB = 4


def _nsa_decode_kernel(pt_ref, *refs):
    npg = ND_SB * N_PAGES
    pages = refs[:npg]
    (q_ref, kv_ref, cg_ref, cz_ref, wb_ref, cos_ref, sin_ref, pos_ref, w1_ref, w2_ref, ov_ref, ex_ref, _,
     yc_ref, rows_ref, wn_ref) = refs[npg:]
    past = N_PAGES * PAGE_SIZE
    n_slc = -(-(past + 1) // SLC_BLK)
    cur = past // SLC_BLK
    cos = cos_ref[...]
    sin = sin_ref[...]

    def rope(x):
        return x * cos + jnp.concatenate([x[:, C_DH // 2:], x[:, :C_DH // 2]], axis=1) * sin

    def row_dot(a_bf, b_row):
        return jnp.sum(a_bf.astype(F32) * b_row.astype(BF16).astype(F32), axis=-1, keepdims=True)

    row8 = lax.broadcasted_iota(jnp.int32, (C_HEADS, 128), 0)
    lane = lax.broadcasted_iota(jnp.int32, (C_HEADS, 128), 1)
    g0 = row8 < C_GROUP
    g0h = g0[:, 0:C_DH]

    def spread(x):
        return jnp.concatenate([jnp.where(g0h, x, 0.0), jnp.where(g0h, 0.0, x)], axis=1).astype(BF16)

    def gather(x):
        return jnp.where(g0h, x[:, 0:C_DH], x[:, C_DH:2 * C_DH])

    def column_tile(row, reps):
        return jnp.concatenate([jnp.broadcast_to(row, (128, 128)).T] * reps, axis=1)

    rows_per_page = PAGE_SIZE // CMP_STRIDE

    def by_offset(x_t):
        return jnp.swapaxes(x_t.T.reshape(rows_per_page, CMP_STRIDE, 128), 0, 1)

    keys_by_offset = [by_offset(pg[0:128, :]) for pg in pages]
    vals_by_offset = [by_offset(pg[128:256, :]) for pg in pages]
    m_rows = ND_SB * 128

    def load_l(l):
        return jnp.concatenate([jnp.concatenate([t[l] for t in keys_by_offset], axis=0),
                                jnp.concatenate([t[l] for t in vals_by_offset], axis=0)], axis=1)

    kcvc_all = _compress(load_l, pos_ref, w1_ref, w2_ref, m_rows).astype(BF16)

    sbs = range(ND_SB)
    spages = [pages[sb * N_PAGES:(sb + 1) * N_PAGES] for sb in sbs]
    kcvc = [kcvc_all[sb * 128:(sb + 1) * 128] for sb in sbs]
    kv = [kv_ref[sb] for sb in sbs]
    ks_new = [rope(kv[sb][4:6]) for sb in sbs]
    kw_new = [rope(kv[sb][8:10]) for sb in sbs]
    q8 = [q_ref[sb] * ATT_SCALE for sb in sbs]
    qs = [spread(q8[sb]) for sb in sbs]
    qrs = [spread(rope(q8[sb])) for sb in sbs]
    ks_row = [jnp.concatenate([ks_new[sb][0:1], ks_new[sb][1:2]], axis=1) for sb in sbs]
    vs_row = [jnp.concatenate([kv[sb][6:7], kv[sb][7:8]], axis=1) for sb in sbs]
    kw_row = [jnp.concatenate([kw_new[sb][0:1], kw_new[sb][1:2]], axis=1) for sb in sbs]
    vw_row = [jnp.concatenate([kv[sb][10:11], kv[sb][11:12]], axis=1) for sb in sbs]

    ok = (CMP_STRIDE * lane + (CMP_LEN - 1)) <= past
    s = [lax.dot_general(qs[sb], kcvc[sb][:, 0:128], NT_DIMS, preferred_element_type=F32) for sb in sbs]
    p = []
    for sb in sbs:
        sm = jnp.where(ok, s[sb], NEG_INF)
        e = jnp.where(ok, jnp.exp(sm - jnp.max(sm, axis=-1, keepdims=True)), 0.0)
        p.append(e / jnp.maximum(jnp.sum(e, axis=-1, keepdims=True), 1e-30))
    o_cmp = [gather(jnp.dot(p[sb].astype(BF16), kcvc[sb][:, 128:256], preferred_element_type=F32)) for sb in sbs]

    imp = []
    for sb in sbs:
        ps0 = jnp.sum(jnp.where(g0, p[sb], 0.0), axis=0, keepdims=True)
        ps1 = jnp.sum(jnp.where(g0, 0.0, p[sb]), axis=0, keepdims=True)
        imp.append(jnp.dot(jnp.where(g0, ps0, ps1), ov_ref[...], precision=lax.Precision.HIGHEST,
                           preferred_element_type=F32))
    sel = [_rank_select(jnp.where(lane < n_slc, _forced_score(lane, cur, imp[sb]), -2.0), lane, n_slc, 1)
           for sb in sbs]
    chosen = [jnp.dot(sel[sb].astype(BF16), ex_ref[...], preferred_element_type=F32) for sb in sbs]

    s_pg = [[None] * N_PAGES for _ in sbs]
    ok_pg = [[None] * N_PAGES for _ in sbs]
    for i in range(N_PAGES):
        for sb in sbs:
            ok_pg[sb][i] = chosen[sb][:, i * PAGE_SIZE:(i + 1) * PAGE_SIZE] > 0.5
            sp = jnp.dot(qrs[sb], spages[sb][i][256:384, :].astype(BF16), preferred_element_type=F32)
            s_pg[sb][i] = jnp.where(ok_pg[sb][i], sp, NEG_INF)
    new_ok = [sel[sb][:, cur:cur + 1] > 0.5 for sb in sbs]
    mx, den, acc = [], [], []
    for sb in sbs:
        s_new = jnp.where(new_ok[sb], row_dot(qrs[sb], ks_row[sb]), NEG_INF)
        m = s_new
        for sp in s_pg[sb]:
            m = jnp.maximum(m, jnp.max(sp, axis=-1, keepdims=True))
        p_new = jnp.where(new_ok[sb], jnp.exp(s_new - m), 0.0)
        mx.append(m)
        den.append(p_new)
        acc.append(p_new * vs_row[sb])
    for i in range(N_PAGES):
        for sb in sbs:
            pp = jnp.where(ok_pg[sb][i], jnp.exp(s_pg[sb][i] - mx[sb]), 0.0)
            den[sb] = den[sb] + jnp.sum(pp, axis=-1, keepdims=True)
            acc[sb] = acc[sb] + lax.dot_general(pp.astype(BF16), spages[sb][i][384:512, :].astype(BF16), NT_DIMS,
                                                preferred_element_type=F32)
    o_slc = [gather(acc[sb] / den[sb]) for sb in sbs]

    nw = wb_ref.shape[2]
    wpos = (past - nw) + lax.broadcasted_iota(jnp.int32, (C_HEADS, nw), 1)
    w_ok = jnp.where(wpos <= past, jnp.where(wpos > past - WINDOW, 1.0, 0.0), 0.0) > 0.5
    sw = [jnp.dot(qrs[sb], wb_ref[sb, 0:128, :].astype(BF16), preferred_element_type=F32) for sb in sbs]
    pw, pw_new = [], []
    for sb in sbs:
        swm = jnp.where(w_ok, sw[sb], NEG_INF)
        sw_new = row_dot(qrs[sb], kw_row[sb])
        mw = jnp.maximum(jnp.max(swm, axis=-1, keepdims=True), sw_new)
        pw.append(jnp.where(w_ok, jnp.exp(swm - mw), 0.0))
        pw_new.append(jnp.exp(sw_new - mw))
    accw = [lax.dot_general(pw[sb].astype(BF16), wb_ref[sb, 128:256, :].astype(BF16), NT_DIMS,
                            preferred_element_type=F32) for sb in sbs]
    wlane = lax.broadcasted_iota(jnp.int32, (256, nw), 1)
    for sb in sbs:
        o_win = gather((accw[sb] + pw_new[sb] * vw_row[sb]) / (jnp.sum(pw[sb], axis=-1, keepdims=True) + pw_new[sb]))
        sig = _sigmoid(cg_ref[sb])
        oc = sig[:, 0:1] * o_cmp[sb] + sig[:, 1:2] * o_slc[sb] + sig[:, 2:3] * o_win
        yc_ref[sb] = (oc * _silu(cz_ref[sb])).astype(BF16)
        rows_ref[sb, 0:4, :] = kv[sb][0:4]
        rows_ref[sb, 4:6, :] = ks_new[sb]
        rows_ref[sb, 6:8, :] = kv[sb][6:8]
        new_cols = jnp.concatenate([column_tile(kw_row[sb], nw // 128), column_tile(vw_row[sb], nw // 128)], axis=0)
        wn_ref[sb] = jnp.where(wlane == nw - 1, new_cols, pltpu.roll(wb_ref[sb], nw - 1, 1))


def _nsa_decode(cache_t, layer, page_table, win_t, q8, kv12, cg83, cz8, cos64, sin64, pos256, w1bd, w2bd,
                new_win_t):
    nb = q8.shape[0]
    nw = win_t.shape[3]
    past = N_PAGES * PAGE_SIZE
    n_slc = -(-(past + 1) // SLC_BLK)
    ov = jnp.asarray(_overlap_t(n_slc, 128).T)
    keys = np.arange(past)
    ex = jnp.asarray((np.arange(128)[:, None] == keys[None, :] // SLC_BLK).astype(np.float32)).astype(BF16)

    def page_spec(sb, i):
        return pl.BlockSpec((None, None, 512, PAGE_SIZE), lambda b, pt: (layer, pt[b * ND_SB + sb, i], 0, 0))

    c3 = lambda n, d: pl.BlockSpec((ND_SB, n, d), lambda b, pt: (b, 0, 0))
    z2 = lambda b, pt: (0, 0)
    grid_spec = pltpu.PrefetchScalarGridSpec(
        num_scalar_prefetch=1,
        grid=(nb // ND_SB,),
        in_specs=[page_spec(sb, i) for sb in range(ND_SB) for i in range(N_PAGES)] + [
            c3(C_HEADS, C_DH), c3(12, C_DH), c3(C_HEADS, 3), c3(C_HEADS, C_DH),
            pl.BlockSpec((None, ND_SB, 256, nw), lambda b, pt: (layer, b, 0, 0)),
            pl.BlockSpec((1, C_DH), z2), pl.BlockSpec((1, C_DH), z2),
            pl.BlockSpec((8, 512), z2),
            pl.BlockSpec((CMP_STRIDE * 256, 512), z2),
            pl.BlockSpec((256, 256), z2),
            pl.BlockSpec((128, 128), z2),
            pl.BlockSpec((128, past), z2),
            pl.BlockSpec(memory_space=pl.ANY)],
        out_specs=[c3(C_HEADS, C_DH), c3(C_HEADS, C_DH),
                   pl.BlockSpec((None, ND_SB, 256, nw), lambda b, pt: (layer, b, 0, 0))],
    )
    return pl.pallas_call(
        _nsa_decode_kernel,
        grid_spec=grid_spec,
        out_shape=[jax.ShapeDtypeStruct((nb, C_HEADS, C_DH), BF16),
                   jax.ShapeDtypeStruct((nb, C_HEADS, C_DH), F32),
                   jax.ShapeDtypeStruct(new_win_t.shape, F32)],
        input_output_aliases={1 + ND_SB * N_PAGES + 12: 2},
        compiler_params=_cparams("arbitrary"),
        name="nsa_decode",
    )(page_table, *([cache_t] * (ND_SB * N_PAGES)), q8, kv12, cg83, cz8, win_t, cos64, sin64, pos256, w1bd, w2bd,
      ov, ex, new_win_t)


def _merge_kernel(ya_ref, yb_ref, yc_ref, mg_ref, x_ref, gate_ref, wb_ref, wo_ref, lg_ref, lb_ref, o_ref):
    merged = None
    for n, y_ref in enumerate((ya_ref, yb_ref, yc_ref)):
        br = jnp.dot(y_ref[0], wb_ref[n], preferred_element_type=F32)
        term = _sigmoid(mg_ref[0, :, n * D_MODEL:(n + 1) * D_MODEL]) * br
        merged = term if merged is None else merged + term
    out = jnp.dot(merged.astype(BF16), wo_ref[...], preferred_element_type=F32)
    z = ALPHA * x_ref[0] + gate_ref[0] * out
    mu = jnp.mean(z, axis=-1, keepdims=True)
    zc = z - mu
    var = jnp.mean(zc * zc, axis=-1, keepdims=True)
    o_ref[0] = zc * lax.rsqrt(var + LN_EPS) * lg_ref[...] + lb_ref[...]


def _merge(ya, yb, yc, proj, x, gate, wb_bf, wo_bf, ln_g, ln_b, tm):
    B, T, _ = x.shape
    ys = pl.BlockSpec((1, tm, 512), lambda b, t: (b, t, 0))
    gs = (pl.BlockSpec((1, 1, D_MODEL), lambda b, t: (b, 0, 0)) if gate.shape[1] == 1
          else pl.BlockSpec((1, tm, D_MODEL), lambda b, t: (b, t, 0)))
    return pl.pallas_call(
        _merge_kernel,
        grid=(B, T // tm),
        in_specs=[ys, ys, ys,
                  pl.BlockSpec((1, tm, 3 * D_MODEL), lambda b, t: (b, t, COL_MG // (3 * D_MODEL))),
                  pl.BlockSpec((1, tm, D_MODEL), lambda b, t: (b, t, 0)),
                  gs,
                  pl.BlockSpec((3, 512, D_MODEL), lambda b, t: (0, 0, 0)),
                  pl.BlockSpec((D_MODEL, D_MODEL), lambda b, t: (0, 0)),
                  pl.BlockSpec((1, D_MODEL), lambda b, t: (0, 0)),
                  pl.BlockSpec((1, D_MODEL), lambda b, t: (0, 0))],
        out_specs=pl.BlockSpec((1, tm, D_MODEL), lambda b, t: (b, t, 0)),
        out_shape=jax.ShapeDtypeStruct((B, T, D_MODEL), F32),
        compiler_params=_cparams("arbitrary", "arbitrary"),
        name="merge",
    )(ya, yb, yc, proj, x, gate, wb_bf, wo_bf, ln_g, ln_b)


def _block_diag(blocks):
    n = len(blocks)
    z = jnp.zeros_like(blocks[0])
    return jnp.concatenate(
        [jnp.concatenate([b if j == i else z for j in range(n)], axis=-1) for i, b in enumerate(blocks)], axis=-2)


def _rope_tables(pos, width):
    inv = ROPE_THETA ** (-jnp.arange(0, C_DH, 2, dtype=F32) / C_DH)
    ang = pos.astype(F32)[:, None] * inv[None, :]
    cos = jnp.cos(ang)
    sin = jnp.sin(ang)
    reps = width // C_DH
    return (jnp.tile(jnp.concatenate([cos, cos], axis=1), (1, reps)),
            jnp.tile(jnp.concatenate([-sin, sin], axis=1), (1, reps)))


def _token_major(x_t, lead):
    kinds = x_t.shape[-2] // (C_KV_HEADS * C_DH)
    x = x_t.reshape(*x_t.shape[:-2], kinds, C_KV_HEADS, C_DH, x_t.shape[-1])
    return jnp.moveaxis(x, -1, lead)


def kernel(x_prompt, x_sample, c_prompt, c_sample, cache_nsa_kv, page_table, state_win_kv, state_hgrn,
           state_rglru, state_conv, ada_w, ada_b, w_in, a_lb, a_norm_g, b_conv_w, b_conv_b, b_gate_a_w,
           b_gate_a_b, b_gate_x_w, b_gate_x_b, b_lambda, c_cmp_pos, c_cmp_w1, c_cmp_w2, w_branch, w_out,
           ln_g, ln_b):
    bp, seq, _ = x_prompt.shape
    bs = x_sample.shape[0]
    n_pages = page_table.shape[1]
    n_pool = cache_nsa_kv.shape[1]
    nw = state_win_kv.shape[2]
    past_len = n_pages * cache_nsa_kv.shape[2]
    assert n_pages == N_PAGES and cache_nsa_kv.shape[2] == PAGE_SIZE and x_sample.shape[1] == 1

    lb_all = jnp.cumsum(jax.nn.softmax(a_lb.astype(F32), axis=0), axis=0)
    lower_bound = lb_all - lb_all[0:1]
    one_minus_lb = 1.0 - lower_bound
    w_in_t = jnp.swapaxes(w_in, 1, 2).astype(BF16)
    w_in_p = jnp.concatenate(
        [w_in_t[:, 4888:7960], w_in_t[:, 0:3584], w_in_t[:, 4376:4888], w_in_t[:, 3584:4352],
         w_in_t[:, 4352:4376], jnp.zeros((DEPTH, N_PROJ - 7960, D_MODEL), BF16)], axis=1)
    wb_bf = w_branch.astype(BF16)
    wo_bf = w_out.astype(BF16)
    cache_t = cache_nsa_kv.transpose(0, 1, 3, 4, 5, 2).reshape(DEPTH, n_pool, 512, PAGE_SIZE)
    win_t = state_win_kv.transpose(0, 1, 3, 4, 5, 2).reshape(DEPTH, bs, 256, nw)
    cos_p, sin_p = _rope_tables(jnp.arange(seq), 128)
    cos_s, sin_s = _rope_tables(past_len + jnp.arange(1), C_DH)

    ada = _ada_all(jnp.concatenate([c_prompt, c_sample], axis=0), ada_w, ada_b)

    xp, xs = x_prompt, x_sample.reshape(1, bs, D_MODEL)
    outs = {k: [] for k in ("nsa_s", "win_p", "hg_p", "rg_p", "rg_s", "cv_p", "cv_s")}
    nsa_p_all = jnp.zeros((DEPTH, bp, 512, seq), F32)
    win_s_all = jnp.zeros((DEPTH, bs, 256, nw), F32)
    hg_s_all = jnp.zeros(state_hgrn.shape, F32)
    for l in range(DEPTH):
        shift, scale, gate = ada[l, :, 0:D_MODEL], ada[l, :, D_MODEL:2 * D_MODEL], ada[l, :, 2 * D_MODEL:]
        rg_w = (b_conv_w[l], b_conv_b[l].reshape(1, -1),
                _block_diag([b_gate_a_w[l, i] for i in range(B_BLOCKS)]).astype(BF16), b_gate_a_b[l].reshape(1, -1),
                _block_diag([b_gate_x_w[l, i] for i in range(B_BLOCKS)]).astype(BF16), b_gate_x_b[l].reshape(1, -1),
                b_lambda[l].reshape(1, -1))
        pos_rows = jnp.concatenate([c_cmp_pos[l, 0], c_cmp_pos[l, 0], c_cmp_pos[l, 1], c_cmp_pos[l, 1]], axis=1)
        pos_rows = jnp.pad(pos_rows.reshape(2, CMP_STRIDE * 256), ((0, 6), (0, 0)))
        w1r = c_cmp_w1[l].reshape(2, CMP_LEN, C_DH, C_DH)
        w1bd = _block_diag([w1r[0], w1r[0], w1r[1], w1r[1]]).astype(BF16).reshape(2, CMP_STRIDE * 256, 256)
        w1bd = jnp.concatenate([w1bd[0], w1bd[1]], axis=1)
        pos256 = _compress_bias(pos_rows, w1bd)
        w2bd = _block_diag([c_cmp_w2[l, 0], c_cmp_w2[l, 0], c_cmp_w2[l, 1], c_cmp_w2[l, 1]]).astype(BF16)
        gain = a_norm_g[l].reshape(1, -1)
        lg, lbias = ln_g[l].reshape(1, -1), ln_b[l].reshape(1, -1)

        proj = _inproj(xp, scale[:bp, None], shift[:bp, None], w_in_p[l], 1024)
        ya, s_new = _hgrn_prompt(proj, lower_bound[l].reshape(1, -1), one_minus_lb[l].reshape(1, -1), gain)
        yb, h_new, cv_new = _rglru_prompt(proj, rg_w)
        nsa_p_all, wrow_t, raw, q_hm, qr_hm, ks_hm, kw_hm = _nsa_prep(proj, cos_p, sin_p, nsa_p_all, l)
        kc, vct = _compress_prompt(raw, pos256, w1bd, w2bd)
        yc = _nsa_prompt(proj, q_hm, qr_hm, kc, vct, ks_hm, nsa_p_all, l, kw_hm, wrow_t)
        xp = _merge(ya, yb, yc, proj, xp, gate[:bp, None], wb_bf[l], wo_bf[l], lg, lbias, 512)
        outs["win_p"].append(wrow_t[:, :, seq - min(WINDOW, seq):])
        outs["hg_p"].append(s_new)
        outs["rg_p"].append(h_new.reshape(bp, B_WIDTH))
        outs["cv_p"].append(cv_new)

        proj_s = _inproj(xs, scale[None, bp:], shift[None, bp:], w_in_p[l], bs).reshape(bs, N_PROJ)
        hg_s_all, ya_s = _hgrn_decode(state_hgrn, l, proj_s, lower_bound[l].reshape(A_HEADS, A_DK, 1),
                                      one_minus_lb[l].reshape(A_HEADS, A_DK, 1), gain, hg_s_all)
        yb_s, h_s, cv_s = _rglru_decode(proj_s, state_conv[l].reshape(bs, -1), state_rglru[l], rg_w)
        q8 = proj_s[:, COL_CQ:COL_CQ + 512].reshape(bs, C_HEADS, C_DH)
        cz8 = proj_s[:, COL_CZ:COL_CZ + 512].reshape(bs, C_HEADS, C_DH)
        kv12 = proj_s[:, COL_CKV:COL_CKV + 768].reshape(bs, 12, C_DH)
        cg83 = proj_s[:, COL_CG:COL_CG + 3 * C_HEADS].reshape(bs, 3, C_HEADS).transpose(0, 2, 1)
        yc_s, rows_s, win_s_all = _nsa_decode(cache_t, l, page_table, win_t, q8, kv12, cg83, cz8, cos_s, sin_s,
                                              pos256, w1bd, w2bd, win_s_all)
        xs = _merge(ya_s.reshape(1, bs, 512), yb_s.reshape(1, bs, 512), yc_s.reshape(1, bs, 512),
                    proj_s.reshape(1, bs, N_PROJ), xs, gate[None, bp:], wb_bf[l], wo_bf[l], lg, lbias, bs)
        outs["nsa_s"].append(rows_s.reshape(bs, 1, 4, C_KV_HEADS, C_DH))
        outs["rg_s"].append(h_s)
        outs["cv_s"].append(cv_s.reshape(bs, CONV_W - 1, B_WIDTH))

    st = {k: jnp.stack(v, axis=0) for k, v in outs.items()}
    return (xp, xs.reshape(bs, 1, D_MODEL), _token_major(nsa_p_all, 2), st["nsa_s"],
            _token_major(st["win_p"], 2), _token_major(win_s_all, 2),
            st["hg_p"], hg_s_all, st["rg_p"], st["rg_s"], st["cv_p"], st["cv_s"])
```

```python
import numpy as np
import jax
import jax.numpy as jnp
from jax import lax
from jax.experimental import pallas as pl
from jax.experimental.pallas import tpu as pltpu

F32 = jnp.float32
BF16 = jnp.bfloat16

D_MODEL = 1024
DEPTH = 4
PAGE_SIZE = 128
A_HEADS = 4
A_DK = 128
A_CHUNK = 64
B_WIDTH = 512
B_BLOCKS = 8
CONV_W = 4
LRU_C = 8.0
C_HEADS = 8
C_KV_HEADS = 2
C_GROUP = 4
C_DH = 64
CMP_LEN = 32
CMP_STRIDE = 16
SLC_BLK = 64
SLC_TOPN = 16
WINDOW = 512
ROPE_THETA = 10000.0
ATT_SCALE = C_DH ** -0.5
LOG2_E = 1.4426950408889634
ALPHA = (2 * DEPTH) ** 0.25
LN_EPS = 1e-5
NEG_INF = -1e30
FORCED_SCORE = 1e9

COL_MG, COL_AQ, COL_AF, COL_AI, COL_AZ = 0, 3072, 3584, 4096, 4608
COL_BX, COL_BZ, COL_CQ, COL_CZ, COL_CKV, COL_CG = 5120, 5632, 6144, 6656, 7168, 7936
N_PROJ = 8064
PROJ_TN = 2688

VMEM_LIMIT = 56 * 1024 * 1024

NT_DIMS = (((1,), (1,)), ((), ()))
TN_DIMS = (((0,), (0,)), ((), ()))


def _cparams(*sem):
    return pltpu.CompilerParams(dimension_semantics=sem, vmem_limit_bytes=VMEM_LIMIT)


def _sigmoid(x):
    return jax.nn.sigmoid(x)


def _silu(x):
    return x * jax.nn.sigmoid(x)


def _forget_gate(af, lb, one_minus_lb):
    e = jnp.exp(-jnp.abs(af))
    r = 1.0 / (1.0 + e)
    er = e * r
    pos = af >= 0.0
    return lb + one_minus_lb * jnp.where(pos, r, er), one_minus_lb * jnp.where(pos, er, r)


def _ada_kernel(c_ref, w_ref, b_ref, o_ref):
    o_ref[0] = jnp.dot(c_ref[...], w_ref[0].astype(BF16), preferred_element_type=F32) + b_ref[0]


def _ada_all(c_all, ada_w, ada_b):
    nb = c_all.shape[0]
    return pl.pallas_call(
        _ada_kernel,
        grid=(DEPTH, 3),
        in_specs=[
            pl.BlockSpec((nb, D_MODEL), lambda l, j: (0, 0)),
            pl.BlockSpec((1, D_MODEL, D_MODEL), lambda l, j: (l, 0, j)),
            pl.BlockSpec((1, 1, D_MODEL), lambda l, j: (l, 0, j)),
        ],
        out_specs=pl.BlockSpec((1, nb, D_MODEL), lambda l, j: (l, 0, j)),
        out_shape=jax.ShapeDtypeStruct((DEPTH, nb, 3 * D_MODEL), F32),
        compiler_params=_cparams("arbitrary", "arbitrary"),
        name="ada",
    )(c_all.astype(BF16), ada_w, ada_b.reshape(DEPTH, 1, 3 * D_MODEL))


def _inproj_kernel(x_ref, sc_ref, sh_ref, w_ref, o_ref, h_ref):
    @pl.when(pl.program_id(2) == 0)
    def _():
        h_ref[...] = (x_ref[0] * (1.0 + sc_ref[0]) + sh_ref[0]).astype(BF16)

    o_ref[0] = lax.dot_general(h_ref[...], w_ref[...], NT_DIMS, preferred_element_type=F32)


def _inproj(x, scale, shift, w_bf, tt):
    B, T, _ = x.shape
    ts = scale.shape[1]
    mod_spec = (pl.BlockSpec((1, 1, D_MODEL), lambda b, t, n: (b, 0, 0)) if ts == 1
                else pl.BlockSpec((1, tt, D_MODEL), lambda b, t, n: (b, t, 0)))
    return pl.pallas_call(
        _inproj_kernel,
        grid=(B, T // tt, N_PROJ // PROJ_TN),
        in_specs=[
            pl.BlockSpec((1, tt, D_MODEL), lambda b, t, n: (b, t, 0)),
            mod_spec, mod_spec,
            pl.BlockSpec((PROJ_TN, D_MODEL), lambda b, t, n: (n, 0)),
        ],
        out_specs=pl.BlockSpec((1, tt, PROJ_TN), lambda b, t, n: (b, t, n)),
        out_shape=jax.ShapeDtypeStruct((B, T, N_PROJ), F32),
        scratch_shapes=[pltpu.VMEM((tt, D_MODEL), BF16)],
        compiler_params=_cparams("arbitrary", "arbitrary", "arbitrary"),
        name="inproj",
    )(x, scale, shift, w_bf)


HG_TT = 128
HG_HP = 4
HG_LEVELS = (1, 2, 4, 8, 16, 32)


def _hgrn_level_table():
    t = np.arange(HG_TT)[:, None]
    s = np.arange(HG_TT)[None, :]
    lvl = np.full((HG_TT, HG_TT), -1, np.int32)
    same = (t // A_CHUNK) == (s // A_CHUNK)
    for i, h in enumerate(HG_LEVELS):
        m = same & ((t // h) == (s // h) + 1) & (((s // h) % 2) == 0)
        lvl[m] = i
    lvl[t == s] = len(HG_LEVELS)
    return lvl


def _hgrn_prompt_kernel(aq_ref, af_ref, ai_ref, az_ref, lb_ref, oml_ref, g_ref, lvl_ref,
                        ya_ref, so_ref, s_ref):
    t = pl.program_id(2)

    @pl.when(t == 0)
    def _():
        s_ref[...] = jnp.zeros_like(s_ref)

    heads = range(HG_HP)
    cols = [slice(hh * A_DK, (hh + 1) * A_DK) for hh in heads]
    lvl = lvl_ref[...]
    row = lax.broadcasted_iota(jnp.int32, (HG_TT, A_DK), 0)
    q = [_silu(aq_ref[0, :, cs]) for cs in cols]
    gates = [_forget_gate(af_ref[0, :, cs], lb_ref[:, cs], oml_ref[:, cs]) for cs in cols]
    k = [gk for _, gk in gates]
    vb = [ai_ref[0, :, cs].astype(BF16) for cs in cols]

    cin = [jnp.log(gf) for gf, _ in gates]
    rex = [jnp.zeros_like(c) for c in cin]
    att = [jnp.where(lvl == len(HG_LEVELS), jnp.sum(q[hh] * k[hh], axis=-1, keepdims=True), 0.0) for hh in heads]
    for i, h in enumerate(HG_LEVELS):
        upper = ((row // h) % 2) == 1
        qh = [(q[hh] * jnp.exp(cin[hh])).astype(BF16) for hh in heads]
        kh = [(k[hh] * jnp.exp(rex[hh])).astype(BF16) for hh in heads]
        a = [lax.dot_general(qh[hh], kh[hh], NT_DIMS, preferred_element_type=F32) for hh in heads]
        for hh in heads:
            att[hh] = jnp.where(lvl == i, a[hh], att[hh])
            tot = cin[hh] + rex[hh]
            cin[hh] = cin[hh] + jnp.where(upper, pltpu.roll(tot, h, 0), 0.0)
            rex[hh] = rex[hh] + jnp.where(upper, 0.0, pltpu.roll(tot, HG_TT - h, 0))

    o = [jnp.dot(att[hh].astype(BF16), vb[hh], preferred_element_type=F32) for hh in heads]
    qb = [(q[hh] * jnp.exp(cin[hh])).astype(BF16) for hh in heads]
    kd = [(k[hh] * jnp.exp(rex[hh])).astype(BF16) for hh in heads]
    tot = [cin[hh] + rex[hh] for hh in heads]
    s = [s_ref[hh] for hh in heads]
    inter = [[] for _ in heads]
    for c in range(HG_TT // A_CHUNK):
        sl = slice(c * A_CHUNK, (c + 1) * A_CHUNK)
        for hh in heads:
            inter[hh].append(jnp.dot(qb[hh][sl], s[hh].astype(BF16), preferred_element_type=F32))
        u = [lax.dot_general(kd[hh][sl], vb[hh][sl], TN_DIMS, preferred_element_type=F32) for hh in heads]
        for hh in heads:
            e_rows = jnp.broadcast_to(jnp.exp(tot[hh][c * A_CHUNK:c * A_CHUNK + 1]), (A_DK, A_DK))
            s[hh] = e_rows.T * s[hh] + u[hh]

    for hh, cs in enumerate(cols):
        oh = o[hh] + jnp.concatenate(inter[hh], axis=0)
        ms = jnp.mean(oh * oh, axis=-1, keepdims=True)
        ya_ref[0, :, cs] = ((oh * lax.rsqrt(ms + LN_EPS)) * g_ref[:, cs] * _silu(az_ref[0, :, cs])).astype(BF16)
        s_ref[hh] = s[hh]

    @pl.when(t == pl.num_programs(2) - 1)
    def _():
        so_ref[0] = s_ref[...]


def _hgrn_prompt(proj, lower_bound, one_minus_lb, gain):
    B, T, _ = proj.shape
    lvl = jnp.asarray(_hgrn_level_table())
    width = HG_HP * A_DK

    def col(base):
        return pl.BlockSpec((1, HG_TT, width), lambda b, h, t: (b, t, base // width + h))

    vec = pl.BlockSpec((1, width), lambda b, h, t: (0, h))
    return pl.pallas_call(
        _hgrn_prompt_kernel,
        grid=(B, A_HEADS // HG_HP, T // HG_TT),
        in_specs=[col(COL_AQ), col(COL_AF), col(COL_AI), col(COL_AZ), vec, vec, vec,
                  pl.BlockSpec((HG_TT, HG_TT), lambda b, h, t: (0, 0))],
        out_specs=[pl.BlockSpec((1, HG_TT, width), lambda b, h, t: (b, t, h)),
                   pl.BlockSpec((1, HG_HP, A_DK, A_DK), lambda b, h, t: (b, h, 0, 0))],
        out_shape=[jax.ShapeDtypeStruct((B, T, A_HEADS * A_DK), BF16),
                   jax.ShapeDtypeStruct((B, A_HEADS, A_DK, A_DK), F32)],
        scratch_shapes=[pltpu.VMEM((HG_HP, A_DK, A_DK), F32)],
        compiler_params=_cparams("arbitrary", "arbitrary", "arbitrary"),
        name="hgrn_prompt",
    )(proj, proj, proj, proj, lower_bound, one_minus_lb, gain, lvl)


HG_SB = 16


def _hgrn_decode_kernel(s_ref, aq_ref, afc_ref, ai_ref, az_ref, lb_ref, oml_ref, g_ref, so_ref, ya_ref):
    qb = _silu(aq_ref[...]).astype(BF16)
    ec, kc = _forget_gate(afc_ref[0, 0], lb_ref[0], oml_ref[0])
    v = ai_ref[...]
    new_states = []
    for r in range(HG_SB):
        sn = ec[:, r:r + 1] * s_ref[r, 0] + kc[:, r:r + 1] * v[r:r + 1, :]
        so_ref[r, 0] = sn
        new_states.append(sn.astype(BF16))
    rows = [jnp.dot(qb, new_states[r], preferred_element_type=F32)[r:r + 1, :] for r in range(HG_SB)]
    o = jnp.concatenate(rows, axis=0)
    ms = jnp.mean(o * o, axis=-1, keepdims=True)
    ya_ref[...] = ((o * lax.rsqrt(ms + LN_EPS)) * g_ref[...] * _silu(az_ref[...])).astype(BF16)


def _skip_ref(kernel_fn, pos):
    def wrapped(*refs):
        return kernel_fn(*refs[:pos], *refs[pos + 1:])

    return wrapped


def _hgrn_decode(state_all, layer, proj_s, lower_bound_c, one_minus_lb_c, gain, new_state_all):
    nb = proj_s.shape[0]
    nblk = nb // HG_SB

    def cols(base):
        a = proj_s[:, base:base + A_HEADS * A_DK].reshape(nblk, HG_SB, A_HEADS, A_DK)
        return a.transpose(2, 0, 3, 1)

    st_spec = pl.BlockSpec((None, HG_SB, 1, A_DK, A_DK), lambda h, i: (layer, i, h, 0, 0))
    col_spec = pl.BlockSpec((1, 1, A_DK, HG_SB), lambda h, i: (h, i, 0, 0))
    cvec = pl.BlockSpec((1, A_DK, 1), lambda h, i: (h, 0, 0))
    return pl.pallas_call(
        _skip_ref(_hgrn_decode_kernel, 8),
        grid=(A_HEADS, nblk),
        in_specs=[st_spec, pl.BlockSpec((HG_SB, A_DK), lambda h, i: (i, COL_AQ // A_DK + h)), col_spec,
                  pl.BlockSpec((HG_SB, A_DK), lambda h, i: (i, COL_AI // A_DK + h)),
                  pl.BlockSpec((HG_SB, A_DK), lambda h, i: (i, COL_AZ // A_DK + h)),
                  cvec, cvec,
                  pl.BlockSpec((1, A_DK), lambda h, i: (0, h)),
                  pl.BlockSpec(memory_space=pl.ANY)],
        out_specs=[st_spec, pl.BlockSpec((HG_SB, A_DK), lambda h, i: (i, h))],
        out_shape=[jax.ShapeDtypeStruct(new_state_all.shape, F32),
                   jax.ShapeDtypeStruct((nb, A_HEADS * A_DK), BF16)],
        input_output_aliases={8: 0},
        compiler_params=_cparams("arbitrary", "arbitrary"),
        name="hgrn_decode",
    )(state_all, proj_s, cols(COL_AF), proj_s, proj_s, lower_bound_c, one_minus_lb_c, gain, new_state_all)


RG_TT = 256


def _rglru_gates(u, ga_ref, gab_ref, gx_ref, gxb_ref, lam_ref):
    ub = u.astype(BF16)
    r = _sigmoid(jnp.dot(ub, ga_ref[...], preferred_element_type=F32) + gab_ref[...])
    ig = _sigmoid(jnp.dot(ub, gx_ref[...], preferred_element_type=F32) + gxb_ref[...])
    nl = -lam_ref[...]
    softplus = jnp.maximum(nl, 0.0) + jnp.log1p(jnp.exp(-jnp.abs(nl)))
    a = jnp.exp(-LRU_C * r * softplus)
    return a, jnp.sqrt(1.0 - a * a) * (ig * u)


def _rglru_prompt_kernel(bx_ref, bz_ref, cw_ref, cb_ref, ga_ref, gab_ref, gx_ref, gxb_ref, lam_ref,
                         yb_ref, h_ref, cv_ref, ext_ref, hc_ref):
    t = pl.program_id(1)

    @pl.when(t == 0)
    def _():
        ext_ref[0:8, :] = jnp.zeros((8, B_WIDTH), F32)
        hc_ref[...] = jnp.zeros_like(hc_ref)

    x = bx_ref[0]
    ext_ref[8:8 + RG_TT, :] = x
    u = cb_ref[...] + (ext_ref[5:5 + RG_TT, :] * cw_ref[0:1, :] + ext_ref[6:6 + RG_TT, :] * cw_ref[1:2, :]
                       + ext_ref[7:7 + RG_TT, :] * cw_ref[2:3, :] + x * cw_ref[3:4, :])
    ext_ref[0:8, :] = x[RG_TT - 8:RG_TT, :]
    a, bt = _rglru_gates(u, ga_ref, gab_ref, gx_ref, gxb_ref, lam_ref)
    row = lax.broadcasted_iota(jnp.int32, (RG_TT, 128), 0)
    z = bz_ref[0]
    for c in range(B_WIDTH // 128):
        cs = slice(c * 128, (c + 1) * 128)
        ac = a[:, cs]
        bc = bt[:, cs] + jnp.where(row == 0, ac * hc_ref[0:1, cs], 0.0)
        s = 1
        while s < RG_TT:
            keep = row >= s
            bc = jnp.where(keep, ac * pltpu.roll(bc, s, 0) + bc, bc)
            ac = jnp.where(keep, ac * pltpu.roll(ac, s, 0), ac)
            s *= 2
        hc_ref[0:1, cs] = bc[RG_TT - 1:RG_TT, :]
        yb_ref[0, :, cs] = (bc * _silu(z[:, cs])).astype(BF16)

    @pl.when(t == pl.num_programs(1) - 1)
    def _():
        h_ref[0] = hc_ref[0:1, :]
        cv_ref[0] = x[RG_TT - (CONV_W - 1):RG_TT, :]


def _rglru_weight_specs(idx):
    return [pl.BlockSpec((CONV_W, B_WIDTH), idx), pl.BlockSpec((1, B_WIDTH), idx),
            pl.BlockSpec((B_WIDTH, B_WIDTH), idx), pl.BlockSpec((1, B_WIDTH), idx),
            pl.BlockSpec((B_WIDTH, B_WIDTH), idx), pl.BlockSpec((1, B_WIDTH), idx),
            pl.BlockSpec((1, B_WIDTH), idx)]


def _rglru_prompt(proj, w):
    B, T, _ = proj.shape
    return pl.pallas_call(
        _rglru_prompt_kernel,
        grid=(B, T // RG_TT),
        in_specs=[pl.BlockSpec((1, RG_TT, B_WIDTH), lambda b, t: (b, t, COL_BX // B_WIDTH)),
                  pl.BlockSpec((1, RG_TT, B_WIDTH), lambda b, t: (b, t, COL_BZ // B_WIDTH))]
        + _rglru_weight_specs(lambda b, t: (0, 0)),
        out_specs=[pl.BlockSpec((1, RG_TT, B_WIDTH), lambda b, t: (b, t, 0)),
                   pl.BlockSpec((1, 1, B_WIDTH), lambda b, t: (b, 0, 0)),
                   pl.BlockSpec((1, CONV_W - 1, B_WIDTH), lambda b, t: (b, 0, 0))],
        out_shape=[jax.ShapeDtypeStruct((B, T, B_WIDTH), BF16),
                   jax.ShapeDtypeStruct((B, 1, B_WIDTH), F32),
                   jax.ShapeDtypeStruct((B, CONV_W - 1, B_WIDTH), F32)],
        scratch_shapes=[pltpu.VMEM((8 + RG_TT, B_WIDTH), F32), pltpu.VMEM((8, B_WIDTH), F32)],
        compiler_params=_cparams("arbitrary", "arbitrary"),
        name="rglru_prompt",
    )(proj, proj, *w)


def _rglru_decode_kernel(bx_ref, bz_ref, cv0_ref, h0_ref, cw_ref, cb_ref, ga_ref, gab_ref, gx_ref, gxb_ref,
                         lam_ref, yb_ref, h_ref, cv_ref):
    x = bx_ref[...]
    c0 = cv0_ref[:, 0:B_WIDTH]
    c1 = cv0_ref[:, B_WIDTH:2 * B_WIDTH]
    c2 = cv0_ref[:, 2 * B_WIDTH:3 * B_WIDTH]
    u = cb_ref[...] + (c0 * cw_ref[0:1, :] + c1 * cw_ref[1:2, :] + c2 * cw_ref[2:3, :] + x * cw_ref[3:4, :])
    a, bt = _rglru_gates(u, ga_ref, gab_ref, gx_ref, gxb_ref, lam_ref)
    h = a * h0_ref[...] + bt
    h_ref[...] = h
    yb_ref[...] = (h * _silu(bz_ref[...])).astype(BF16)
    cv_ref[:, 0:B_WIDTH] = c1
    cv_ref[:, B_WIDTH:2 * B_WIDTH] = c2
    cv_ref[:, 2 * B_WIDTH:3 * B_WIDTH] = x


def _rglru_decode(proj_s, conv0, h0, w):
    nb = proj_s.shape[0]
    z2 = lambda i: (0, 0)
    return pl.pallas_call(
        _rglru_decode_kernel,
        grid=(1,),
        in_specs=[pl.BlockSpec((nb, B_WIDTH), lambda i: (0, COL_BX // B_WIDTH)),
                  pl.BlockSpec((nb, B_WIDTH), lambda i: (0, COL_BZ // B_WIDTH)),
                  pl.BlockSpec((nb, 3 * B_WIDTH), z2), pl.BlockSpec((nb, B_WIDTH), z2)]
        + _rglru_weight_specs(z2),
        out_specs=[pl.BlockSpec((nb, B_WIDTH), z2), pl.BlockSpec((nb, B_WIDTH), z2),
                   pl.BlockSpec((nb, 3 * B_WIDTH), z2)],
        out_shape=[jax.ShapeDtypeStruct((nb, B_WIDTH), BF16), jax.ShapeDtypeStruct((nb, B_WIDTH), F32),
                   jax.ShapeDtypeStruct((nb, 3 * B_WIDTH), F32)],
        compiler_params=_cparams("arbitrary"),
        name="rglru_decode",
    )(proj_s, proj_s, conv0, h0, *w)


def _gelu_tanh(x):
    return 0.5 * x * (1.0 + jnp.tanh(0.7978845608028654 * (x + 0.044715 * (x * x * x))))


def _compress(load_l, bias_ref, w1_ref, w2_ref, m):
    x = jnp.concatenate([load_l(l).astype(BF16) for l in range(CMP_STRIDE)], axis=1)
    acc = jnp.dot(x, w1_ref[...], preferred_element_type=F32)
    acc_a = acc[:, 0:256] + bias_ref[0:1, 0:256]
    acc_b = acc[:, 256:512] + bias_ref[1:2, 256:512]
    hid = _gelu_tanh(acc_a + pltpu.roll(acc_b, m - 1, 0))
    return jnp.dot(hid.astype(BF16), w2_ref[...], preferred_element_type=F32)


def _compress_bias_kernel(pos_ref, w1_ref, o_ref):
    o_ref[...] = jnp.dot(pos_ref[...], w1_ref[...].astype(F32), precision=lax.Precision.HIGHEST,
                         preferred_element_type=F32)


def _compress_bias(pos_rows, w1cat):
    return pl.pallas_call(
        _compress_bias_kernel,
        grid=(1,),
        in_specs=[pl.BlockSpec(pos_rows.shape, lambda i: (0, 0)), pl.BlockSpec(w1cat.shape, lambda i: (0, 0))],
        out_specs=pl.BlockSpec((8, 512), lambda i: (0, 0)),
        out_shape=jax.ShapeDtypeStruct((8, 512), F32),
        compiler_params=_cparams("arbitrary"),
        name="compress_bias",
    )(pos_rows, w1cat)


def _rank_select(score, idx, n_rows, axis):
    rank = jnp.zeros_like(score)
    for jp in range(n_rows):
        other = score[jp:jp + 1, :] if axis == 0 else score[:, jp:jp + 1]
        rank = rank + jnp.where(other > score, 1.0, jnp.where(other == score, jnp.where(idx > jp, 1.0, 0.0), 0.0))
    return jnp.where(rank < float(SLC_TOPN), 1.0, 0.0)


def _forced_score(idx, cur, imp):
    return jnp.where(idx == 0, FORCED_SCORE,
                     jnp.where(idx == cur, FORCED_SCORE,
                               jnp.where(idx == cur - 1, FORCED_SCORE, jnp.where(idx <= cur, imp, -1.0))))


NP_TT = 512


def _nsa_prep_kernel(cq_ref, kv01_ref, kv23_ref, kv45_ref, cos_ref, sin_ref,
                     rows_t_ref, win_t_ref, raw_ref, q_ref, qr_ref, ks_ref, kw_ref):
    cos = cos_ref[...]
    sin = sin_ref[...]
    lane = lax.broadcasted_iota(jnp.int32, cos.shape, 1)
    first = (lane % C_DH) < (C_DH // 2)

    def rope(x):
        return x * cos + jnp.where(first, pltpu.roll(x, 128 - C_DH // 2, 1), pltpu.roll(x, C_DH // 2, 1)) * sin

    def heads(ref, x, h0):
        ref[0, h0] = x[:, 0:C_DH].astype(BF16)
        ref[0, h0 + 1] = x[:, C_DH:2 * C_DH].astype(BF16)

    cq = cq_ref[0]
    for c in range(C_HEADS // 2):
        x = cq[:, c * 128:(c + 1) * 128]
        heads(q_ref, x * ATT_SCALE, 2 * c)
        heads(qr_ref, rope(x * (ATT_SCALE * LOG2_E)), 2 * c)
    kv01 = kv01_ref[0]
    kv23 = kv23_ref[0]
    kv45 = kv45_ref[0]
    ks = rope(kv23[:, 0:128])
    kw = rope(kv45[:, 0:128])
    raw_ref[0] = kv01
    heads(ks_ref, ks, 0)
    heads(kw_ref, kw, 0)
    rows_t_ref[0, 0:128, :] = kv01[:, 0:128].T
    rows_t_ref[0, 128:256, :] = kv01[:, 128:256].T
    rows_t_ref[0, 256:384, :] = ks.T
    rows_t_ref[0, 384:512, :] = kv23[:, 128:256].T
    win_t_ref[0, 0:128, :] = kw.T
    win_t_ref[0, 128:256, :] = kv45[:, 128:256].T


def _nsa_prep(proj, cos128, sin128, rows_all, layer):
    B, T, _ = proj.shape
    hm = lambda n: pl.BlockSpec((1, n, NP_TT, C_DH), lambda b, t: (b, 0, t, 0))
    hs = lambda n: jax.ShapeDtypeStruct((B, n, T, C_DH), BF16)
    kvb = COL_CKV // 256
    return pl.pallas_call(
        _skip_ref(_nsa_prep_kernel, 6),
        grid=(B, T // NP_TT),
        in_specs=[pl.BlockSpec((1, NP_TT, 512), lambda b, t: (b, t, COL_CQ // 512)),
                  pl.BlockSpec((1, NP_TT, 256), lambda b, t: (b, t, kvb)),
                  pl.BlockSpec((1, NP_TT, 256), lambda b, t: (b, t, kvb + 1)),
                  pl.BlockSpec((1, NP_TT, 256), lambda b, t: (b, t, kvb + 2)),
                  pl.BlockSpec((NP_TT, 128), lambda b, t: (t, 0)),
                  pl.BlockSpec((NP_TT, 128), lambda b, t: (t, 0)),
                  pl.BlockSpec(memory_space=pl.ANY)],
        out_specs=[pl.BlockSpec((None, 1, 512, NP_TT), lambda b, t: (layer, b, 0, t)),
                   pl.BlockSpec((1, 256, NP_TT), lambda b, t: (b, 0, t)),
                   pl.BlockSpec((1, NP_TT, 256), lambda b, t: (b, t, 0)),
                   hm(8), hm(8), hm(2), hm(2)],
        out_shape=[jax.ShapeDtypeStruct(rows_all.shape, F32), jax.ShapeDtypeStruct((B, 256, T), F32),
                   jax.ShapeDtypeStruct((B, T, 256), F32), hs(8), hs(8), hs(2), hs(2)],
        input_output_aliases={6: 0},
        compiler_params=_cparams("arbitrary", "arbitrary"),
        name="nsa_prep",
    )(proj, proj, proj, proj, cos128, sin128, rows_all)


def _compress_prompt_kernel(xk_ref, xv_ref, pos_ref, w1_ref, w2_ref, kc_ref, vct_ref):
    def load_l(l):
        return jnp.concatenate([xk_ref[pl.ds(l, 128, stride=CMP_STRIDE), :],
                                xv_ref[pl.ds(l, 128, stride=CMP_STRIDE), :]], axis=1)

    out = _compress(load_l, pos_ref, w1_ref, w2_ref, 128)
    vt = out[:, 128:256].T
    for g in range(C_KV_HEADS):
        kc_ref[0, g] = out[:, g * C_DH:(g + 1) * C_DH].astype(BF16)
        vct_ref[0, g] = vt[g * C_DH:(g + 1) * C_DH, :].astype(BF16)


def _compress_prompt(raw, pos256, w1bd, w2bd):
    B, T, _ = raw.shape
    return pl.pallas_call(
        _compress_prompt_kernel,
        grid=(B,),
        in_specs=[pl.BlockSpec((None, T, 128), lambda b: (b, 0, 0)),
                  pl.BlockSpec((None, T, 128), lambda b: (b, 0, 1)),
                  pl.BlockSpec((8, 512), lambda b: (0, 0)),
                  pl.BlockSpec((CMP_STRIDE * 256, 512), lambda b: (0, 0)),
                  pl.BlockSpec((256, 256), lambda b: (0, 0))],
        out_specs=[pl.BlockSpec((1, C_KV_HEADS, 128, C_DH), lambda b: (b, 0, 0, 0)),
                   pl.BlockSpec((1, C_KV_HEADS, C_DH, 128), lambda b: (b, 0, 0, 0))],
        out_shape=[jax.ShapeDtypeStruct((B, C_KV_HEADS, 128, C_DH), BF16),
                   jax.ShapeDtypeStruct((B, C_KV_HEADS, C_DH, 128), BF16)],
        compiler_params=_cparams("arbitrary"),
        name="compress_prompt",
    )(raw, raw, pos256, w1bd, w2bd)


NA_TQ = 256
NA_KT = 256
NA_M = C_GROUP * NA_TQ
FLASH_ACC_ROWS = C_DH + 16


def _lanes4(x):
    return jnp.concatenate([x] * C_GROUP, axis=1)


def _flash_t(qs, k_ref, vt_ref, j_lo, j_hi, bias_fn, s_ref, p_ref, acc_ref):
    groups = range(C_KV_HEADS)
    ones_rows = jnp.ones((FLASH_ACC_ROWS - C_DH, NA_KT), BF16)

    def scores(j, g):
        off = pl.multiple_of(j * NA_KT, NA_KT)
        return lax.dot_general(k_ref[0, g, pl.ds(off, NA_KT), :], qs[g], NT_DIMS, preferred_element_type=F32)

    def values(j, g):
        off = pl.multiple_of(j * NA_KT, NA_KT)
        vt = vt_ref[0, g * C_DH:(g + 1) * C_DH, pl.ds(off, NA_KT)].astype(BF16)
        return jnp.dot(jnp.concatenate([vt, ones_rows], axis=0), p_ref[g], preferred_element_type=F32)

    acc_ref[...] = jnp.zeros_like(acc_ref)
    p_ref[...] = jnp.zeros_like(p_ref)
    for g in groups:
        s_ref[g] = scores(j_lo, g)
    row = lambda v: tuple(jnp.full((1, NA_M), v, F32) for _ in groups)

    def body(j, carry):
        m, alpha = carry
        j_next = jnp.minimum(j + 1, j_hi - 1)
        j_prev = jnp.maximum(j - 1, j_lo)
        m_out, a_out = [], []
        for g in groups:
            s = s_ref[g] + _lanes4(bias_fn(j, g))
            s_ref[g] = scores(j_next, g)
            acc_ref[g] = alpha[g] * acc_ref[g] + values(j_prev, g)
            m_new = jnp.maximum(m[g], jnp.max(s, axis=0, keepdims=True))
            p_ref[g] = jnp.exp2(s - m_new).astype(BF16)
            m_out.append(m_new)
            a_out.append(jnp.exp2(m[g] - m_new))
        return tuple(m_out), tuple(a_out)

    m, alpha = lax.fori_loop(j_lo, j_hi, body, (row(NEG_INF), row(1.0)))
    outs = []
    for g in groups:
        acc = alpha[g] * acc_ref[g] + values(j_hi - 1, g)
        outs.append(acc[0:C_DH] / acc[C_DH:C_DH + 1])
    return outs


def _nsa_prompt_kernel(q_ref, qr_ref, kc_ref, vct_ref, ks_ref, vst_ref, kw_ref, vwt_ref, cg_ref, cz_ref, ovt_ref,
                       yc_ref, s_ref, p_ref, acc_ref, sel_ref):
    qi = pl.program_id(1)
    t0 = qi * NA_TQ
    n_slc = ovt_ref.shape[0]
    sig_t = _sigmoid(cg_ref[0]).T
    cmp_row = lax.broadcasted_iota(jnp.int32, (128, NA_TQ), 0)
    cmp_q = t0 + lax.broadcasted_iota(jnp.int32, (128, NA_TQ), 1)
    cmp_ok = _lanes4(jnp.where(CMP_STRIDE * cmp_row + (CMP_LEN - 1) <= cmp_q, 1.0, 0.0)) > 0.5
    jidx = lax.broadcasted_iota(jnp.int32, (n_slc, NA_TQ), 0)
    cur = (t0 + lax.broadcasted_iota(jnp.int32, (n_slc, NA_TQ), 1)) // SLC_BLK
    groups = range(C_KV_HEADS)
    qrs = [qr_ref[0, C_GROUP * g:C_GROUP * (g + 1)].reshape(NA_M, C_DH) for g in groups]
    s = [lax.dot_general(kc_ref[0, g], q_ref[0, C_GROUP * g:C_GROUP * (g + 1)].reshape(NA_M, C_DH), NT_DIMS,
                         preferred_element_type=F32) for g in groups]
    p = []
    for g in groups:
        sm = jnp.where(cmp_ok, s[g], NEG_INF)
        e = jnp.where(cmp_ok, jnp.exp(sm - jnp.max(sm, axis=0, keepdims=True)), 0.0)
        p.append(e / jnp.maximum(jnp.sum(e, axis=0, keepdims=True), 1e-30))
    o_cmp = [jnp.dot(vct_ref[0, g], p[g].astype(BF16), preferred_element_type=F32) for g in groups]
    imp = [jnp.dot(ovt_ref[...],
                   p[g][:, 0:NA_TQ] + p[g][:, NA_TQ:2 * NA_TQ] + p[g][:, 2 * NA_TQ:3 * NA_TQ] + p[g][:, 3 * NA_TQ:],
                   precision=lax.Precision.HIGHEST, preferred_element_type=F32) for g in groups]
    for g in groups:
        sel_t = _rank_select(_forced_score(jidx, cur, imp[g]), jidx, n_slc, 0)
        sel_bias = jnp.where(sel_t > 0.5, 0.0, NEG_INF)
        for jj in range(n_slc):
            sel_ref[g, jj] = sel_bias[jj:jj + 1, :]

    key_row = lax.broadcasted_iota(jnp.int32, (NA_KT, NA_TQ), 0)
    q_pos = t0 + lax.broadcasted_iota(jnp.int32, (NA_KT, NA_TQ), 1)
    blocks_per_tile = NA_KT // SLC_BLK

    def slc_mask(j, g):
        chosen = sel_ref[g, blocks_per_tile * j + blocks_per_tile - 1]
        for i in range(blocks_per_tile - 2, -1, -1):
            chosen = jnp.where(key_row < (i + 1) * SLC_BLK, sel_ref[g, blocks_per_tile * j + i], chosen)
        return jnp.where(j * NA_KT + key_row <= q_pos, chosen, NEG_INF)

    def win_mask(j, g):
        kpos = j * NA_KT + key_row
        return jnp.where(kpos <= q_pos, jnp.where(kpos > q_pos - WINDOW, 0.0, NEG_INF), NEG_INF)

    j_hi = (t0 + NA_TQ + NA_KT - 1) // NA_KT
    o_slc = _flash_t(qrs, ks_ref, vst_ref, 0, j_hi, slc_mask, s_ref, p_ref, acc_ref)
    o_win = _flash_t(qrs, kw_ref, vwt_ref, jnp.maximum(t0 - (WINDOW - 1), 0) // NA_KT, j_hi, win_mask,
                     s_ref, p_ref, acc_ref)
    heads_t = []
    for g in range(C_KV_HEADS):
        for hh in range(C_GROUP):
            hd = C_GROUP * g + hh
            cs = slice(hh * NA_TQ, (hh + 1) * NA_TQ)
            heads_t.append(sig_t[hd:hd + 1, :] * o_cmp[g][:, cs]
                           + sig_t[C_HEADS + hd:C_HEADS + hd + 1, :] * o_slc[g][:, cs]
                           + sig_t[2 * C_HEADS + hd:2 * C_HEADS + hd + 1, :] * o_win[g][:, cs])
    yc_ref[0] = (jnp.concatenate(heads_t, axis=0).T * _silu(cz_ref[0])).astype(BF16)


def _overlap_t(n_slc, n_rows):
    cs = np.arange(128) * CMP_STRIDE
    ss = np.arange(n_rows) * SLC_BLK
    ov = (cs[None, :] < ss[:, None] + SLC_BLK) & (cs[None, :] + CMP_LEN > ss[:, None])
    ov &= (np.arange(128)[None, :] < 127) & (np.arange(n_rows)[:, None] < n_slc)
    return ov.astype(np.float32)


def _nsa_prompt(proj, q_hm, qr_hm, kc, vct, ks_hm, rows_all, layer, kw_hm, win_t):
    B, T, _ = proj.shape
    n_slc = T // SLC_BLK
    ovt = jnp.asarray(_overlap_t(n_slc, n_slc))
    qspec = pl.BlockSpec((1, C_HEADS, NA_TQ, C_DH), lambda b, t: (b, 0, t, 0))
    kspec = pl.BlockSpec((1, C_KV_HEADS, T, C_DH), lambda b, t: (b, 0, 0, 0))
    return pl.pallas_call(
        _nsa_prompt_kernel,
        grid=(B, T // NA_TQ),
        in_specs=[qspec, qspec,
                  pl.BlockSpec((1, C_KV_HEADS, 128, C_DH), lambda b, t: (b, 0, 0, 0)),
                  pl.BlockSpec((1, C_KV_HEADS, C_DH, 128), lambda b, t: (b, 0, 0, 0)),
                  kspec, pl.BlockSpec((None, 1, 128, T), lambda b, t: (layer, b, 3, 0)),
                  kspec, pl.BlockSpec((1, 128, T), lambda b, t: (b, 1, 0)),
                  pl.BlockSpec((1, NA_TQ, 128), lambda b, t: (b, t, COL_CG // 128)),
                  pl.BlockSpec((1, NA_TQ, 512), lambda b, t: (b, t, COL_CZ // 512)),
                  pl.BlockSpec((n_slc, 128), lambda b, t: (0, 0))],
        out_specs=pl.BlockSpec((1, NA_TQ, 512), lambda b, t: (b, t, 0)),
        out_shape=jax.ShapeDtypeStruct((B, T, 512), BF16),
        scratch_shapes=[pltpu.VMEM((C_KV_HEADS, NA_KT, NA_M), F32), pltpu.VMEM((C_KV_HEADS, NA_KT, NA_M), BF16),
                        pltpu.VMEM((C_KV_HEADS, FLASH_ACC_ROWS, NA_M), F32),
                        pltpu.VMEM((C_KV_HEADS, n_slc, 1, NA_TQ), F32)],
        compiler_params=_cparams("arbitrary", "arbitrary"),
        name="nsa_prompt",
    )(q_hm, qr_hm, kc, vct, ks_hm, rows_all, kw_hm, win_t, proj, proj, ovt)


N_PAGES = 16
ND_SB = 4


def _nsa_decode_kernel(pt_ref, *refs):
    npg = ND_SB * N_PAGES
    pages = refs[:npg]
    (q_ref, kv_ref, cg_ref, cz_ref, wb_ref, cos_ref, sin_ref, pos_ref, w1_ref, w2_ref, ov_ref, ex_ref, _,
     yc_ref, rows_ref, wn_ref) = refs[npg:]
    past = N_PAGES * PAGE_SIZE
    n_slc = -(-(past + 1) // SLC_BLK)
    cur = past // SLC_BLK
    cos = cos_ref[...]
    sin = sin_ref[...]

    def rope(x):
        return x * cos + jnp.concatenate([x[:, C_DH // 2:], x[:, :C_DH // 2]], axis=1) * sin

    def row_dot(a_bf, b_row):
        return jnp.sum(a_bf.astype(F32) * b_row.astype(BF16).astype(F32), axis=-1, keepdims=True)

    row8 = lax.broadcasted_iota(jnp.int32, (C_HEADS, 128), 0)
    lane = lax.broadcasted_iota(jnp.int32, (C_HEADS, 128), 1)
    g0 = row8 < C_GROUP
    g0h = g0[:, 0:C_DH]

    def spread(x):
        return jnp.concatenate([jnp.where(g0h, x, 0.0), jnp.where(g0h, 0.0, x)], axis=1).astype(BF16)

    def gather(x):
        return jnp.where(g0h, x[:, 0:C_DH], x[:, C_DH:2 * C_DH])

    def column_tile(row, reps):
        return jnp.concatenate([jnp.broadcast_to(row, (128, 128)).T] * reps, axis=1)

    rows_per_page = PAGE_SIZE // CMP_STRIDE

    def by_offset(x_t):
        return jnp.swapaxes(x_t.T.reshape(rows_per_page, CMP_STRIDE, 128), 0, 1)

    keys_by_offset = [by_offset(pg[0:128, :]) for pg in pages]
    vals_by_offset = [by_offset(pg[128:256, :]) for pg in pages]
    m_rows = ND_SB * 128

    def load_l(l):
        return jnp.concatenate([jnp.concatenate([t[l] for t in keys_by_offset], axis=0),
                                jnp.concatenate([t[l] for t in vals_by_offset], axis=0)], axis=1)

    kcvc_all = _compress(load_l, pos_ref, w1_ref, w2_ref, m_rows).astype(BF16)

    sbs = range(ND_SB)
    spages = [pages[sb * N_PAGES:(sb + 1) * N_PAGES] for sb in sbs]
    kcvc = [kcvc_all[sb * 128:(sb + 1) * 128] for sb in sbs]
    kv = [kv_ref[sb] for sb in sbs]
    ks_new = [rope(kv[sb][4:6]) for sb in sbs]
    kw_new = [rope(kv[sb][8:10]) for sb in sbs]
    q8 = [q_ref[sb] * ATT_SCALE for sb in sbs]
    qs = [spread(q8[sb]) for sb in sbs]
    qrs = [spread(rope(q8[sb])) for sb in sbs]
    ks_row = [jnp.concatenate([ks_new[sb][0:1], ks_new[sb][1:2]], axis=1) for sb in sbs]
    vs_row = [jnp.concatenate([kv[sb][6:7], kv[sb][7:8]], axis=1) for sb in sbs]
    kw_row = [jnp.concatenate([kw_new[sb][0:1], kw_new[sb][1:2]], axis=1) for sb in sbs]
    vw_row = [jnp.concatenate([kv[sb][10:11], kv[sb][11:12]], axis=1) for sb in sbs]

    ok = (CMP_STRIDE * lane + (CMP_LEN - 1)) <= past
    s = [lax.dot_general(qs[sb], kcvc[sb][:, 0:128], NT_DIMS, preferred_element_type=F32) for sb in sbs]
    p = []
    for sb in sbs:
        sm = jnp.where(ok, s[sb], NEG_INF)
        e = jnp.where(ok, jnp.exp(sm - jnp.max(sm, axis=-1, keepdims=True)), 0.0)
        p.append(e / jnp.maximum(jnp.sum(e, axis=-1, keepdims=True), 1e-30))
    o_cmp = [gather(jnp.dot(p[sb].astype(BF16), kcvc[sb][:, 128:256], preferred_element_type=F32)) for sb in sbs]

    imp = []
    for sb in sbs:
        ps0 = jnp.sum(jnp.where(g0, p[sb], 0.0), axis=0, keepdims=True)
        ps1 = jnp.sum(jnp.where(g0, 0.0, p[sb]), axis=0, keepdims=True)
        imp.append(jnp.dot(jnp.where(g0, ps0, ps1), ov_ref[...], precision=lax.Precision.HIGHEST,
                           preferred_element_type=F32))
    sel = [_rank_select(jnp.where(lane < n_slc, _forced_score(lane, cur, imp[sb]), -2.0), lane, n_slc, 1)
           for sb in sbs]
    chosen = [jnp.dot(sel[sb].astype(BF16), ex_ref[...], preferred_element_type=F32) for sb in sbs]

    s_pg = [[None] * N_PAGES for _ in sbs]
    ok_pg = [[None] * N_PAGES for _ in sbs]
    for i in range(N_PAGES):
        for sb in sbs:
            ok_pg[sb][i] = chosen[sb][:, i * PAGE_SIZE:(i + 1) * PAGE_SIZE] > 0.5
            sp = jnp.dot(qrs[sb], spages[sb][i][256:384, :].astype(BF16), preferred_element_type=F32)
            s_pg[sb][i] = jnp.where(ok_pg[sb][i], sp, NEG_INF)
    new_ok = [sel[sb][:, cur:cur + 1] > 0.5 for sb in sbs]
    mx, den, acc = [], [], []
    for sb in sbs:
        s_new = jnp.where(new_ok[sb], row_dot(qrs[sb], ks_row[sb]), NEG_INF)
        m = s_new
        for sp in s_pg[sb]:
            m = jnp.maximum(m, jnp.max(sp, axis=-1, keepdims=True))
        p_new = jnp.where(new_ok[sb], jnp.exp(s_new - m), 0.0)
        mx.append(m)
        den.append(p_new)
        acc.append(p_new * vs_row[sb])
    for i in range(N_PAGES):
        for sb in sbs:
            pp = jnp.where(ok_pg[sb][i], jnp.exp(s_pg[sb][i] - mx[sb]), 0.0)
            den[sb] = den[sb] + jnp.sum(pp, axis=-1, keepdims=True)
            acc[sb] = acc[sb] + lax.dot_general(pp.astype(BF16), spages[sb][i][384:512, :].astype(BF16), NT_DIMS,
                                                preferred_element_type=F32)
    o_slc = [gather(acc[sb] / den[sb]) for sb in sbs]

    nw = wb_ref.shape[2]
    wpos = (past - nw) + lax.broadcasted_iota(jnp.int32, (C_HEADS, nw), 1)
    w_ok = jnp.where(wpos <= past, jnp.where(wpos > past - WINDOW, 1.0, 0.0), 0.0) > 0.5
    sw = [jnp.dot(qrs[sb], wb_ref[sb, 0:128, :].astype(BF16), preferred_element_type=F32) for sb in sbs]
    pw, pw_new = [], []
    for sb in sbs:
        swm = jnp.where(w_ok, sw[sb], NEG_INF)
        sw_new = row_dot(qrs[sb], kw_row[sb])
        mw = jnp.maximum(jnp.max(swm, axis=-1, keepdims=True), sw_new)
        pw.append(jnp.where(w_ok, jnp.exp(swm - mw), 0.0))
        pw_new.append(jnp.exp(sw_new - mw))
    accw = [lax.dot_general(pw[sb].astype(BF16), wb_ref[sb, 128:256, :].astype(BF16), NT_DIMS,
                            preferred_element_type=F32) for sb in sbs]
    wlane = lax.broadcasted_iota(jnp.int32, (256, nw), 1)
    for sb in sbs:
        o_win = gather((accw[sb] + pw_new[sb] * vw_row[sb]) / (jnp.sum(pw[sb], axis=-1, keepdims=True) + pw_new[sb]))
        sig = _sigmoid(cg_ref[sb])
        oc = sig[:, 0:1] * o_cmp[sb] + sig[:, 1:2] * o_slc[sb] + sig[:, 2:3] * o_win
        yc_ref[sb] = (oc * _silu(cz_ref[sb])).astype(BF16)
        rows_ref[sb, 0:4, :] = kv[sb][0:4]
        rows_ref[sb, 4:6, :] = ks_new[sb]
        rows_ref[sb, 6:8, :] = kv[sb][6:8]
        new_cols = jnp.concatenate([column_tile(kw_row[sb], nw // 128), column_tile(vw_row[sb], nw // 128)], axis=0)
        wn_ref[sb] = jnp.where(wlane == nw - 1, new_cols, pltpu.roll(wb_ref[sb], nw - 1, 1))


def _nsa_decode(cache_t, layer, page_table, win_t, q8, kv12, cg83, cz8, cos64, sin64, pos256, w1bd, w2bd,
                new_win_t):
    nb = q8.shape[0]
    nw = win_t.shape[3]
    past = N_PAGES * PAGE_SIZE
    n_slc = -(-(past + 1) // SLC_BLK)
    ov = jnp.asarray(_overlap_t(n_slc, 128).T)
    keys = np.arange(past)
    ex = jnp.asarray((np.arange(128)[:, None] == keys[None, :] // SLC_BLK).astype(np.float32)).astype(BF16)

    def page_spec(sb, i):
        return pl.BlockSpec((None, None, 512, PAGE_SIZE), lambda b, pt: (layer, pt[b * ND_SB + sb, i], 0, 0))

    c3 = lambda n, d: pl.BlockSpec((ND_SB, n, d), lambda b, pt: (b, 0, 0))
    z2 = lambda b, pt: (0, 0)
    grid_spec = pltpu.PrefetchScalarGridSpec(
        num_scalar_prefetch=1,
        grid=(nb // ND_SB,),
        in_specs=[page_spec(sb, i) for sb in range(ND_SB) for i in range(N_PAGES)] + [
            c3(C_HEADS, C_DH), c3(12, C_DH), c3(C_HEADS, 3), c3(C_HEADS, C_DH),
            pl.BlockSpec((None, ND_SB, 256, nw), lambda b, pt: (layer, b, 0, 0)),
            pl.BlockSpec((1, C_DH), z2), pl.BlockSpec((1, C_DH), z2),
            pl.BlockSpec((8, 512), z2),
            pl.BlockSpec((CMP_STRIDE * 256, 512), z2),
            pl.BlockSpec((256, 256), z2),
            pl.BlockSpec((128, 128), z2),
            pl.BlockSpec((128, past), z2),
            pl.BlockSpec(memory_space=pl.ANY)],
        out_specs=[c3(C_HEADS, C_DH), c3(C_HEADS, C_DH),
                   pl.BlockSpec((None, ND_SB, 256, nw), lambda b, pt: (layer, b, 0, 0))],
    )
    return pl.pallas_call(
        _nsa_decode_kernel,
        grid_spec=grid_spec,
        out_shape=[jax.ShapeDtypeStruct((nb, C_HEADS, C_DH), BF16),
                   jax.ShapeDtypeStruct((nb, C_HEADS, C_DH), F32),
                   jax.ShapeDtypeStruct(new_win_t.shape, F32)],
        input_output_aliases={1 + ND_SB * N_PAGES + 12: 2},
        compiler_params=_cparams("arbitrary"),
        name="nsa_decode",
    )(page_table, *([cache_t] * (ND_SB * N_PAGES)), q8, kv12, cg83, cz8, win_t, cos64, sin64, pos256, w1bd, w2bd,
      ov, ex, new_win_t)


def _merge_kernel(ya_ref, yb_ref, yc_ref, mg_ref, x_ref, gate_ref, wb_ref, wo_ref, lg_ref, lb_ref, o_ref):
    merged = None
    for n, y_ref in enumerate((ya_ref, yb_ref, yc_ref)):
        br = jnp.dot(y_ref[0], wb_ref[n], preferred_element_type=F32)
        term = _sigmoid(mg_ref[0, :, n * D_MODEL:(n + 1) * D_MODEL]) * br
        merged = term if merged is None else merged + term
    out = jnp.dot(merged.astype(BF16), wo_ref[...], preferred_element_type=F32)
    z = ALPHA * x_ref[0] + gate_ref[0] * out
    mu = jnp.mean(z, axis=-1, keepdims=True)
    zc = z - mu
    var = jnp.mean(zc * zc, axis=-1, keepdims=True)
    o_ref[0] = zc * lax.rsqrt(var + LN_EPS) * lg_ref[...] + lb_ref[...]


def _merge(ya, yb, yc, proj, x, gate, wb_bf, wo_bf, ln_g, ln_b, tm):
    B, T, _ = x.shape
    ys = pl.BlockSpec((1, tm, 512), lambda b, t: (b, t, 0))
    gs = (pl.BlockSpec((1, 1, D_MODEL), lambda b, t: (b, 0, 0)) if gate.shape[1] == 1
          else pl.BlockSpec((1, tm, D_MODEL), lambda b, t: (b, t, 0)))
    return pl.pallas_call(
        _merge_kernel,
        grid=(B, T // tm),
        in_specs=[ys, ys, ys,
                  pl.BlockSpec((1, tm, 3 * D_MODEL), lambda b, t: (b, t, COL_MG // (3 * D_MODEL))),
                  pl.BlockSpec((1, tm, D_MODEL), lambda b, t: (b, t, 0)),
                  gs,
                  pl.BlockSpec((3, 512, D_MODEL), lambda b, t: (0, 0, 0)),
                  pl.BlockSpec((D_MODEL, D_MODEL), lambda b, t: (0, 0)),
                  pl.BlockSpec((1, D_MODEL), lambda b, t: (0, 0)),
                  pl.BlockSpec((1, D_MODEL), lambda b, t: (0, 0))],
        out_specs=pl.BlockSpec((1, tm, D_MODEL), lambda b, t: (b, t, 0)),
        out_shape=jax.ShapeDtypeStruct((B, T, D_MODEL), F32),
        compiler_params=_cparams("arbitrary", "arbitrary"),
        name="merge",
    )(ya, yb, yc, proj, x, gate, wb_bf, wo_bf, ln_g, ln_b)


def _block_diag(blocks):
    n = len(blocks)
    z = jnp.zeros_like(blocks[0])
    return jnp.concatenate(
        [jnp.concatenate([b if j == i else z for j in range(n)], axis=-1) for i, b in enumerate(blocks)], axis=-2)


def _rope_tables(pos, width):
    inv = ROPE_THETA ** (-jnp.arange(0, C_DH, 2, dtype=F32) / C_DH)
    ang = pos.astype(F32)[:, None] * inv[None, :]
    cos = jnp.cos(ang)
    sin = jnp.sin(ang)
    reps = width // C_DH
    return (jnp.tile(jnp.concatenate([cos, cos], axis=1), (1, reps)),
            jnp.tile(jnp.concatenate([-sin, sin], axis=1), (1, reps)))


def _token_major(x_t, lead):
    kinds = x_t.shape[-2] // (C_KV_HEADS * C_DH)
    x = x_t.reshape(*x_t.shape[:-2], kinds, C_KV_HEADS, C_DH, x_t.shape[-1])
    return jnp.moveaxis(x, -1, lead)


def kernel(x_prompt, x_sample, c_prompt, c_sample, cache_nsa_kv, page_table, state_win_kv, state_hgrn,
           state_rglru, state_conv, ada_w, ada_b, w_in, a_lb, a_norm_g, b_conv_w, b_conv_b, b_gate_a_w,
           b_gate_a_b, b_gate_x_w, b_gate_x_b, b_lambda, c_cmp_pos, c_cmp_w1, c_cmp_w2, w_branch, w_out,
           ln_g, ln_b):
    bp, seq, _ = x_prompt.shape
    bs = x_sample.shape[0]
    n_pages = page_table.shape[1]
    n_pool = cache_nsa_kv.shape[1]
    nw = state_win_kv.shape[2]
    past_len = n_pages * cache_nsa_kv.shape[2]
    assert n_pages == N_PAGES and cache_nsa_kv.shape[2] == PAGE_SIZE and x_sample.shape[1] == 1

    lb_all = jnp.cumsum(jax.nn.softmax(a_lb.astype(F32), axis=0), axis=0)
    lower_bound = lb_all - lb_all[0:1]
    one_minus_lb = 1.0 - lower_bound
    w_in_t = jnp.swapaxes(w_in, 1, 2).astype(BF16)
    w_in_p = jnp.concatenate(
        [w_in_t[:, 4888:7960], w_in_t[:, 0:3584], w_in_t[:, 4376:4888], w_in_t[:, 3584:4352],
         w_in_t[:, 4352:4376], jnp.zeros((DEPTH, N_PROJ - 7960, D_MODEL), BF16)], axis=1)
    wb_bf = w_branch.astype(BF16)
    wo_bf = w_out.astype(BF16)
    cache_t = cache_nsa_kv.transpose(0, 1, 3, 4, 5, 2).reshape(DEPTH, n_pool, 512, PAGE_SIZE)
    win_t = state_win_kv.transpose(0, 1, 3, 4, 5, 2).reshape(DEPTH, bs, 256, nw)
    cos_p, sin_p = _rope_tables(jnp.arange(seq), 128)
    cos_s, sin_s = _rope_tables(past_len + jnp.arange(1), C_DH)

    ada = _ada_all(jnp.concatenate([c_prompt, c_sample], axis=0), ada_w, ada_b)

    xp, xs = x_prompt, x_sample.reshape(1, bs, D_MODEL)
    outs = {k: [] for k in ("nsa_s", "win_p", "hg_p", "rg_p", "rg_s", "cv_p", "cv_s")}
    nsa_p_all = jnp.zeros((DEPTH, bp, 512, seq), F32)
    win_s_all = jnp.zeros((DEPTH, bs, 256, nw), F32)
    hg_s_all = jnp.zeros(state_hgrn.shape, F32)
    for l in range(DEPTH):
        shift, scale, gate = ada[l, :, 0:D_MODEL], ada[l, :, D_MODEL:2 * D_MODEL], ada[l, :, 2 * D_MODEL:]
        rg_w = (b_conv_w[l], b_conv_b[l].reshape(1, -1),
                _block_diag([b_gate_a_w[l, i] for i in range(B_BLOCKS)]).astype(BF16), b_gate_a_b[l].reshape(1, -1),
                _block_diag([b_gate_x_w[l, i] for i in range(B_BLOCKS)]).astype(BF16), b_gate_x_b[l].reshape(1, -1),
                b_lambda[l].reshape(1, -1))
        pos_rows = jnp.concatenate([c_cmp_pos[l, 0], c_cmp_pos[l, 0], c_cmp_pos[l, 1], c_cmp_pos[l, 1]], axis=1)
        pos_rows = jnp.pad(pos_rows.reshape(2, CMP_STRIDE * 256), ((0, 6), (0, 0)))
        w1r = c_cmp_w1[l].reshape(2, CMP_LEN, C_DH, C_DH)
        w1bd = _block_diag([w1r[0], w1r[0], w1r[1], w1r[1]]).astype(BF16).reshape(2, CMP_STRIDE * 256, 256)
        w1bd = jnp.concatenate([w1bd[0], w1bd[1]], axis=1)
        pos256 = _compress_bias(pos_rows, w1bd)
        w2bd = _block_diag([c_cmp_w2[l, 0], c_cmp_w2[l, 0], c_cmp_w2[l, 1], c_cmp_w2[l, 1]]).astype(BF16)
        gain = a_norm_g[l].reshape(1, -1)
        lg, lbias = ln_g[l].reshape(1, -1), ln_b[l].reshape(1, -1)

        proj = _inproj(xp, scale[:bp, None], shift[:bp, None], w_in_p[l], 1024)
        ya, s_new = _hgrn_prompt(proj, lower_bound[l].reshape(1, -1), one_minus_lb[l].reshape(1, -1), gain)
        yb, h_new, cv_new = _rglru_prompt(proj, rg_w)
        nsa_p_all, wrow_t, raw, q_hm, qr_hm, ks_hm, kw_hm = _nsa_prep(proj, cos_p, sin_p, nsa_p_all, l)
        kc, vct = _compress_prompt(raw, pos256, w1bd, w2bd)
        yc = _nsa_prompt(proj, q_hm, qr_hm, kc, vct, ks_hm, nsa_p_all, l, kw_hm, wrow_t)
        xp = _merge(ya, yb, yc, proj, xp, gate[:bp, None], wb_bf[l], wo_bf[l], lg, lbias, 512)
        outs["win_p"].append(wrow_t[:, :, seq - min(WINDOW, seq):])
        outs["hg_p"].append(s_new)
        outs["rg_p"].append(h_new.reshape(bp, B_WIDTH))
        outs["cv_p"].append(cv_new)

        proj_s = _inproj(xs, scale[None, bp:], shift[None, bp:], w_in_p[l], bs).reshape(bs, N_PROJ)
        hg_s_all, ya_s = _hgrn_decode(state_hgrn, l, proj_s, lower_bound[l].reshape(A_HEADS, A_DK, 1),
                                      one_minus_lb[l].reshape(A_HEADS, A_DK, 1), gain, hg_s_all)
        yb_s, h_s, cv_s = _rglru_decode(proj_s, state_conv[l].reshape(bs, -1), state_rglru[l], rg_w)
        q8 = proj_s[:, COL_CQ:COL_CQ + 512].reshape(bs, C_HEADS, C_DH)
        cz8 = proj_s[:, COL_CZ:COL_CZ + 512].reshape(bs, C_HEADS, C_DH)
        kv12 = proj_s[:, COL_CKV:COL_CKV + 768].reshape(bs, 12, C_DH)
        cg83 = proj_s[:, COL_CG:COL_CG + 3 * C_HEADS].reshape(bs, 3, C_HEADS).transpose(0, 2, 1)
        yc_s, rows_s, win_s_all = _nsa_decode(cache_t, l, page_table, win_t, q8, kv12, cg83, cz8, cos_s, sin_s,
                                              pos256, w1bd, w2bd, win_s_all)
        xs = _merge(ya_s.reshape(1, bs, 512), yb_s.reshape(1, bs, 512), yc_s.reshape(1, bs, 512),
                    proj_s.reshape(1, bs, N_PROJ), xs, gate[None, bp:], wb_bf[l], wo_bf[l], lg, lbias, bs)
        outs["nsa_s"].append(rows_s.reshape(bs, 1, 4, C_KV_HEADS, C_DH))
        outs["rg_s"].append(h_s)
        outs["cv_s"].append(cv_s.reshape(bs, CONV_W - 1, B_WIDTH))

    st = {k: jnp.stack(v, axis=0) for k, v in outs.items()}
    return (xp, xs.reshape(bs, 1, D_MODEL), _token_major(nsa_p_all, 2), st["nsa_s"],
            _token_major(st["win_p"], 2), _token_major(win_s_all, 2),
            st["hg_p"], hg_s_all, st["rg_p"], st["rg_s"], st["cv_p"], st["cv_s"])
```

```python
import numpy as np
import jax
import jax.numpy as jnp
from jax import lax
from jax.experimental import pallas as pl
from jax.experimental.pallas import tpu as pltpu

F32 = jnp.float32
BF16 = jnp.bfloat16

D_MODEL = 1024
DEPTH = 4
PAGE_SIZE = 128
A_HEADS = 4
A_DK = 128
A_CHUNK = 64
B_WIDTH = 512
B_BLOCKS = 8
CONV_W = 4
LRU_C = 8.0
C_HEADS = 8
C_KV_HEADS = 2
C_GROUP = 4
C_DH = 64
CMP_LEN = 32
CMP_STRIDE = 16
SLC_BLK = 64
SLC_TOPN = 16
WINDOW = 512
ROPE_THETA = 10000.0
ATT_SCALE = C_DH ** -0.5
LOG2_E = 1.4426950408889634
ALPHA = (2 * DEPTH) ** 0.25
LN_EPS = 1e-5
NEG_INF = -1e30
FORCED_SCORE = 1e9

COL_MG, COL_AQ, COL_AF, COL_AI, COL_AZ = 0, 3072, 3584, 4096, 4608
COL_BX, COL_BZ, COL_CQ, COL_CZ, COL_CKV, COL_CG = 5120, 5632, 6144, 6656, 7168, 7936
N_PROJ = 8064
PROJ_TN = 2688

VMEM_LIMIT = 56 * 1024 * 1024

NT_DIMS = (((1,), (1,)), ((), ()))
TN_DIMS = (((0,), (0,)), ((), ()))


def _cparams(*sem):
    return pltpu.CompilerParams(dimension_semantics=sem, vmem_limit_bytes=VMEM_LIMIT)


def _sigmoid(x):
    return jax.nn.sigmoid(x)


def _silu(x):
    return x * jax.nn.sigmoid(x)


def _forget_gate(af, lb, one_minus_lb):
    e = jnp.exp(-jnp.abs(af))
    r = 1.0 / (1.0 + e)
    er = e * r
    pos = af >= 0.0
    return lb + one_minus_lb * jnp.where(pos, r, er), one_minus_lb * jnp.where(pos, er, r)


def _ada_kernel(c_ref, w_ref, b_ref, o_ref):
    o_ref[0] = jnp.dot(c_ref[...], w_ref[0].astype(BF16), preferred_element_type=F32) + b_ref[0]


def _ada_all(c_all, ada_w, ada_b):
    nb = c_all.shape[0]
    return pl.pallas_call(
        _ada_kernel,
        grid=(DEPTH, 3),
        in_specs=[
            pl.BlockSpec((nb, D_MODEL), lambda l, j: (0, 0)),
            pl.BlockSpec((1, D_MODEL, D_MODEL), lambda l, j: (l, 0, j)),
            pl.BlockSpec((1, 1, D_MODEL), lambda l, j: (l, 0, j)),
        ],
        out_specs=pl.BlockSpec((1, nb, D_MODEL), lambda l, j: (l, 0, j)),
        out_shape=jax.ShapeDtypeStruct((DEPTH, nb, 3 * D_MODEL), F32),
        compiler_params=_cparams("arbitrary", "arbitrary"),
        name="ada",
    )(c_all.astype(BF16), ada_w, ada_b.reshape(DEPTH, 1, 3 * D_MODEL))


def _inproj_kernel(x_ref, sc_ref, sh_ref, w_ref, o_ref, h_ref):
    @pl.when(pl.program_id(2) == 0)
    def _():
        h_ref[...] = (x_ref[0] * (1.0 + sc_ref[0]) + sh_ref[0]).astype(BF16)

    o_ref[0] = lax.dot_general(h_ref[...], w_ref[...], NT_DIMS, preferred_element_type=F32)


def _inproj(x, scale, shift, w_bf, tt):
    B, T, _ = x.shape
    ts = scale.shape[1]
    mod_spec = (pl.BlockSpec((1, 1, D_MODEL), lambda b, t, n: (b, 0, 0)) if ts == 1
                else pl.BlockSpec((1, tt, D_MODEL), lambda b, t, n: (b, t, 0)))
    return pl.pallas_call(
        _inproj_kernel,
        grid=(B, T // tt, N_PROJ // PROJ_TN),
        in_specs=[
            pl.BlockSpec((1, tt, D_MODEL), lambda b, t, n: (b, t, 0)),
            mod_spec, mod_spec,
            pl.BlockSpec((PROJ_TN, D_MODEL), lambda b, t, n: (n, 0)),
        ],
        out_specs=pl.BlockSpec((1, tt, PROJ_TN), lambda b, t, n: (b, t, n)),
        out_shape=jax.ShapeDtypeStruct((B, T, N_PROJ), F32),
        scratch_shapes=[pltpu.VMEM((tt, D_MODEL), BF16)],
        compiler_params=_cparams("arbitrary", "arbitrary", "arbitrary"),
        name="inproj",
    )(x, scale, shift, w_bf)


HG_TT = 128
HG_HP = 4
HG_LEVELS = (1, 2, 4, 8, 16, 32)


def _hgrn_level_table():
    t = np.arange(HG_TT)[:, None]
    s = np.arange(HG_TT)[None, :]
    lvl = np.full((HG_TT, HG_TT), -1, np.int32)
    same = (t // A_CHUNK) == (s // A_CHUNK)
    for i, h in enumerate(HG_LEVELS):
        m = same & ((t // h) == (s // h) + 1) & (((s // h) % 2) == 0)
        lvl[m] = i
    lvl[t == s] = len(HG_LEVELS)
    return lvl


def _hgrn_prompt_kernel(aq_ref, af_ref, ai_ref, az_ref, lb_ref, oml_ref, g_ref, lvl_ref,
                        ya_ref, so_ref, s_ref):
    t = pl.program_id(2)

    @pl.when(t == 0)
    def _():
        s_ref[...] = jnp.zeros_like(s_ref)

    heads = range(HG_HP)
    cols = [slice(hh * A_DK, (hh + 1) * A_DK) for hh in heads]
    lvl = lvl_ref[...]
    row = lax.broadcasted_iota(jnp.int32, (HG_TT, A_DK), 0)
    q = [_silu(aq_ref[0, :, cs]) for cs in cols]
    gates = [_forget_gate(af_ref[0, :, cs], lb_ref[:, cs], oml_ref[:, cs]) for cs in cols]
    k = [gk for _, gk in gates]
    vb = [ai_ref[0, :, cs].astype(BF16) for cs in cols]

    cin = [jnp.log(gf) for gf, _ in gates]
    rex = [jnp.zeros_like(c) for c in cin]
    att = [jnp.where(lvl == len(HG_LEVELS), jnp.sum(q[hh] * k[hh], axis=-1, keepdims=True), 0.0) for hh in heads]
    for i, h in enumerate(HG_LEVELS):
        upper = ((row // h) % 2) == 1
        qh = [(q[hh] * jnp.exp(cin[hh])).astype(BF16) for hh in heads]
        kh = [(k[hh] * jnp.exp(rex[hh])).astype(BF16) for hh in heads]
        a = [lax.dot_general(qh[hh], kh[hh], NT_DIMS, preferred_element_type=F32) for hh in heads]
        for hh in heads:
            att[hh] = jnp.where(lvl == i, a[hh], att[hh])
            tot = cin[hh] + rex[hh]
            cin[hh] = cin[hh] + jnp.where(upper, pltpu.roll(tot, h, 0), 0.0)
            rex[hh] = rex[hh] + jnp.where(upper, 0.0, pltpu.roll(tot, HG_TT - h, 0))

    o = [jnp.dot(att[hh].astype(BF16), vb[hh], preferred_element_type=F32) for hh in heads]
    qb = [(q[hh] * jnp.exp(cin[hh])).astype(BF16) for hh in heads]
    kd = [(k[hh] * jnp.exp(rex[hh])).astype(BF16) for hh in heads]
    tot = [cin[hh] + rex[hh] for hh in heads]
    s = [s_ref[hh] for hh in heads]
    inter = [[] for _ in heads]
    for c in range(HG_TT // A_CHUNK):
        sl = slice(c * A_CHUNK, (c + 1) * A_CHUNK)
        for hh in heads:
            inter[hh].append(jnp.dot(qb[hh][sl], s[hh].astype(BF16), preferred_element_type=F32))
        u = [lax.dot_general(kd[hh][sl], vb[hh][sl], TN_DIMS, preferred_element_type=F32) for hh in heads]
        for hh in heads:
            e_rows = jnp.broadcast_to(jnp.exp(tot[hh][c * A_CHUNK:c * A_CHUNK + 1]), (A_DK, A_DK))
            s[hh] = e_rows.T * s[hh] + u[hh]

    for hh, cs in enumerate(cols):
        oh = o[hh] + jnp.concatenate(inter[hh], axis=0)
        ms = jnp.mean(oh * oh, axis=-1, keepdims=True)
        ya_ref[0, :, cs] = ((oh * lax.rsqrt(ms + LN_EPS)) * g_ref[:, cs] * _silu(az_ref[0, :, cs])).astype(BF16)
        s_ref[hh] = s[hh]

    @pl.when(t == pl.num_programs(2) - 1)
    def _():
        so_ref[0] = s_ref[...]


def _hgrn_prompt(proj, lower_bound, one_minus_lb, gain):
    B, T, _ = proj.shape
    lvl = jnp.asarray(_hgrn_level_table())
    width = HG_HP * A_DK

    def col(base):
        return pl.BlockSpec((1, HG_TT, width), lambda b, h, t: (b, t, base // width + h))

    vec = pl.BlockSpec((1, width), lambda b, h, t: (0, h))
    return pl.pallas_call(
        _hgrn_prompt_kernel,
        grid=(B, A_HEADS // HG_HP, T // HG_TT),
        in_specs=[col(COL_AQ), col(COL_AF), col(COL_AI), col(COL_AZ), vec, vec, vec,
                  pl.BlockSpec((HG_TT, HG_TT), lambda b, h, t: (0, 0))],
        out_specs=[pl.BlockSpec((1, HG_TT, width), lambda b, h, t: (b, t, h)),
                   pl.BlockSpec((1, HG_HP, A_DK, A_DK), lambda b, h, t: (b, h, 0, 0))],
        out_shape=[jax.ShapeDtypeStruct((B, T, A_HEADS * A_DK), BF16),
                   jax.ShapeDtypeStruct((B, A_HEADS, A_DK, A_DK), F32)],
        scratch_shapes=[pltpu.VMEM((HG_HP, A_DK, A_DK), F32)],
        compiler_params=_cparams("arbitrary", "arbitrary", "arbitrary"),
        name="hgrn_prompt",
    )(proj, proj, proj, proj, lower_bound, one_minus_lb, gain, lvl)


HG_SB = 16


def _hgrn_decode_kernel(s_ref, aq_ref, afc_ref, ai_ref, az_ref, lb_ref, oml_ref, g_ref, so_ref, ya_ref):
    qb = _silu(aq_ref[...]).astype(BF16)
    ec, kc = _forget_gate(afc_ref[0, 0], lb_ref[0], oml_ref[0])
    v = ai_ref[...]
    new_states = []
    for r in range(HG_SB):
        sn = ec[:, r:r + 1] * s_ref[r, 0] + kc[:, r:r + 1] * v[r:r + 1, :]
        so_ref[r, 0] = sn
        new_states.append(sn.astype(BF16))
    rows = [jnp.dot(qb, new_states[r], preferred_element_type=F32)[r:r + 1, :] for r in range(HG_SB)]
    o = jnp.concatenate(rows, axis=0)
    ms = jnp.mean(o * o, axis=-1, keepdims=True)
    ya_ref[...] = ((o * lax.rsqrt(ms + LN_EPS)) * g_ref[...] * _silu(az_ref[...])).astype(BF16)


def _skip_ref(kernel_fn, pos):
    def wrapped(*refs):
        return kernel_fn(*refs[:pos], *refs[pos + 1:])

    return wrapped


def _hgrn_decode(state_all, layer, proj_s, lower_bound_c, one_minus_lb_c, gain, new_state_all):
    nb = proj_s.shape[0]
    nblk = nb // HG_SB

    def cols(base):
        a = proj_s[:, base:base + A_HEADS * A_DK].reshape(nblk, HG_SB, A_HEADS, A_DK)
        return a.transpose(2, 0, 3, 1)

    st_spec = pl.BlockSpec((None, HG_SB, 1, A_DK, A_DK), lambda h, i: (layer, i, h, 0, 0))
    col_spec = pl.BlockSpec((1, 1, A_DK, HG_SB), lambda h, i: (h, i, 0, 0))
    cvec = pl.BlockSpec((1, A_DK, 1), lambda h, i: (h, 0, 0))
    return pl.pallas_call(
        _skip_ref(_hgrn_decode_kernel, 8),
        grid=(A_HEADS, nblk),
        in_specs=[st_spec, pl.BlockSpec((HG_SB, A_DK), lambda h, i: (i, COL_AQ // A_DK + h)), col_spec,
                  pl.BlockSpec((HG_SB, A_DK), lambda h, i: (i, COL_AI // A_DK + h)),
                  pl.BlockSpec((HG_SB, A_DK), lambda h, i: (i, COL_AZ // A_DK + h)),
                  cvec, cvec,
                  pl.BlockSpec((1, A_DK), lambda h, i: (0, h)),
                  pl.BlockSpec(memory_space=pl.ANY)],
        out_specs=[st_spec, pl.BlockSpec((HG_SB, A_DK), lambda h, i: (i, h))],
        out_shape=[jax.ShapeDtypeStruct(new_state_all.shape, F32),
                   jax.ShapeDtypeStruct((nb, A_HEADS * A_DK), BF16)],
        input_output_aliases={8: 0},
        compiler_params=_cparams("arbitrary", "arbitrary"),
        name="hgrn_decode",
    )(state_all, proj_s, cols(COL_AF), proj_s, proj_s, lower_bound_c, one_minus_lb_c, gain, new_state_all)


RG_TT = 256


def _rglru_gates(u, ga_ref, gab_ref, gx_ref, gxb_ref, lam_ref):
    ub = u.astype(BF16)
    r = _sigmoid(jnp.dot(ub, ga_ref[...], preferred_element_type=F32) + gab_ref[...])
    ig = _sigmoid(jnp.dot(ub, gx_ref[...], preferred_element_type=F32) + gxb_ref[...])
    nl = -lam_ref[...]
    softplus = jnp.maximum(nl, 0.0) + jnp.log1p(jnp.exp(-jnp.abs(nl)))
    a = jnp.exp(-LRU_C * r * softplus)
    return a, jnp.sqrt(1.0 - a * a) * (ig * u)


def _rglru_prompt_kernel(bx_ref, bz_ref, cw_ref, cb_ref, ga_ref, gab_ref, gx_ref, gxb_ref, lam_ref,
                         yb_ref, h_ref, cv_ref, ext_ref, hc_ref):
    t = pl.program_id(1)

    @pl.when(t == 0)
    def _():
        ext_ref[0:8, :] = jnp.zeros((8, B_WIDTH), F32)
        hc_ref[...] = jnp.zeros_like(hc_ref)

    x = bx_ref[0]
    ext_ref[8:8 + RG_TT, :] = x
    u = cb_ref[...] + (ext_ref[5:5 + RG_TT, :] * cw_ref[0:1, :] + ext_ref[6:6 + RG_TT, :] * cw_ref[1:2, :]
                       + ext_ref[7:7 + RG_TT, :] * cw_ref[2:3, :] + x * cw_ref[3:4, :])
    ext_ref[0:8, :] = x[RG_TT - 8:RG_TT, :]
    a, bt = _rglru_gates(u, ga_ref, gab_ref, gx_ref, gxb_ref, lam_ref)
    row = lax.broadcasted_iota(jnp.int32, (RG_TT, 128), 0)
    z = bz_ref[0]
    for c in range(B_WIDTH // 128):
        cs = slice(c * 128, (c + 1) * 128)
        ac = a[:, cs]
        bc = bt[:, cs] + jnp.where(row == 0, ac * hc_ref[0:1, cs], 0.0)
        s = 1
        while s < RG_TT:
            keep = row >= s
            bc = jnp.where(keep, ac * pltpu.roll(bc, s, 0) + bc, bc)
            ac = jnp.where(keep, ac * pltpu.roll(ac, s, 0), ac)
            s *= 2
        hc_ref[0:1, cs] = bc[RG_TT - 1:RG_TT, :]
        yb_ref[0, :, cs] = (bc * _silu(z[:, cs])).astype(BF16)

    @pl.when(t == pl.num_programs(1) - 1)
    def _():
        h_ref[0] = hc_ref[0:1, :]
        cv_ref[0] = x[RG_TT - (CONV_W - 1):RG_TT, :]


def _rglru_weight_specs(idx):
    return [pl.BlockSpec((CONV_W, B_WIDTH), idx), pl.BlockSpec((1, B_WIDTH), idx),
            pl.BlockSpec((B_WIDTH, B_WIDTH), idx), pl.BlockSpec((1, B_WIDTH), idx),
            pl.BlockSpec((B_WIDTH, B_WIDTH), idx), pl.BlockSpec((1, B_WIDTH), idx),
            pl.BlockSpec((1, B_WIDTH), idx)]


def _rglru_prompt(proj, w):
    B, T, _ = proj.shape
    return pl.pallas_call(
        _rglru_prompt_kernel,
        grid=(B, T // RG_TT),
        in_specs=[pl.BlockSpec((1, RG_TT, B_WIDTH), lambda b, t: (b, t, COL_BX // B_WIDTH)),
                  pl.BlockSpec((1, RG_TT, B_WIDTH), lambda b, t: (b, t, COL_BZ // B_WIDTH))]
        + _rglru_weight_specs(lambda b, t: (0, 0)),
        out_specs=[pl.BlockSpec((1, RG_TT, B_WIDTH), lambda b, t: (b, t, 0)),
                   pl.BlockSpec((1, 1, B_WIDTH), lambda b, t: (b, 0, 0)),
                   pl.BlockSpec((1, CONV_W - 1, B_WIDTH), lambda b, t: (b, 0, 0))],
        out_shape=[jax.ShapeDtypeStruct((B, T, B_WIDTH), BF16),
                   jax.ShapeDtypeStruct((B, 1, B_WIDTH), F32),
                   jax.ShapeDtypeStruct((B, CONV_W - 1, B_WIDTH), F32)],
        scratch_shapes=[pltpu.VMEM((8 + RG_TT, B_WIDTH), F32), pltpu.VMEM((8, B_WIDTH), F32)],
        compiler_params=_cparams("arbitrary", "arbitrary"),
        name="rglru_prompt",
    )(proj, proj, *w)


def _rglru_decode_kernel(bx_ref, bz_ref, cv0_ref, h0_ref, cw_ref, cb_ref, ga_ref, gab_ref, gx_ref, gxb_ref,
                         lam_ref, yb_ref, h_ref, cv_ref):
    x = bx_ref[...]
    c0 = cv0_ref[:, 0:B_WIDTH]
    c1 = cv0_ref[:, B_WIDTH:2 * B_WIDTH]
    c2 = cv0_ref[:, 2 * B_WIDTH:3 * B_WIDTH]
    u = cb_ref[...] + (c0 * cw_ref[0:1, :] + c1 * cw_ref[1:2, :] + c2 * cw_ref[2:3, :] + x * cw_ref[3:4, :])
    a, bt = _rglru_gates(u, ga_ref, gab_ref, gx_ref, gxb_ref, lam_ref)
    h = a * h0_ref[...] + bt
    h_ref[...] = h
    yb_ref[...] = (h * _silu(bz_ref[...])).astype(BF16)
    cv_ref[:, 0:B_WIDTH] = c1
    cv_ref[:, B_WIDTH:2 * B_WIDTH] = c2
    cv_ref[:, 2 * B_WIDTH:3 * B_WIDTH] = x


def _rglru_decode(proj_s, conv0, h0, w):
    nb = proj_s.shape[0]
    z2 = lambda i: (0, 0)
    return pl.pallas_call(
        _rglru_decode_kernel,
        grid=(1,),
        in_specs=[pl.BlockSpec((nb, B_WIDTH), lambda i: (0, COL_BX // B_WIDTH)),
                  pl.BlockSpec((nb, B_WIDTH), lambda i: (0, COL_BZ // B_WIDTH)),
                  pl.BlockSpec((nb, 3 * B_WIDTH), z2), pl.BlockSpec((nb, B_WIDTH), z2)]
        + _rglru_weight_specs(z2),
        out_specs=[pl.BlockSpec((nb, B_WIDTH), z2), pl.BlockSpec((nb, B_WIDTH), z2),
                   pl.BlockSpec((nb, 3 * B_WIDTH), z2)],
        out_shape=[jax.ShapeDtypeStruct((nb, B_WIDTH), BF16), jax.ShapeDtypeStruct((nb, B_WIDTH), F32),
                   jax.ShapeDtypeStruct((nb, 3 * B_WIDTH), F32)],
        compiler_params=_cparams("arbitrary"),
        name="rglru_decode",
    )(proj_s, proj_s, conv0, h0, *w)


def _gelu_tanh(x):
    return 0.5 * x * (1.0 + jnp.tanh(0.7978845608028654 * (x + 0.044715 * (x * x * x))))


def _compress(load_l, bias_ref, w1_ref, w2_ref, m):
    x = jnp.concatenate([load_l(l).astype(BF16) for l in range(CMP_STRIDE)], axis=1)
    acc = jnp.dot(x, w1_ref[...], preferred_element_type=F32)
    acc_a = acc[:, 0:256] + bias_ref[0:1, 0:256]
    acc_b = acc[:, 256:512] + bias_ref[1:2, 256:512]
    hid = _gelu_tanh(acc_a + pltpu.roll(acc_b, m - 1, 0))
    return jnp.dot(hid.astype(BF16), w2_ref[...], preferred_element_type=F32)


def _compress_bias_kernel(pos_ref, w1_ref, o_ref):
    o_ref[...] = jnp.dot(pos_ref[...], w1_ref[...].astype(F32), precision=lax.Precision.HIGHEST,
                         preferred_element_type=F32)


def _compress_bias(pos_rows, w1cat):
    return pl.pallas_call(
        _compress_bias_kernel,
        grid=(1,),
        in_specs=[pl.BlockSpec(pos_rows.shape, lambda i: (0, 0)), pl.BlockSpec(w1cat.shape, lambda i: (0, 0))],
        out_specs=pl.BlockSpec((8, 512), lambda i: (0, 0)),
        out_shape=jax.ShapeDtypeStruct((8, 512), F32),
        compiler_params=_cparams("arbitrary"),
        name="compress_bias",
    )(pos_rows, w1cat)


def _rank_select(score, idx, n_rows, axis):
    rank = jnp.zeros_like(score)
    for jp in range(n_rows):
        other = score[jp:jp + 1, :] if axis == 0 else score[:, jp:jp + 1]
        rank = rank + jnp.where(other > score, 1.0, jnp.where(other == score, jnp.where(idx > jp, 1.0, 0.0), 0.0))
    return jnp.where(rank < float(SLC_TOPN), 1.0, 0.0)


def _forced_score(idx, cur, imp):
    return jnp.where(idx == 0, FORCED_SCORE,
                     jnp.where(idx == cur, FORCED_SCORE,
                               jnp.where(idx == cur - 1, FORCED_SCORE, jnp.where(idx <= cur, imp, -1.0))))


NP_TT = 512


def _nsa_prep_kernel(cq_ref, kv01_ref, kv23_ref, kv45_ref, cos_ref, sin_ref,
                     rows_t_ref, win_t_ref, raw_ref, q_ref, qr_ref, ks_ref, kw_ref):
    cos = cos_ref[...]
    sin = sin_ref[...]
    lane = lax.broadcasted_iota(jnp.int32, cos.shape, 1)
    first = (lane % C_DH) < (C_DH // 2)

    def rope(x):
        return x * cos + jnp.where(first, pltpu.roll(x, 128 - C_DH // 2, 1), pltpu.roll(x, C_DH // 2, 1)) * sin

    def heads(ref, x, h0):
        ref[0, h0] = x[:, 0:C_DH].astype(BF16)
        ref[0, h0 + 1] = x[:, C_DH:2 * C_DH].astype(BF16)

    cq = cq_ref[0]
    for c in range(C_HEADS // 2):
        x = cq[:, c * 128:(c + 1) * 128]
        q_ref[0, c] = (x * ATT_SCALE).astype(BF16)
        qr_ref[0, c] = rope(x * (ATT_SCALE * LOG2_E)).astype(BF16)
    kv01 = kv01_ref[0]
    kv23 = kv23_ref[0]
    kv45 = kv45_ref[0]
    ks = rope(kv23[:, 0:128])
    kw = rope(kv45[:, 0:128])
    raw_ref[0] = kv01
    heads(ks_ref, ks, 0)
    heads(kw_ref, kw, 0)
    rows_t_ref[0, 0:128, :] = kv01[:, 0:128].T
    rows_t_ref[0, 128:256, :] = kv01[:, 128:256].T
    rows_t_ref[0, 256:384, :] = ks.T
    rows_t_ref[0, 384:512, :] = kv23[:, 128:256].T
    win_t_ref[0, 0:128, :] = kw.T
    win_t_ref[0, 128:256, :] = kv45[:, 128:256].T


def _nsa_prep(proj, cos128, sin128, rows_all, layer):
    B, T, _ = proj.shape
    hm = lambda n: pl.BlockSpec((1, n, NP_TT, C_DH), lambda b, t: (b, 0, t, 0))
    hs = lambda n: jax.ShapeDtypeStruct((B, n, T, C_DH), BF16)
    qpair = pl.BlockSpec((1, C_HEADS // 2, NP_TT, 128), lambda b, t: (b, 0, t, 0))
    qpairs = jax.ShapeDtypeStruct((B, C_HEADS // 2, T, 128), BF16)
    kvb = COL_CKV // 256
    return pl.pallas_call(
        _skip_ref(_nsa_prep_kernel, 6),
        grid=(B, T // NP_TT),
        in_specs=[pl.BlockSpec((1, NP_TT, 512), lambda b, t: (b, t, COL_CQ // 512)),
                  pl.BlockSpec((1, NP_TT, 256), lambda b, t: (b, t, kvb)),
                  pl.BlockSpec((1, NP_TT, 256), lambda b, t: (b, t, kvb + 1)),
                  pl.BlockSpec((1, NP_TT, 256), lambda b, t: (b, t, kvb + 2)),
                  pl.BlockSpec((NP_TT, 128), lambda b, t: (t, 0)),
                  pl.BlockSpec((NP_TT, 128), lambda b, t: (t, 0)),
                  pl.BlockSpec(memory_space=pl.ANY)],
        out_specs=[pl.BlockSpec((None, 1, 512, NP_TT), lambda b, t: (layer, b, 0, t)),
                   pl.BlockSpec((1, 256, NP_TT), lambda b, t: (b, 0, t)),
                   pl.BlockSpec((1, NP_TT, 256), lambda b, t: (b, t, 0)),
                   qpair, qpair, hm(2), hm(2)],
        out_shape=[jax.ShapeDtypeStruct(rows_all.shape, F32), jax.ShapeDtypeStruct((B, 256, T), F32),
                   jax.ShapeDtypeStruct((B, T, 256), F32), qpairs, qpairs, hs(2), hs(2)],
        input_output_aliases={6: 0},
        compiler_params=_cparams("arbitrary", "arbitrary"),
        name="nsa_prep",
    )(proj, proj, proj, proj, cos128, sin128, rows_all)


def _compress_prompt_kernel(xk_ref, xv_ref, pos_ref, w1_ref, w2_ref, kc_ref, vct_ref):
    def load_l(l):
        return jnp.concatenate([xk_ref[pl.ds(l, 128, stride=CMP_STRIDE), :],
                                xv_ref[pl.ds(l, 128, stride=CMP_STRIDE), :]], axis=1)

    out = _compress(load_l, pos_ref, w1_ref, w2_ref, 128)
    vt = out[:, 128:256].T
    for g in range(C_KV_HEADS):
        kc_ref[0, g] = out[:, g * C_DH:(g + 1) * C_DH].astype(BF16)
        vct_ref[0, g] = vt[g * C_DH:(g + 1) * C_DH, :].astype(BF16)


def _compress_prompt(raw, pos256, w1bd, w2bd):
    B, T, _ = raw.shape
    return pl.pallas_call(
        _compress_prompt_kernel,
        grid=(B,),
        in_specs=[pl.BlockSpec((None, T, 128), lambda b: (b, 0, 0)),
                  pl.BlockSpec((None, T, 128), lambda b: (b, 0, 1)),
                  pl.BlockSpec((8, 512), lambda b: (0, 0)),
                  pl.BlockSpec((CMP_STRIDE * 256, 512), lambda b: (0, 0)),
                  pl.BlockSpec((256, 256), lambda b: (0, 0))],
        out_specs=[pl.BlockSpec((1, C_KV_HEADS, 128, C_DH), lambda b: (b, 0, 0, 0)),
                   pl.BlockSpec((1, C_KV_HEADS, C_DH, 128), lambda b: (b, 0, 0, 0))],
        out_shape=[jax.ShapeDtypeStruct((B, C_KV_HEADS, 128, C_DH), BF16),
                   jax.ShapeDtypeStruct((B, C_KV_HEADS, C_DH, 128), BF16)],
        compiler_params=_cparams("arbitrary"),
        name="compress_prompt",
    )(raw, raw, pos256, w1bd, w2bd)


NA_TQ = 256
NA_KT = 256
NA_M = C_GROUP * NA_TQ
FLASH_ACC_ROWS = C_DH + 16


def _lanes4(x):
    return jnp.concatenate([x] * C_GROUP, axis=1)


def _flash_t(qs, k_ref, vt_ref, j_lo, j_hi, bias_fn, s_ref, p_ref, acc_ref):
    groups = range(C_KV_HEADS)
    ones_rows = jnp.ones((FLASH_ACC_ROWS - C_DH, NA_KT), BF16)

    def scores(j, g):
        off = pl.multiple_of(j * NA_KT, NA_KT)
        return lax.dot_general(k_ref[0, g, pl.ds(off, NA_KT), :], qs[g], NT_DIMS, preferred_element_type=F32)

    def values(j, g):
        off = pl.multiple_of(j * NA_KT, NA_KT)
        vt = vt_ref[0, g * C_DH:(g + 1) * C_DH, pl.ds(off, NA_KT)].astype(BF16)
        return jnp.dot(jnp.concatenate([vt, ones_rows], axis=0), p_ref[g], preferred_element_type=F32)

    acc_ref[...] = jnp.zeros_like(acc_ref)
    p_ref[...] = jnp.zeros_like(p_ref)
    for g in groups:
        s_ref[g] = scores(j_lo, g)
    row = lambda v: tuple(jnp.full((1, NA_M), v, F32) for _ in groups)

    def body(j, carry):
        m, alpha = carry
        j_next = jnp.minimum(j + 1, j_hi - 1)
        j_prev = jnp.maximum(j - 1, j_lo)
        m_out, a_out = [], []
        for g in groups:
            s = s_ref[g] + _lanes4(bias_fn(j, g))
            s_ref[g] = scores(j_next, g)
            acc_ref[g] = alpha[g] * acc_ref[g] + values(j_prev, g)
            m_new = jnp.maximum(m[g], jnp.max(s, axis=0, keepdims=True))
            p_ref[g] = jnp.exp2(s - m_new).astype(BF16)
            m_out.append(m_new)
            a_out.append(jnp.exp2(m[g] - m_new))
        return tuple(m_out), tuple(a_out)

    m, alpha = lax.fori_loop(j_lo, j_hi, body, (row(NEG_INF), row(1.0)))
    outs = []
    for g in groups:
        acc = alpha[g] * acc_ref[g] + values(j_hi - 1, g)
        outs.append(acc[0:C_DH] / acc[C_DH:C_DH + 1])
    return outs


def _nsa_prompt_kernel(q_ref, qr_ref, kc_ref, vct_ref, ks_ref, vst_ref, kw_ref, vwt_ref, cg_ref, cz_ref, ovt_ref,
                       yc_ref, s_ref, p_ref, acc_ref, sel_ref):
    qi = pl.program_id(1)
    t0 = qi * NA_TQ
    n_slc = ovt_ref.shape[0]
    sig_t = _sigmoid(cg_ref[0]).T
    cmp_row = lax.broadcasted_iota(jnp.int32, (128, NA_TQ), 0)
    cmp_q = t0 + lax.broadcasted_iota(jnp.int32, (128, NA_TQ), 1)
    cmp_ok = _lanes4(jnp.where(CMP_STRIDE * cmp_row + (CMP_LEN - 1) <= cmp_q, 1.0, 0.0)) > 0.5
    jidx = lax.broadcasted_iota(jnp.int32, (n_slc, NA_TQ), 0)
    cur = (t0 + lax.broadcasted_iota(jnp.int32, (n_slc, NA_TQ), 1)) // SLC_BLK
    groups = range(C_KV_HEADS)
    def group_heads(ref, g):
        pairs = [ref[0, 2 * g], ref[0, 2 * g + 1]]
        return jnp.concatenate([pr[:, i * C_DH:(i + 1) * C_DH] for pr in pairs for i in range(2)], axis=0)

    qrs = [group_heads(qr_ref, g) for g in groups]
    s = [lax.dot_general(kc_ref[0, g], group_heads(q_ref, g), NT_DIMS, preferred_element_type=F32) for g in groups]
    p = []
    for g in groups:
        sm = jnp.where(cmp_ok, s[g], NEG_INF)
        e = jnp.where(cmp_ok, jnp.exp(sm - jnp.max(sm, axis=0, keepdims=True)), 0.0)
        p.append(e / jnp.maximum(jnp.sum(e, axis=0, keepdims=True), 1e-30))
    o_cmp = [jnp.dot(vct_ref[0, g], p[g].astype(BF16), preferred_element_type=F32) for g in groups]
    imp = [jnp.dot(ovt_ref[...],
                   p[g][:, 0:NA_TQ] + p[g][:, NA_TQ:2 * NA_TQ] + p[g][:, 2 * NA_TQ:3 * NA_TQ] + p[g][:, 3 * NA_TQ:],
                   precision=lax.Precision.HIGHEST, preferred_element_type=F32) for g in groups]
    for g in groups:
        sel_t = _rank_select(_forced_score(jidx, cur, imp[g]), jidx, n_slc, 0)
        sel_bias = jnp.where(sel_t > 0.5, 0.0, NEG_INF)
        for jj in range(n_slc):
            sel_ref[g, jj] = sel_bias[jj:jj + 1, :]

    key_row = lax.broadcasted_iota(jnp.int32, (NA_KT, NA_TQ), 0)
    q_pos = t0 + lax.broadcasted_iota(jnp.int32, (NA_KT, NA_TQ), 1)
    blocks_per_tile = NA_KT // SLC_BLK

    def slc_mask(j, g):
        chosen = sel_ref[g, blocks_per_tile * j + blocks_per_tile - 1]
        for i in range(blocks_per_tile - 2, -1, -1):
            chosen = jnp.where(key_row < (i + 1) * SLC_BLK, sel_ref[g, blocks_per_tile * j + i], chosen)
        return jnp.where(j * NA_KT + key_row <= q_pos, chosen, NEG_INF)

    def win_mask(j, g):
        kpos = j * NA_KT + key_row
        return jnp.where(kpos <= q_pos, jnp.where(kpos > q_pos - WINDOW, 0.0, NEG_INF), NEG_INF)

    j_hi = (t0 + NA_TQ + NA_KT - 1) // NA_KT
    o_slc = _flash_t(qrs, ks_ref, vst_ref, 0, j_hi, slc_mask, s_ref, p_ref, acc_ref)
    o_win = _flash_t(qrs, kw_ref, vwt_ref, jnp.maximum(t0 - (WINDOW - 1), 0) // NA_KT, j_hi, win_mask,
                     s_ref, p_ref, acc_ref)
    heads_t = []
    for g in range(C_KV_HEADS):
        for hh in range(C_GROUP):
            hd = C_GROUP * g + hh
            cs = slice(hh * NA_TQ, (hh + 1) * NA_TQ)
            heads_t.append(sig_t[hd:hd + 1, :] * o_cmp[g][:, cs]
                           + sig_t[C_HEADS + hd:C_HEADS + hd + 1, :] * o_slc[g][:, cs]
                           + sig_t[2 * C_HEADS + hd:2 * C_HEADS + hd + 1, :] * o_win[g][:, cs])
    yc_ref[0] = (jnp.concatenate(heads_t, axis=0).T * _silu(cz_ref[0])).astype(BF16)


def _overlap_t(n_slc, n_rows):
    cs = np.arange(128) * CMP_STRIDE
    ss = np.arange(n_rows) * SLC_BLK
    ov = (cs[None, :] < ss[:, None] + SLC_BLK) & (cs[None, :] + CMP_LEN > ss[:, None])
    ov &= (np.arange(128)[None, :] < 127) & (np.arange(n_rows)[:, None] < n_slc)
    return ov.astype(np.float32)


def _nsa_prompt(proj, q_hm, qr_hm, kc, vct, ks_hm, rows_all, layer, kw_hm, win_t):
    B, T, _ = proj.shape
    n_slc = T // SLC_BLK
    ovt = jnp.asarray(_overlap_t(n_slc, n_slc))
    qspec = pl.BlockSpec((1, C_HEADS // 2, NA_TQ, 128), lambda b, t: (b, 0, t, 0))
    kspec = pl.BlockSpec((1, C_KV_HEADS, T, C_DH), lambda b, t: (b, 0, 0, 0))
    return pl.pallas_call(
        _nsa_prompt_kernel,
        grid=(B, T // NA_TQ),
        in_specs=[qspec, qspec,
                  pl.BlockSpec((1, C_KV_HEADS, 128, C_DH), lambda b, t: (b, 0, 0, 0)),
                  pl.BlockSpec((1, C_KV_HEADS, C_DH, 128), lambda b, t: (b, 0, 0, 0)),
                  kspec, pl.BlockSpec((None, 1, 128, T), lambda b, t: (layer, b, 3, 0)),
                  kspec, pl.BlockSpec((1, 128, T), lambda b, t: (b, 1, 0)),
                  pl.BlockSpec((1, NA_TQ, 128), lambda b, t: (b, t, COL_CG // 128)),
                  pl.BlockSpec((1, NA_TQ, 512), lambda b, t: (b, t, COL_CZ // 512)),
                  pl.BlockSpec((n_slc, 128), lambda b, t: (0, 0))],
        out_specs=pl.BlockSpec((1, NA_TQ, 512), lambda b, t: (b, t, 0)),
        out_shape=jax.ShapeDtypeStruct((B, T, 512), BF16),
        scratch_shapes=[pltpu.VMEM((C_KV_HEADS, NA_KT, NA_M), F32), pltpu.VMEM((C_KV_HEADS, NA_KT, NA_M), BF16),
                        pltpu.VMEM((C_KV_HEADS, FLASH_ACC_ROWS, NA_M), F32),
                        pltpu.VMEM((C_KV_HEADS, n_slc, 1, NA_TQ), F32)],
        compiler_params=_cparams("arbitrary", "arbitrary"),
        name="nsa_prompt",
    )(q_hm, qr_hm, kc, vct, ks_hm, rows_all, kw_hm, win_t, proj, proj, ovt)


N_PAGES = 16
ND_SB = 4


def _nsa_decode_kernel(pt_ref, *refs):
    npg = ND_SB * N_PAGES
    pages = refs[:npg]
    (q_ref, kv_ref, cg_ref, cz_ref, wb_ref, cos_ref, sin_ref, pos_ref, w1_ref, w2_ref, ov_ref, ex_ref, _,
     yc_ref, rows_ref, wn_ref) = refs[npg:]
    past = N_PAGES * PAGE_SIZE
    n_slc = -(-(past + 1) // SLC_BLK)
    cur = past // SLC_BLK
    cos = cos_ref[...]
    sin = sin_ref[...]

    def rope(x):
        return x * cos + jnp.concatenate([x[:, C_DH // 2:], x[:, :C_DH // 2]], axis=1) * sin

    def row_dot(a_bf, b_row):
        return jnp.sum(a_bf.astype(F32) * b_row.astype(BF16).astype(F32), axis=-1, keepdims=True)

    row8 = lax.broadcasted_iota(jnp.int32, (C_HEADS, 128), 0)
    lane = lax.broadcasted_iota(jnp.int32, (C_HEADS, 128), 1)
    g0 = row8 < C_GROUP
    g0h = g0[:, 0:C_DH]

    def spread(x):
        return jnp.concatenate([jnp.where(g0h, x, 0.0), jnp.where(g0h, 0.0, x)], axis=1).astype(BF16)

    def gather(x):
        return jnp.where(g0h, x[:, 0:C_DH], x[:, C_DH:2 * C_DH])

    def column_tile(row, reps):
        return jnp.concatenate([jnp.broadcast_to(row, (128, 128)).T] * reps, axis=1)

    rows_per_page = PAGE_SIZE // CMP_STRIDE

    def by_offset(x_t):
        return jnp.swapaxes(x_t.T.reshape(rows_per_page, CMP_STRIDE, 128), 0, 1)

    keys_by_offset = [by_offset(pg[0:128, :]) for pg in pages]
    vals_by_offset = [by_offset(pg[128:256, :]) for pg in pages]
    m_rows = ND_SB * 128

    def load_l(l):
        return jnp.concatenate([jnp.concatenate([t[l] for t in keys_by_offset], axis=0),
                                jnp.concatenate([t[l] for t in vals_by_offset], axis=0)], axis=1)

    kcvc_all = _compress(load_l, pos_ref, w1_ref, w2_ref, m_rows).astype(BF16)

    sbs = range(ND_SB)
    spages = [pages[sb * N_PAGES:(sb + 1) * N_PAGES] for sb in sbs]
    kcvc = [kcvc_all[sb * 128:(sb + 1) * 128] for sb in sbs]
    kv = [kv_ref[sb] for sb in sbs]
    ks_new = [rope(kv[sb][4:6]) for sb in sbs]
    kw_new = [rope(kv[sb][8:10]) for sb in sbs]
    q8 = [q_ref[sb] * ATT_SCALE for sb in sbs]
    qs = [spread(q8[sb]) for sb in sbs]
    qrs = [spread(rope(q8[sb])) for sb in sbs]
    ks_row = [jnp.concatenate([ks_new[sb][0:1], ks_new[sb][1:2]], axis=1) for sb in sbs]
    vs_row = [jnp.concatenate([kv[sb][6:7], kv[sb][7:8]], axis=1) for sb in sbs]
    kw_row = [jnp.concatenate([kw_new[sb][0:1], kw_new[sb][1:2]], axis=1) for sb in sbs]
    vw_row = [jnp.concatenate([kv[sb][10:11], kv[sb][11:12]], axis=1) for sb in sbs]

    ok = (CMP_STRIDE * lane + (CMP_LEN - 1)) <= past
    s = [lax.dot_general(qs[sb], kcvc[sb][:, 0:128], NT_DIMS, preferred_element_type=F32) for sb in sbs]
    p = []
    for sb in sbs:
        sm = jnp.where(ok, s[sb], NEG_INF)
        e = jnp.where(ok, jnp.exp(sm - jnp.max(sm, axis=-1, keepdims=True)), 0.0)
        p.append(e / jnp.maximum(jnp.sum(e, axis=-1, keepdims=True), 1e-30))
    o_cmp = [gather(jnp.dot(p[sb].astype(BF16), kcvc[sb][:, 128:256], preferred_element_type=F32)) for sb in sbs]

    imp = []
    for sb in sbs:
        ps0 = jnp.sum(jnp.where(g0, p[sb], 0.0), axis=0, keepdims=True)
        ps1 = jnp.sum(jnp.where(g0, 0.0, p[sb]), axis=0, keepdims=True)
        imp.append(jnp.dot(jnp.where(g0, ps0, ps1), ov_ref[...], precision=lax.Precision.HIGHEST,
                           preferred_element_type=F32))
    sel = [_rank_select(jnp.where(lane < n_slc, _forced_score(lane, cur, imp[sb]), -2.0), lane, n_slc, 1)
           for sb in sbs]
    chosen = [jnp.dot(sel[sb].astype(BF16), ex_ref[...], preferred_element_type=F32) for sb in sbs]

    s_pg = [[None] * N_PAGES for _ in sbs]
    ok_pg = [[None] * N_PAGES for _ in sbs]
    for i in range(N_PAGES):
        for sb in sbs:
            ok_pg[sb][i] = chosen[sb][:, i * PAGE_SIZE:(i + 1) * PAGE_SIZE] > 0.5
            sp = jnp.dot(qrs[sb], spages[sb][i][256:384, :].astype(BF16), preferred_element_type=F32)
            s_pg[sb][i] = jnp.where(ok_pg[sb][i], sp, NEG_INF)
    new_ok = [sel[sb][:, cur:cur + 1] > 0.5 for sb in sbs]
    mx, den, acc = [], [], []
    for sb in sbs:
        s_new = jnp.where(new_ok[sb], row_dot(qrs[sb], ks_row[sb]), NEG_INF)
        m = s_new
        for sp in s_pg[sb]:
            m = jnp.maximum(m, jnp.max(sp, axis=-1, keepdims=True))
        p_new = jnp.where(new_ok[sb], jnp.exp(s_new - m), 0.0)
        mx.append(m)
        den.append(p_new)
        acc.append(p_new * vs_row[sb])
    for i in range(N_PAGES):
        for sb in sbs:
            pp = jnp.where(ok_pg[sb][i], jnp.exp(s_pg[sb][i] - mx[sb]), 0.0)
            den[sb] = den[sb] + jnp.sum(pp, axis=-1, keepdims=True)
            acc[sb] = acc[sb] + lax.dot_general(pp.astype(BF16), spages[sb][i][384:512, :].astype(BF16), NT_DIMS,
                                                preferred_element_type=F32)
    o_slc = [gather(acc[sb] / den[sb]) for sb in sbs]

    nw = wb_ref.shape[2]
    wpos = (past - nw) + lax.broadcasted_iota(jnp.int32, (C_HEADS, nw), 1)
    w_ok = jnp.where(wpos <= past, jnp.where(wpos > past - WINDOW, 1.0, 0.0), 0.0) > 0.5
    sw = [jnp.dot(qrs[sb], wb_ref[sb, 0:128, :].astype(BF16), preferred_element_type=F32) for sb in sbs]
    pw, pw_new = [], []
    for sb in sbs:
        swm = jnp.where(w_ok, sw[sb], NEG_INF)
        sw_new = row_dot(qrs[sb], kw_row[sb])
        mw = jnp.maximum(jnp.max(swm, axis=-1, keepdims=True), sw_new)
        pw.append(jnp.where(w_ok, jnp.exp(swm - mw), 0.0))
        pw_new.append(jnp.exp(sw_new - mw))
    accw = [lax.dot_general(pw[sb].astype(BF16), wb_ref[sb, 128:256, :].astype(BF16), NT_DIMS,
                            preferred_element_type=F32) for sb in sbs]
    wlane = lax.broadcasted_iota(jnp.int32, (256, nw), 1)
    for sb in sbs:
        o_win = gather((accw[sb] + pw_new[sb] * vw_row[sb]) / (jnp.sum(pw[sb], axis=-1, keepdims=True) + pw_new[sb]))
        sig = _sigmoid(cg_ref[sb])
        oc = sig[:, 0:1] * o_cmp[sb] + sig[:, 1:2] * o_slc[sb] + sig[:, 2:3] * o_win
        yc_ref[sb] = (oc * _silu(cz_ref[sb])).astype(BF16)
        rows_ref[sb, 0:4, :] = kv[sb][0:4]
        rows_ref[sb, 4:6, :] = ks_new[sb]
        rows_ref[sb, 6:8, :] = kv[sb][6:8]
        new_cols = jnp.concatenate([column_tile(kw_row[sb], nw // 128), column_tile(vw_row[sb], nw // 128)], axis=0)
        wn_ref[sb] = jnp.where(wlane == nw - 1, new_cols, pltpu.roll(wb_ref[sb], nw - 1, 1))


def _nsa_decode(cache_t, layer, page_table, win_t, q8, kv12, cg83, cz8, cos64, sin64, pos256, w1bd, w2bd,
                new_win_t):
    nb = q8.shape[0]
    nw = win_t.shape[3]
    past = N_PAGES * PAGE_SIZE
    n_slc = -(-(past + 1) // SLC_BLK)
    ov = jnp.asarray(_overlap_t(n_slc, 128).T)
    keys = np.arange(past)
    ex = jnp.asarray((np.arange(128)[:, None] == keys[None, :] // SLC_BLK).astype(np.float32)).astype(BF16)

    def page_spec(sb, i):
        return pl.BlockSpec((None, None, 512, PAGE_SIZE), lambda b, pt: (layer, pt[b * ND_SB + sb, i], 0, 0))

    c3 = lambda n, d: pl.BlockSpec((ND_SB, n, d), lambda b, pt: (b, 0, 0))
    z2 = lambda b, pt: (0, 0)
    grid_spec = pltpu.PrefetchScalarGridSpec(
        num_scalar_prefetch=1,
        grid=(nb // ND_SB,),
        in_specs=[page_spec(sb, i) for sb in range(ND_SB) for i in range(N_PAGES)] + [
            c3(C_HEADS, C_DH), c3(12, C_DH), c3(C_HEADS, 3), c3(C_HEADS, C_DH),
            pl.BlockSpec((None, ND_SB, 256, nw), lambda b, pt: (layer, b, 0, 0)),
            pl.BlockSpec((1, C_DH), z2), pl.BlockSpec((1, C_DH), z2),
            pl.BlockSpec((8, 512), z2),
            pl.BlockSpec((CMP_STRIDE * 256, 512), z2),
            pl.BlockSpec((256, 256), z2),
            pl.BlockSpec((128, 128), z2),
            pl.BlockSpec((128, past), z2),
            pl.BlockSpec(memory_space=pl.ANY)],
        out_specs=[c3(C_HEADS, C_DH), c3(C_HEADS, C_DH),
                   pl.BlockSpec((None, ND_SB, 256, nw), lambda b, pt: (layer, b, 0, 0))],
    )
    return pl.pallas_call(
        _nsa_decode_kernel,
        grid_spec=grid_spec,
        out_shape=[jax.ShapeDtypeStruct((nb, C_HEADS, C_DH), BF16),
                   jax.ShapeDtypeStruct((nb, C_HEADS, C_DH), F32),
                   jax.ShapeDtypeStruct(new_win_t.shape, F32)],
        input_output_aliases={1 + ND_SB * N_PAGES + 12: 2},
        compiler_params=_cparams("arbitrary"),
        name="nsa_decode",
    )(page_table, *([cache_t] * (ND_SB * N_PAGES)), q8, kv12, cg83, cz8, win_t, cos64, sin64, pos256, w1bd, w2bd,
      ov, ex, new_win_t)


def _merge_kernel(ya_ref, yb_ref, yc_ref, mg_ref, x_ref, gate_ref, wb_ref, wo_ref, lg_ref, lb_ref, o_ref):
    merged = None
    for n, y_ref in enumerate((ya_ref, yb_ref, yc_ref)):
        br = jnp.dot(y_ref[0], wb_ref[n], preferred_element_type=F32)
        term = _sigmoid(mg_ref[0, :, n * D_MODEL:(n + 1) * D_MODEL]) * br
        merged = term if merged is None else merged + term
    out = jnp.dot(merged.astype(BF16), wo_ref[...], preferred_element_type=F32)
    z = ALPHA * x_ref[0] + gate_ref[0] * out
    mu = jnp.mean(z, axis=-1, keepdims=True)
    zc = z - mu
    var = jnp.mean(zc * zc, axis=-1, keepdims=True)
    o_ref[0] = zc * lax.rsqrt(var + LN_EPS) * lg_ref[...] + lb_ref[...]


def _merge(ya, yb, yc, proj, x, gate, wb_bf, wo_bf, ln_g, ln_b, tm):
    B, T, _ = x.shape
    ys = pl.BlockSpec((1, tm, 512), lambda b, t: (b, t, 0))
    gs = (pl.BlockSpec((1, 1, D_MODEL), lambda b, t: (b, 0, 0)) if gate.shape[1] == 1
          else pl.BlockSpec((1, tm, D_MODEL), lambda b, t: (b, t, 0)))
    return pl.pallas_call(
        _merge_kernel,
        grid=(B, T // tm),
        in_specs=[ys, ys, ys,
                  pl.BlockSpec((1, tm, 3 * D_MODEL), lambda b, t: (b, t, COL_MG // (3 * D_MODEL))),
                  pl.BlockSpec((1, tm, D_MODEL), lambda b, t: (b, t, 0)),
                  gs,
                  pl.BlockSpec((3, 512, D_MODEL), lambda b, t: (0, 0, 0)),
                  pl.BlockSpec((D_MODEL, D_MODEL), lambda b, t: (0, 0)),
                  pl.BlockSpec((1, D_MODEL), lambda b, t: (0, 0)),
                  pl.BlockSpec((1, D_MODEL), lambda b, t: (0, 0))],
        out_specs=pl.BlockSpec((1, tm, D_MODEL), lambda b, t: (b, t, 0)),
        out_shape=jax.ShapeDtypeStruct((B, T, D_MODEL), F32),
        compiler_params=_cparams("arbitrary", "arbitrary"),
        name="merge",
    )(ya, yb, yc, proj, x, gate, wb_bf, wo_bf, ln_g, ln_b)


def _block_diag(blocks):
    n = len(blocks)
    z = jnp.zeros_like(blocks[0])
    return jnp.concatenate(
        [jnp.concatenate([b if j == i else z for j in range(n)], axis=-1) for i, b in enumerate(blocks)], axis=-2)


def _rope_tables(pos, width):
    inv = ROPE_THETA ** (-jnp.arange(0, C_DH, 2, dtype=F32) / C_DH)
    ang = pos.astype(F32)[:, None] * inv[None, :]
    cos = jnp.cos(ang)
    sin = jnp.sin(ang)
    reps = width // C_DH
    return (jnp.tile(jnp.concatenate([cos, cos], axis=1), (1, reps)),
            jnp.tile(jnp.concatenate([-sin, sin], axis=1), (1, reps)))


def _token_major(x_t, lead):
    kinds = x_t.shape[-2] // (C_KV_HEADS * C_DH)
    x = x_t.reshape(*x_t.shape[:-2], kinds, C_KV_HEADS, C_DH, x_t.shape[-1])
    return jnp.moveaxis(x, -1, lead)


def kernel(x_prompt, x_sample, c_prompt, c_sample, cache_nsa_kv, page_table, state_win_kv, state_hgrn,
           state_rglru, state_conv, ada_w, ada_b, w_in, a_lb, a_norm_g, b_conv_w, b_conv_b, b_gate_a_w,
           b_gate_a_b, b_gate_x_w, b_gate_x_b, b_lambda, c_cmp_pos, c_cmp_w1, c_cmp_w2, w_branch, w_out,
           ln_g, ln_b):
    bp, seq, _ = x_prompt.shape
    bs = x_sample.shape[0]
    n_pages = page_table.shape[1]
    n_pool = cache_nsa_kv.shape[1]
    nw = state_win_kv.shape[2]
    past_len = n_pages * cache_nsa_kv.shape[2]
    assert n_pages == N_PAGES and cache_nsa_kv.shape[2] == PAGE_SIZE and x_sample.shape[1] == 1

    lb_all = jnp.cumsum(jax.nn.softmax(a_lb.astype(F32), axis=0), axis=0)
    lower_bound = lb_all - lb_all[0:1]
    one_minus_lb = 1.0 - lower_bound
    w_in_t = jnp.swapaxes(w_in, 1, 2).astype(BF16)
    w_in_p = jnp.concatenate(
        [w_in_t[:, 4888:7960], w_in_t[:, 0:3584], w_in_t[:, 4376:4888], w_in_t[:, 3584:4352],
         w_in_t[:, 4352:4376], jnp.zeros((DEPTH, N_PROJ - 7960, D_MODEL), BF16)], axis=1)
    wb_bf = w_branch.astype(BF16)
    wo_bf = w_out.astype(BF16)
    cache_t = cache_nsa_kv.transpose(0, 1, 3, 4, 5, 2).reshape(DEPTH, n_pool, 512, PAGE_SIZE)
    win_t = state_win_kv.transpose(0, 1, 3, 4, 5, 2).reshape(DEPTH, bs, 256, nw)
    cos_p, sin_p = _rope_tables(jnp.arange(seq), 128)
    cos_s, sin_s = _rope_tables(past_len + jnp.arange(1), C_DH)

    ada = _ada_all(jnp.concatenate([c_prompt, c_sample], axis=0), ada_w, ada_b)

    xp, xs = x_prompt, x_sample.reshape(1, bs, D_MODEL)
    outs = {k: [] for k in ("nsa_s", "win_p", "hg_p", "rg_p", "rg_s", "cv_p", "cv_s")}
    nsa_p_all = jnp.zeros((DEPTH, bp, 512, seq), F32)
    win_s_all = jnp.zeros((DEPTH, bs, 256, nw), F32)
    hg_s_all = jnp.zeros(state_hgrn.shape, F32)
    for l in range(DEPTH):
        shift, scale, gate = ada[l, :, 0:D_MODEL], ada[l, :, D_MODEL:2 * D_MODEL], ada[l, :, 2 * D_MODEL:]
        rg_w = (b_conv_w[l], b_conv_b[l].reshape(1, -1),
                _block_diag([b_gate_a_w[l, i] for i in range(B_BLOCKS)]).astype(BF16), b_gate_a_b[l].reshape(1, -1),
                _block_diag([b_gate_x_w[l, i] for i in range(B_BLOCKS)]).astype(BF16), b_gate_x_b[l].reshape(1, -1),
                b_lambda[l].reshape(1, -1))
        pos_rows = jnp.concatenate([c_cmp_pos[l, 0], c_cmp_pos[l, 0], c_cmp_pos[l, 1], c_cmp_pos[l, 1]], axis=1)
        pos_rows = jnp.pad(pos_rows.reshape(2, CMP_STRIDE * 256), ((0, 6), (0, 0)))
        w1r = c_cmp_w1[l].reshape(2, CMP_LEN, C_DH, C_DH)
        w1bd = _block_diag([w1r[0], w1r[0], w1r[1], w1r[1]]).astype(BF16).reshape(2, CMP_STRIDE * 256, 256)
        w1bd = jnp.concatenate([w1bd[0], w1bd[1]], axis=1)
        pos256 = _compress_bias(pos_rows, w1bd)
        w2bd = _block_diag([c_cmp_w2[l, 0], c_cmp_w2[l, 0], c_cmp_w2[l, 1], c_cmp_w2[l, 1]]).astype(BF16)
        gain = a_norm_g[l].reshape(1, -1)
        lg, lbias = ln_g[l].reshape(1, -1), ln_b[l].reshape(1, -1)

        proj = _inproj(xp, scale[:bp, None], shift[:bp, None], w_in_p[l], 1024)
        ya, s_new = _hgrn_prompt(proj, lower_bound[l].reshape(1, -1), one_minus_lb[l].reshape(1, -1), gain)
        yb, h_new, cv_new = _rglru_prompt(proj, rg_w)
        nsa_p_all, wrow_t, raw, q_hm, qr_hm, ks_hm, kw_hm = _nsa_prep(proj, cos_p, sin_p, nsa_p_all, l)
        kc, vct = _compress_prompt(raw, pos256, w1bd, w2bd)
        yc = _nsa_prompt(proj, q_hm, qr_hm, kc, vct, ks_hm, nsa_p_all, l, kw_hm, wrow_t)
        xp = _merge(ya, yb, yc, proj, xp, gate[:bp, None], wb_bf[l], wo_bf[l], lg, lbias, 512)
        outs["win_p"].append(wrow_t[:, :, seq - min(WINDOW, seq):])
        outs["hg_p"].append(s_new)
        outs["rg_p"].append(h_new.reshape(bp, B_WIDTH))
        outs["cv_p"].append(cv_new)

        proj_s = _inproj(xs, scale[None, bp:], shift[None, bp:], w_in_p[l], bs).reshape(bs, N_PROJ)
        hg_s_all, ya_s = _hgrn_decode(state_hgrn, l, proj_s, lower_bound[l].reshape(A_HEADS, A_DK, 1),
                                      one_minus_lb[l].reshape(A_HEADS, A_DK, 1), gain, hg_s_all)
        yb_s, h_s, cv_s = _rglru_decode(proj_s, state_conv[l].reshape(bs, -1), state_rglru[l], rg_w)
        q8 = proj_s[:, COL_CQ:COL_CQ + 512].reshape(bs, C_HEADS, C_DH)
        cz8 = proj_s[:, COL_CZ:COL_CZ + 512].reshape(bs, C_HEADS, C_DH)
        kv12 = proj_s[:, COL_CKV:COL_CKV + 768].reshape(bs, 12, C_DH)
        cg83 = proj_s[:, COL_CG:COL_CG + 3 * C_HEADS].reshape(bs, 3, C_HEADS).transpose(0, 2, 1)
        yc_s, rows_s, win_s_all = _nsa_decode(cache_t, l, page_table, win_t, q8, kv12, cg83, cz8, cos_s, sin_s,
                                              pos256, w1bd, w2bd, win_s_all)
        xs = _merge(ya_s.reshape(1, bs, 512), yb_s.reshape(1, bs, 512), yc_s.reshape(1, bs, 512),
                    proj_s.reshape(1, bs, N_PROJ), xs, gate[None, bp:], wb_bf[l], wo_bf[l], lg, lbias, bs)
        outs["nsa_s"].append(rows_s.reshape(bs, 1, 4, C_KV_HEADS, C_DH))
        outs["rg_s"].append(h_s)
        outs["cv_s"].append(cv_s.reshape(bs, CONV_W - 1, B_WIDTH))

    st = {k: jnp.stack(v, axis=0) for k, v in outs.items()}
    return (xp, xs.reshape(bs, 1, D_MODEL), _token_major(nsa_p_all, 2), st["nsa_s"],
            _token_major(st["win_p"], 2), _token_major(win_s_all, 2),
            st["hg_p"], hg_s_all, st["rg_p"], st["rg_s"], st["cv_p"], st["cv_s"])
```
